```python
import math
import jax, jax.numpy as jnp
from jax import lax
import numpy as np

D_MODEL = 1024
BATCH = 16
SEQ = 4096
DEPTH = 2
DEC_BATCH = 2
DEC_SEQ = 8192
PAST_LEN = 128

HEAD_DIM = 64
N_BRANCHES = 4
BRANCH_WIDTH = D_MODEL // 4
RET_HEADS = BRANCH_WIDTH // HEAD_DIM
RET_CHUNK = 128
ROPE_BASE = 10000.0
CONV_CH = BRANCH_WIDTH
CONV_WIDTH = 31
GRID_W = 64
NA_HEADS = BRANCH_WIDTH // HEAD_DIM
NA_WIN_ROWS = 8
NA_WIN_COLS = 16
NA_QBLK_COLS = 16
NA_KBLK_COLS = NA_QBLK_COLS + NA_WIN_COLS
SWA_Q_HEADS = BRANCH_WIDTH // HEAD_DIM
SWA_KV_HEADS = SWA_Q_HEADS // 2
SWA_WINDOW = 128
SWA_BLOCK = 128
T5_BUCKETS = 32
T5_MAX_DIST = 128
D_FF = 4 * D_MODEL
NORM_EPS = 1e-6
NEG_INF = -1e30
IN_SIZES = (BRANCH_WIDTH, BRANCH_WIDTH, BRANCH_WIDTH, BRANCH_WIDTH,
            CONV_CH, CONV_CH,
            BRANCH_WIDTH, BRANCH_WIDTH, BRANCH_WIDTH,
            BRANCH_WIDTH, SWA_KV_HEADS * HEAD_DIM, SWA_KV_HEADS * HEAD_DIM)
IN_COLS = sum(IN_SIZES)

kernel_name = "hybrid_gated_encoder_trunk"


def rmsnorm(x, gain):
    xf = x.astype(jnp.float32)
    y = xf * lax.rsqrt(jnp.mean(xf * xf, axis=-1, keepdims=True) + NORM_EPS)
    return (y * gain.astype(jnp.float32)).astype(x.dtype)


def layernorm(x, gain, bias):
    xf = x.astype(jnp.float32)
    mu = jnp.mean(xf, axis=-1, keepdims=True)
    var = jnp.mean(jnp.square(xf - mu), axis=-1, keepdims=True)
    y = (xf - mu) * lax.rsqrt(var + NORM_EPS)
    return (y * gain.astype(jnp.float32) + bias.astype(jnp.float32)).astype(x.dtype)


def rotary(x, pos):
    half = x.shape[-1] // 2
    inv = ROPE_BASE ** (-np.arange(half, dtype=np.float32) / half)
    ang = pos[:, None] * inv[None, :]
    cos = jnp.cos(ang)[None, :, None, :].astype(x.dtype)
    sin = jnp.sin(ang)[None, :, None, :].astype(x.dtype)
    x1, x2 = x[..., :half], x[..., half:]
    return jnp.concatenate([x1 * cos - x2 * sin, x1 * sin + x2 * cos], axis=-1)


def retention_scan(q, k, v, log_g, include_diag):
    B, T, H, dk = q.shape
    dv = v.shape[-1]
    C = RET_CHUNK
    N = T // C

    def chunks(t):
        return t.reshape(B, N, C, H, t.shape[-1]).transpose(0, 3, 1, 2, 4)

    qc, kc, vc = chunks(q), chunks(k), chunks(v)
    pos = np.arange(C, dtype=np.float32)
    diff = pos[:, None] - pos[None, :]
    mask = (diff >= 0) if include_diag else (diff > 0)
    decay = jnp.exp(jnp.where(mask, diff[None] * log_g[:, None, None], -jnp.inf)).astype(q.dtype)
    scores = jnp.einsum('bhncd,bhnmd->bhncm', qc, kc) * decay[:, None]
    o_intra = jnp.einsum('bhncm,bhnme->bhnce', scores, vc)
    zeta = jnp.exp((C - 1 - pos)[None, :] * log_g[:, None]).astype(q.dtype)[:, None, :, None]
    xi = jnp.exp((pos + 1)[None, :] * log_g[:, None]).astype(q.dtype)[:, None, :, None]
    chunk_state = jnp.einsum('bhnmd,bhnme->nbhde', kc * zeta, vc)
    g_chunk = jnp.exp(C * log_g).astype(q.dtype)[None, :, None, None]

    def step(R, u):
        return g_chunk * R + u, R

    _, R_prev = lax.scan(step, jnp.zeros((B, H, dk, dv), q.dtype), chunk_state)
    o_inter = jnp.einsum('bhncd,nbhde->bhnce', qc * xi, R_prev)
    return (o_intra + o_inter).transpose(0, 2, 3, 1, 4).reshape(B, T, H, dv)


def retention_branch(q, k, v, g, decay_logit, gn_gain):
    B, T, _ = q.shape
    pos = jnp.arange(T, dtype=jnp.float32)
    q = rotary(q.reshape(B, T, RET_HEADS, HEAD_DIM), pos) * (HEAD_DIM ** -0.5)
    k = rotary(k.reshape(B, T, RET_HEADS, HEAD_DIM), pos)
    v = v.reshape(B, T, RET_HEADS, HEAD_DIM)
    log_g = jax.nn.log_sigmoid(decay_logit.astype(jnp.float32))
    fwd = retention_scan(q, k, v, log_g[0], True)
    bwd = retention_scan(q[:, ::-1], k[:, ::-1], v[:, ::-1], log_g[1], False)[:, ::-1]
    of = (fwd + bwd).astype(jnp.float32)
    mu = jnp.mean(of, axis=-1, keepdims=True)
    var = jnp.mean(jnp.square(of - mu), axis=-1, keepdims=True)
    o = ((of - mu) * lax.rsqrt(var + NORM_EPS)).reshape(B, T, BRANCH_WIDTH) * gn_gain.astype(jnp.float32)
    return o.astype(g.dtype) * jax.nn.silu(g)


def conv_branch(a, b, dw_kernel, dw_bias, ln_gain, ln_bias):
    u = a * jax.nn.sigmoid(b)
    y = lax.conv_general_dilated(u, dw_kernel[:, None, :], window_strides=(1,),
                                 padding=[(CONV_WIDTH // 2, CONV_WIDTH // 2)],
                                 dimension_numbers=('NWC', 'WIO', 'NWC'),
                                 feature_group_count=CONV_CH) + dw_bias
    return jax.nn.silu(layernorm(y, ln_gain, ln_bias))


def neighborhood_attention(q, k, v, rpb):
    B, T, H, d = q.shape
    rows = T // GRID_W
    wr = min(NA_WIN_ROWS, rows)
    nblk = GRID_W // NA_QBLK_COLS
    qcol = np.arange(GRID_W).reshape(nblk, NA_QBLK_COLS)
    win_start = np.clip(qcol - NA_WIN_COLS // 2, 0, GRID_W - NA_WIN_COLS)
    kblk_start = np.minimum(win_start[:, 0], GRID_W - NA_KBLK_COLS)
    kcol = kblk_start[:, None] + np.arange(NA_KBLK_COLS)
    rel = kcol[:, None, :] - win_start[:, :, None]
    col_mask = (rel >= 0) & (rel < NA_WIN_COLS)
    dc_idx = np.clip(kcol[:, None, :] - qcol[:, :, None] + NA_WIN_COLS - 1, 0, 2 * NA_WIN_COLS - 2)
    kg = k.reshape(B, rows, GRID_W, H, d)[:, :, kcol]
    vg = v.reshape(B, rows, GRID_W, H, d)[:, :, kcol]
    qg = q.reshape(B, rows, nblk, NA_QBLK_COLS, H, d).transpose(1, 0, 2, 3, 4, 5)
    rpb32 = rpb.astype(jnp.float32)
    scale = d ** -0.5

    def row_block(args):
        r, q_r = args
        start = jnp.clip(r - wr // 2, 0, rows - wr)
        k_r = lax.dynamic_slice_in_dim(kg, start, wr, axis=1)
        v_r = lax.dynamic_slice_in_dim(vg, start, wr, axis=1)
        s = jnp.einsum('bjqhd,bijkhd->bhjqik', q_r, k_r).astype(jnp.float32) * scale
        dr_idx = start + jnp.arange(wr) - r + NA_WIN_ROWS - 1
        bias = rpb32[:, dr_idx][:, :, dc_idx]
        s = s + bias.transpose(0, 2, 3, 1, 4)[None]
        s = jnp.where(col_mask[:, :, None, :], s, NEG_INF)
        p = jax.nn.softmax(s.reshape(B, H, nblk, NA_QBLK_COLS, wr * NA_KBLK_COLS), axis=-1)
        p = p.reshape(s.shape).astype(v.dtype)
        return jnp.einsum('bhjqik,bijkhd->bjqhd', p, v_r)

    out = lax.map(row_block, (jnp.arange(rows), qg))
    return out.transpose(1, 0, 2, 3, 4, 5).reshape(B, T, H * d)


def t5_bucket(rel):
    half = T5_BUCKETS // 2
    exact = half // 2
    n = np.abs(rel)
    large = exact + (np.log(np.maximum(n, 1) / exact) / math.log(T5_MAX_DIST / exact) * (half - exact)).astype(np.int64)
    large = np.minimum(large, half - 1)
    return (rel > 0).astype(np.int64) * half + np.where(n < exact, n, large)


def sliding_window_attention(q, k, v, t5_bias, sink):
    B, T, Hq, d = q.shape
    Hkv = k.shape[2]
    G = Hq // Hkv
    nb = T // SWA_BLOCK
    KB = 3 * SWA_BLOCK
    qb = q.reshape(B, nb, SWA_BLOCK, Hkv, G, d)

    def band(t):
        tb = t.reshape(B, nb, SWA_BLOCK, Hkv, d)
        tp = jnp.pad(tb, ((0, 0), (1, 1), (0, 0), (0, 0), (0, 0)))
        return jnp.concatenate([tp[:, :-2], tp[:, 1:-1], tp[:, 2:]], axis=2)

    kb, vb = band(k), band(v)
    kpos = np.arange(KB) - SWA_BLOCK
    rel = kpos[None, :] - np.arange(SWA_BLOCK)[:, None]
    abs_kpos = np.arange(nb)[:, None] * SWA_BLOCK + kpos[None, :]
    valid = (np.abs(rel) <= SWA_WINDOW)[None] & ((abs_kpos >= 0) & (abs_kpos < T))[:, None, :]
    bias = t5_bias.astype(jnp.float32)[t5_bucket(rel)]
    bias = bias.transpose(2, 0, 1).reshape(Hkv, G, 1, SWA_BLOCK, KB)
    s = jnp.einsum('bnqhgd,bnshd->bhgnqs', qb, kb).astype(jnp.float32) * (d ** -0.5) + bias
    s = jnp.where(valid, s, NEG_INF)
    sk = sink.astype(jnp.float32).reshape(Hkv, G, 1, 1)
    m = jnp.maximum(jnp.max(s, axis=-1), sk)
    p = jnp.exp(s - m[..., None])
    denom = jnp.sum(p, axis=-1) + jnp.exp(sk - m)
    p = (p / denom[..., None]).astype(v.dtype)
    out = jnp.einsum('bhgnqs,bnshd->bnqhgd', p, vb)
    return out.reshape(B, T, Hq * d)


def mixer(h, w_in, ret_decay_logit, ret_gn_gain, conv_dw_kernel, conv_dw_bias, conv_ln_gain, conv_ln_bias,
          na_rpb, swa_sink, t5_bias, w_branch, w_merge, w_out):
    B, T, _ = h.shape
    split_points = [int(s) for s in np.cumsum(IN_SIZES)[:-1]]
    rq, rk, rv, rg, ca, cb, nq, nk, nv, sq, sk, sv = jnp.split(h @ w_in, split_points, axis=-1)
    ret = retention_branch(rq, rk, rv, rg, ret_decay_logit, ret_gn_gain)
    conv = conv_branch(ca, cb, conv_dw_kernel, conv_dw_bias, conv_ln_gain, conv_ln_bias)
    na = neighborhood_attention(nq.reshape(B, T, NA_HEADS, HEAD_DIM), nk.reshape(B, T, NA_HEADS, HEAD_DIM),
                                nv.reshape(B, T, NA_HEADS, HEAD_DIM), na_rpb)
    swa = sliding_window_attention(sq.reshape(B, T, SWA_Q_HEADS, HEAD_DIM), sk.reshape(B, T, SWA_KV_HEADS, HEAD_DIM),
                                   sv.reshape(B, T, SWA_KV_HEADS, HEAD_DIM), t5_bias, swa_sink)
    branches = (ret, conv, na, swa)
    merged = jax.nn.sigmoid(h @ w_merge[0]) * (branches[0] @ w_branch[0])
    for i in range(1, N_BRANCHES):
        merged = merged + jax.nn.sigmoid(h @ w_merge[i]) * (branches[i] @ w_branch[i])
    return merged @ w_out


def trunk(x, c, w_ada, b_ada, norm_gain, w_in, ret_decay_logit, ret_gn_gain, conv_dw_kernel, conv_dw_bias,
          conv_ln_gain, conv_ln_bias, na_rpb, swa_sink, t5_bias, w_branch, w_merge, w_out, w_ff1, w_ff2, final_gain):
    for l in range(DEPTH):
        mod = jax.nn.silu(c) @ w_ada[l] + b_ada[l]
        sh1, sc1, g1, sh2, sc2, g2 = [m[:, None, :] for m in jnp.split(mod, 6, axis=-1)]
        h = rmsnorm(x, norm_gain[l, 0]) * (1 + sc1) + sh1
        x = x + g1 * mixer(h, w_in[l], ret_decay_logit[l], ret_gn_gain[l], conv_dw_kernel[l], conv_dw_bias[l],
                           conv_ln_gain[l], conv_ln_bias[l], na_rpb[l], swa_sink[l], t5_bias,
                           w_branch[l], w_merge[l], w_out[l])
        h = rmsnorm(x, norm_gain[l, 1]) * (1 + sc2) + sh2
        x = x + g2 * (jnp.square(jax.nn.relu(h @ w_ff1[l])) @ w_ff2[l])
    return rmsnorm(x, final_gain)


def setup_inputs(seed: int = 0) -> dict:
    key = jax.random.key(seed)
    ks = jax.random.split(key, 24)

    def nrm(k, shape, scale):
        return jax.random.normal(k, shape, jnp.float32) * scale

    gamma = 1.0 - 2.0 ** (-5.0 - np.arange(RET_HEADS))
    base_logit = np.log(gamma / (1.0 - gamma)).astype(np.float32)
    return {
        "x_prompt": nrm(ks[0], (BATCH, SEQ, D_MODEL), 1.0),
        "x_sample": nrm(ks[1], (DEC_BATCH, DEC_SEQ, D_MODEL), 1.0),
        "c_prompt": nrm(ks[2], (BATCH, D_MODEL), 1.0),
        "c_sample": nrm(ks[3], (DEC_BATCH, D_MODEL), 1.0),
        "w_ada": nrm(ks[4], (DEPTH, D_MODEL, 6 * D_MODEL), 0.5 * D_MODEL ** -0.5),
        "b_ada": nrm(ks[5], (DEPTH, 6 * D_MODEL), 0.02),
        "norm_gain": 1.0 + nrm(ks[6], (DEPTH, 2, D_MODEL), 0.02),
        "w_in": nrm(ks[7], (DEPTH, D_MODEL, IN_COLS), D_MODEL ** -0.5),
        "ret_decay_logit": jnp.asarray(base_logit) + nrm(ks[8], (DEPTH, 2, RET_HEADS), 0.1),
        "ret_gn_gain": 1.0 + nrm(ks[9], (DEPTH, BRANCH_WIDTH), 0.02),
        "conv_dw_kernel": nrm(ks[10], (DEPTH, CONV_WIDTH, CONV_CH), CONV_WIDTH ** -0.5),
        "conv_dw_bias": nrm(ks[11], (DEPTH, CONV_CH), 0.02),
        "conv_ln_gain": 1.0 + nrm(ks[12], (DEPTH, CONV_CH), 0.02),
        "conv_ln_bias": nrm(ks[13], (DEPTH, CONV_CH), 0.02),
        "na_rpb": nrm(ks[14], (DEPTH, NA_HEADS, 2 * NA_WIN_ROWS - 1, 2 * NA_WIN_COLS - 1), 0.1),
        "swa_sink": nrm(ks[15], (DEPTH, SWA_Q_HEADS), 0.5),
        "t5_bias": nrm(ks[16], (T5_BUCKETS, SWA_Q_HEADS), 0.1),
        "w_branch": nrm(ks[17], (DEPTH, N_BRANCHES, BRANCH_WIDTH, D_MODEL), BRANCH_WIDTH ** -0.5),
        "w_merge": nrm(ks[18], (DEPTH, N_BRANCHES, D_MODEL, D_MODEL), D_MODEL ** -0.5),
        "w_out": nrm(ks[19], (DEPTH, D_MODEL, D_MODEL), D_MODEL ** -0.5),
        "w_ff1": nrm(ks[20], (DEPTH, D_MODEL, D_FF), D_MODEL ** -0.5),
        "w_ff2": nrm(ks[21], (DEPTH, D_FF, D_MODEL), D_FF ** -0.5),
        "final_gain": 1.0 + nrm(ks[22], (D_MODEL,), 0.02),
    }


def reference(x_prompt, x_sample, c_prompt, c_sample, w_ada, b_ada, norm_gain, w_in, ret_decay_logit, ret_gn_gain,
              conv_dw_kernel, conv_dw_bias, conv_ln_gain, conv_ln_bias, na_rpb, swa_sink, t5_bias,
              w_branch, w_merge, w_out, w_ff1, w_ff2, final_gain):
    y_prompt = trunk(x_prompt, c_prompt, w_ada, b_ada, norm_gain, w_in, ret_decay_logit, ret_gn_gain,
                     conv_dw_kernel, conv_dw_bias, conv_ln_gain, conv_ln_bias, na_rpb, swa_sink, t5_bias,
                     w_branch, w_merge, w_out, w_ff1, w_ff2, final_gain)
    y_sample = trunk(x_sample, c_sample, w_ada, b_ada, norm_gain, w_in, ret_decay_logit, ret_gn_gain,
                     conv_dw_kernel, conv_dw_bias, conv_ln_gain, conv_ln_bias, na_rpb, swa_sink, t5_bias,
                     w_branch, w_merge, w_out, w_ff1, w_ff2, final_gain)
    return (y_prompt, y_sample)
```

```python
import functools
import math

import jax
import jax.numpy as jnp
import numpy as np
from jax import lax
from jax.experimental import pallas as pl
from jax.experimental.pallas import tpu as pltpu

F32 = jnp.float32
BF16 = jnp.bfloat16

D_MODEL = 1024
DEPTH = 2
HEAD_DIM = 64
BRANCH_WIDTH = 256
N_BRANCHES = 4
RET_CHUNK = 128
ROPE_BASE = 10000.0
CONV_WIDTH = 31
CONV_HALO = 16
GRID_W = 64
NA_WIN_ROWS = 8
NA_WIN_COLS = 16
NA_QROWS = 2
SWA_WINDOW = 128
SWA_BLOCK = 128
T5_BUCKETS = 32
T5_MAX_DIST = 128
D_FF = 4 * D_MODEL
NORM_EPS = 1e-6
NEG_INF = -1e30
IN_COLS = 2816
LANES = 128

TM_PROJ = 512
TM_MERGE = 512
TM_FFN = 512
TT_CONV = 256
TQ_SWA = 512
VMEM_LIMIT = 56 * 1024 * 1024


def _params(sem):
    return pltpu.CompilerParams(dimension_semantics=sem, vmem_limit_bytes=VMEM_LIMIT)


def _resident(shape):
    nd = len(shape)
    return pl.BlockSpec(shape, lambda *_: (0,) * nd)


def _modulated_rmsnorm(x, gain, scale, shift):
    y = x * lax.rsqrt(jnp.mean(x * x, axis=-1, keepdims=True) + NORM_EPS)
    return (y * gain) * (1.0 + scale) + shift


def _dot(a, b):
    return jnp.dot(a, b, preferred_element_type=F32)


def _dot_nt(a, b):
    return lax.dot_general(a, b, (((1,), (1,)), ((), ())), preferred_element_type=F32)


def _dot_tn(a, b):
    return lax.dot_general(a, b, (((0,), (0,)), ((), ())), preferred_element_type=F32)


def _dot_split(x, w):
    hi = x.astype(BF16)
    lo = (x - hi.astype(F32)).astype(BF16)
    return _dot(hi, w) + _dot(lo, w)


def _ada_kernel(c_ref, w_ref, b_ref, o_ref):
    c = c_ref[...]
    a = (c * jax.nn.sigmoid(c)).astype(BF16)
    o_ref[0] = _dot(a, w_ref[0]) + b_ref[0]


def _ada_call(c_all, w_ada, b_ada):
    nb = c_all.shape[0]
    return pl.pallas_call(
        _ada_kernel,
        grid=(DEPTH, 6),
        in_specs=[
            pl.BlockSpec((nb, D_MODEL), lambda l, j: (0, 0)),
            pl.BlockSpec((1, D_MODEL, D_MODEL), lambda l, j: (l, 0, j)),
            pl.BlockSpec((1, 1, D_MODEL), lambda l, j: (l, 0, j)),
        ],
        out_specs=pl.BlockSpec((1, nb, D_MODEL), lambda l, j: (l, 0, j)),
        out_shape=jax.ShapeDtypeStruct((DEPTH, nb, 6 * D_MODEL), F32),
        compiler_params=_params(("arbitrary", "arbitrary")),
        name="ada",
    )(c_all, w_ada, b_ada.reshape(DEPTH, 1, 6 * D_MODEL))


def _proj_kernel(x_ref, mod_ref, gain_ref, w_ref, cos_ref, sin_ref,
                 ret_ref, u_ref, naq_ref, nak_ref, nav_ref, sq_ref, sk_ref, sv_ref):
    tm = x_ref.shape[1]
    h = _modulated_rmsnorm(x_ref[0], gain_ref[...], mod_ref[0, 1:2, :], mod_ref[0, 0:1, :]).astype(BF16)
    cos = cos_ref[...]
    sin = sin_ref[...]
    lane = lax.broadcasted_iota(jnp.int32, (tm, LANES), 1)
    first_half = (lane & (HEAD_DIM // 2)) == 0
    qscale = HEAD_DIM ** -0.5

    def rot(y):
        sw = jnp.where(first_half, pltpu.roll(y, LANES - HEAD_DIM // 2, 1), pltpu.roll(y, HEAD_DIM // 2, 1))
        return y * cos + sw * sin

    y = _dot(h, w_ref[:, 0:1024])
    for t in range(2):
        ret_ref[0, :, t * LANES:(t + 1) * LANES] = (rot(y[:, t * LANES:(t + 1) * LANES]) * qscale).astype(BF16)
    for t in range(2, 4):
        ret_ref[0, :, t * LANES:(t + 1) * LANES] = rot(y[:, t * LANES:(t + 1) * LANES]).astype(BF16)
    ret_ref[0, :, 512:1024] = y[:, 512:1024].astype(BF16)
    y = _dot(h, w_ref[:, 1024:1536])
    u_ref[0] = (y[:, 0:256] * jax.nn.sigmoid(y[:, 256:512])).astype(BF16)
    y = _dot(h, w_ref[:, 1536:2304])
    naq_ref[0] = (y[:, 0:256] * qscale).astype(BF16)
    nak_ref[0] = y[:, 256:512].astype(BF16)
    nav_ref[0] = y[:, 512:768].astype(BF16)
    y = _dot(h, w_ref[:, 2304:2816])
    sq_ref[0] = (y[:, 0:256] * qscale).astype(BF16)
    sk_ref[0] = y[:, 256:384].astype(BF16)
    sv_ref[0] = y[:, 384:512].astype(BF16)


def _proj_call(x, mod, gain, w_in, cos_t, sin_t):
    B, T, _ = x.shape
    tm = TM_PROJ
    tok = lambda w: pl.BlockSpec((1, tm, w), lambda b, i: (b, i, 0))
    widths = (1024, 256, 256, 256, 256, 256, 128, 128)
    return pl.pallas_call(
        _proj_kernel,
        grid=(B, T // tm),
        in_specs=[
            tok(D_MODEL),
            pl.BlockSpec((1, 6, D_MODEL), lambda b, i: (b, 0, 0)),
            _resident((1, D_MODEL)),
            _resident((D_MODEL, IN_COLS)),
            pl.BlockSpec((tm, LANES), lambda b, i: (i, 0)),
            pl.BlockSpec((tm, LANES), lambda b, i: (i, 0)),
        ],
        out_specs=[tok(w) for w in widths],
        out_shape=[jax.ShapeDtypeStruct((B, T, w), BF16) for w in widths],
        compiler_params=_params(("arbitrary", "arbitrary")),
        name="proj",
    )(x, mod, gain, w_in, cos_t, sin_t)


def _ret_kernel(x_ref, dec_ref, xif_ref, xib_ref, zf_ref, zb_ref, gf_ref, gb_ref, bd_ref, avg_ref, gn_ref,
                o_ref, rf_ref, rb_ref, stash_ref):
    p = pl.program_id(1)
    n = pl.program_id(2)
    nchunks = pl.num_programs(2)
    k = x_ref[0, :, 256:512]
    v = x_ref[0, :, 512:768]

    @pl.when(p == 0)
    def _forward_states():
        @pl.when(n == 0)
        def _():
            rf_ref[...] = jnp.zeros_like(rf_ref)

        rf = rf_ref[...]
        stash_ref[n] = rf.astype(BF16)
        kz = (k.astype(F32) * zf_ref[...]).astype(BF16)
        rf_ref[...] = gf_ref[...] * rf + _dot_tn(kz, v) * bd_ref[...]

    @pl.when(p == 1)
    def _outputs():
        @pl.when(n == 0)
        def _():
            rb_ref[...] = jnp.zeros_like(rb_ref)

        c = nchunks - 1 - n
        q = x_ref[0, :, 0:256]
        g = x_ref[0, :, 768:1024].astype(F32)
        lane = lax.broadcasted_iota(jnp.int32, k.shape, 1) // HEAD_DIM
        heads = BRANCH_WIDTH // HEAD_DIM
        k4 = jnp.concatenate([jnp.where(lane == hh, k, jnp.zeros_like(k)) for hh in range(heads)], axis=0)
        v4 = jnp.concatenate([jnp.where(lane == hh, v, jnp.zeros_like(v)) for hh in range(heads)], axis=0)
        s = _dot_nt(q, k4)
        pm = (s * dec_ref[...]).astype(BF16)
        qf = q.astype(F32)
        rb = rb_ref[...]
        o = _dot(pm, v4)
        o = o + _dot((qf * xif_ref[...]).astype(BF16), stash_ref[c])
        o = o + _dot((qf * xib_ref[...]).astype(BF16), rb.astype(BF16))
        kz = (k.astype(F32) * zb_ref[...]).astype(BF16)
        rb_ref[...] = gb_ref[...] * rb + _dot_tn(kz, v) * bd_ref[...]
        avg = avg_ref[...]
        mu = _dot_split(o, avg)
        d = o - mu
        var = _dot_split(d * d, avg)
        on = d * lax.rsqrt(var + NORM_EPS) * gn_ref[...]
        o_ref[0] = (on.astype(F32) * (g * jax.nn.sigmoid(g))).astype(BF16)


def _ret_tables(decay_logit):
    C = RET_CHUNK
    heads = BRANCH_WIDTH // HEAD_DIM
    lg = jax.nn.log_sigmoid(decay_logit.astype(F32))
    pos = np.arange(C, dtype=np.float32)
    diff = pos[:, None] - pos[None, :]
    fwd = jnp.exp(jnp.where(diff >= 0, diff[None] * lg[0][:, None, None], -jnp.inf))
    bwd = jnp.exp(jnp.where(diff < 0, -diff[None] * lg[1][:, None, None], -jnp.inf))
    dec = (fwd + bwd).transpose(1, 0, 2).reshape(C, heads * C)
    lane_lg = jnp.repeat(lg, HEAD_DIM, axis=1)
    xif = jnp.exp((pos + 1.0)[:, None] * lane_lg[0][None, :])
    xib = jnp.exp((C - pos)[:, None] * lane_lg[1][None, :])
    zf = jnp.exp((C - 1.0 - pos)[:, None] * lane_lg[0][None, :])
    zb = jnp.exp(pos[:, None] * lane_lg[1][None, :])
    gf = jnp.exp(C * lane_lg[0])[None, :]
    gb = jnp.exp(C * lane_lg[1])[None, :]
    return dec, xif, xib, zf, zb, gf, gb


def _head_block_constants():
    hid = np.arange(BRANCH_WIDTH) // HEAD_DIM
    same = (hid[:, None] == hid[None, :])
    return jnp.asarray(same.astype(np.float32)), jnp.asarray(same.astype(np.float32) / HEAD_DIM, dtype=BF16)


def _ret_call(ret_in, tables, gn_gain):
    B, T, _ = ret_in.shape
    C = RET_CHUNK
    N = T // C
    bd, avg = _head_block_constants()
    chunk = lambda b, p, n: (b, jnp.where(p == 0, n, N - 1 - n), 0)
    out_chunk = lambda b, p, n: (b, jnp.where(p == 0, N - 1, N - 1 - n), 0)
    consts = list(tables) + [bd, avg, gn_gain.reshape(1, BRANCH_WIDTH).astype(F32)]
    return pl.pallas_call(
        _ret_kernel,
        grid=(B, 2, N),
        in_specs=[pl.BlockSpec((1, C, 4 * BRANCH_WIDTH), chunk)] + [_resident(t.shape) for t in consts],
        out_specs=pl.BlockSpec((1, C, BRANCH_WIDTH), out_chunk),
        out_shape=jax.ShapeDtypeStruct((B, T, BRANCH_WIDTH), BF16),
        scratch_shapes=[
            pltpu.VMEM((BRANCH_WIDTH, BRANCH_WIDTH), F32),
            pltpu.VMEM((BRANCH_WIDTH, BRANCH_WIDTH), F32),
            pltpu.VMEM((N, BRANCH_WIDTH, BRANCH_WIDTH), BF16),
        ],
        compiler_params=_params(("arbitrary", "arbitrary", "arbitrary")),
        name="ret",
    )(ret_in, *consts)


def _conv_kernel(u_ref, w_ref, b_ref, lg_ref, lb_ref, o_ref):
    i = pl.program_id(1)
    nt = pl.num_programs(1)
    tt = o_ref.shape[1]
    T = u_ref.shape[1]
    t0 = pl.multiple_of(i * tt, tt)
    left_start = pl.multiple_of(jnp.maximum(t0 - CONV_HALO, 0), CONV_HALO)
    right_start = pl.multiple_of(jnp.minimum(t0 + tt, T - CONV_HALO), CONV_HALO)
    left = u_ref[0, pl.ds(left_start, CONV_HALO), :].astype(F32) * (i > 0).astype(F32)
    right = u_ref[0, pl.ds(right_start, CONV_HALO), :].astype(F32) * (i < nt - 1).astype(F32)
    main = u_ref[0, pl.ds(t0, tt), :].astype(F32)
    win = jnp.concatenate([left, main, right], axis=0)
    w = w_ref[...]
    acc = jnp.zeros((tt, BRANCH_WIDTH), F32) + b_ref[...]
    off = CONV_HALO - CONV_WIDTH // 2
    for tap in range(CONV_WIDTH):
        acc = acc + win[tap + off:tap + off + tt, :] * w[tap:tap + 1, :]
    mu = jnp.mean(acc, axis=-1, keepdims=True)
    d = acc - mu
    var = jnp.mean(d * d, axis=-1, keepdims=True)
    y = d * lax.rsqrt(var + NORM_EPS) * lg_ref[...] + lb_ref[...]
    o_ref[0] = (y * jax.nn.sigmoid(y)).astype(BF16)


def _conv_call(u, dw_kernel, dw_bias, ln_gain, ln_bias):
    B, T, _ = u.shape
    tt = TT_CONV
    row = lambda a: a.reshape(1, BRANCH_WIDTH).astype(F32)
    return pl.pallas_call(
        _conv_kernel,
        grid=(B, T // tt),
        in_specs=[
            pl.BlockSpec((1, T, BRANCH_WIDTH), lambda b, i: (b, 0, 0)),
            _resident((CONV_WIDTH, BRANCH_WIDTH)),
            _resident((1, BRANCH_WIDTH)),
            _resident((1, BRANCH_WIDTH)),
            _resident((1, BRANCH_WIDTH)),
        ],
        out_specs=pl.BlockSpec((1, tt, BRANCH_WIDTH), lambda b, i: (b, i, 0)),
        out_shape=jax.ShapeDtypeStruct((B, T, BRANCH_WIDTH), BF16),
        compiler_params=_params(("arbitrary", "arbitrary")),
        name="conv",
    )(u, dw_kernel.astype(F32), row(dw_bias), row(ln_gain), row(ln_bias))


def _na_layout(rows):
    R = NA_QROWS
    KR = R + NA_WIN_ROWS
    nblk = rows // R
    qc = np.arange(GRID_W)
    ws = np.clip(qc - NA_WIN_COLS // 2, 0, GRID_W - NA_WIN_COLS)
    col_ok = (qc[None, :] >= ws[:, None]) & (qc[None, :] < ws[:, None] + NA_WIN_COLS)
    dc = qc[None, :] - qc[:, None] + NA_WIN_COLS - 1
    variants, var_of_block, kr0s = [], [], []
    for j in range(nblk):
        r0 = j * R
        kr0 = int(np.clip(r0 - NA_WIN_ROWS // 2, 0, rows - KR))
        r = r0 + np.arange(R)
        start = np.clip(r - NA_WIN_ROWS // 2, 0, rows - NA_WIN_ROWS)
        kr = kr0 + np.arange(KR)
        row_ok = (kr[None, :] >= start[:, None]) & (kr[None, :] < start[:, None] + NA_WIN_ROWS)
        dr = kr[None, :] - r[:, None] + NA_WIN_ROWS - 1
        key = (row_ok.tobytes(), dr.tobytes())
        if key not in [v[0] for v in variants]:
            variants.append((key, row_ok, dr))
        var_of_block.append([v[0] for v in variants].index(key))
        kr0s.append(kr0)
    valid = np.stack([(v[1][:, None, :, None] & col_ok[None, :, None, :]) for v in variants])
    dr_idx = np.stack([np.broadcast_to(v[2][:, None, :, None], valid.shape[1:]) for v in variants])
    dc_idx = np.broadcast_to(dc[None, None, :, None, :], valid.shape)
    nv = len(variants)
    shape2 = (nv, R * GRID_W, KR * GRID_W)
    valid = valid.reshape(shape2)
    dr_idx = np.clip(dr_idx, 0, 2 * NA_WIN_ROWS - 2).reshape(shape2)
    dc_idx = np.clip(dc_idx, 0, 2 * NA_WIN_COLS - 2).reshape(shape2)
    return valid, dr_idx, dc_idx, np.asarray(var_of_block, np.int32), np.asarray(kr0s, np.int32)


def _na_bias_table(rpb, rows):
    valid, dr_idx, dc_idx, _, _ = _na_layout(rows)
    tab = rpb.astype(F32)[:, dr_idx, dc_idx]
    tab = jnp.where(valid[None], tab, NEG_INF)
    return tab.transpose(1, 0, 2, 3)


def _na_kernel(var_ref, kr0_ref, q_ref, k_ref, v_ref, tab_ref, o_ref):
    j = pl.program_id(1)
    nk = tab_ref.shape[3]
    kstart = pl.multiple_of(kr0_ref[j] * GRID_W, GRID_W)
    lane = lax.broadcasted_iota(jnp.int32, (nk, LANES), 1)
    low = lane < HEAD_DIM
    for pair in range(BRANCH_WIDTH // LANES):
        cols = slice(pair * LANES, (pair + 1) * LANES)
        q2 = q_ref[0, :, cols]
        kb = k_ref[0, pl.ds(kstart, nk), cols]
        vb = v_ref[0, pl.ds(kstart, nk), cols]
        zero = jnp.zeros_like(kb)
        k2 = jnp.concatenate([jnp.where(low, kb, zero), jnp.where(low, zero, kb)], axis=0)
        v2 = jnp.concatenate([jnp.where(low, vb, zero), jnp.where(low, zero, vb)], axis=0)
        s = _dot_nt(q2, k2)
        probs = []
        for hh in range(2):
            sh = s[:, hh * nk:(hh + 1) * nk] + tab_ref[0, 2 * pair + hh]
            m = jnp.max(sh, axis=-1, keepdims=True)
            e = jnp.exp(sh - m)
            probs.append((e * (1.0 / jnp.sum(e, axis=-1, keepdims=True))).astype(BF16))
        o_ref[0, :, cols] = _dot(jnp.concatenate(probs, axis=1), v2).astype(BF16)


def _na_call(q, k, v, tab):
    B, T, _ = q.shape
    rows = T // GRID_W
    _, _, _, var_of_block, kr0s = _na_layout(rows)
    nq = NA_QROWS * GRID_W
    nblk = rows // NA_QROWS
    heads = BRANCH_WIDTH // HEAD_DIM
    grid_spec = pltpu.PrefetchScalarGridSpec(
        num_scalar_prefetch=2,
        grid=(B, nblk),
        in_specs=[
            pl.BlockSpec((1, nq, BRANCH_WIDTH), lambda b, j, var, kr0: (b, j, 0)),
            pl.BlockSpec((1, T, BRANCH_WIDTH), lambda b, j, var, kr0: (b, 0, 0)),
            pl.BlockSpec((1, T, BRANCH_WIDTH), lambda b, j, var, kr0: (b, 0, 0)),
            pl.BlockSpec((1, heads) + tab.shape[2:], lambda b, j, var, kr0: (var[j], 0, 0, 0)),
        ],
        out_specs=pl.BlockSpec((1, nq, BRANCH_WIDTH), lambda b, j, var, kr0: (b, j, 0)),
    )
    return pl.pallas_call(
        _na_kernel,
        grid_spec=grid_spec,
        out_shape=jax.ShapeDtypeStruct((B, T, BRANCH_WIDTH), BF16),
        compiler_params=_params(("arbitrary", "arbitrary")),
        name="na",
    )(jnp.asarray(var_of_block), jnp.asarray(kr0s), q, k, v, tab)


def _t5_bucket(rel):
    half = T5_BUCKETS // 2
    exact = half // 2
    n = np.abs(rel)
    large = exact + (np.log(np.maximum(n, 1) / exact) / math.log(T5_MAX_DIST / exact) * (half - exact)).astype(np.int64)
    large = np.minimum(large, half - 1)
    return (rel > 0).astype(np.int64) * half + np.where(n < exact, n, large)


def _swa_bias_table(t5_bias):
    kpos = np.arange(3 * SWA_BLOCK) - SWA_BLOCK
    rel = kpos[None, :] - np.arange(SWA_BLOCK)[:, None]
    tab = t5_bias.astype(F32)[_t5_bucket(rel)].transpose(2, 0, 1)
    return jnp.where((np.abs(rel) <= SWA_WINDOW)[None], tab, NEG_INF)


def _swa_kernel(sink_ref, q_ref, k_ref, v_ref, tab_ref, o_ref):
    n = pl.program_id(1)
    tq = q_ref.shape[1]
    T = k_ref.shape[1]
    nb = T // SWA_BLOCK
    per_step = tq // SWA_BLOCK
    lane = lax.broadcasted_iota(jnp.int32, (3 * SWA_BLOCK, LANES), 1)
    low = lane < HEAD_DIM
    for i in range(per_step):
        blk = n * per_step + i
        prev_start = pl.multiple_of(jnp.maximum(blk - 1, 0) * SWA_BLOCK, SWA_BLOCK)
        cur_start = pl.multiple_of(blk * SWA_BLOCK, SWA_BLOCK)
        next_start = pl.multiple_of(jnp.minimum(blk + 1, nb - 1) * SWA_BLOCK, SWA_BLOCK)
        band = lambda ref: jnp.concatenate([ref[0, pl.ds(prev_start, SWA_BLOCK), :],
                                            ref[0, pl.ds(cur_start, SWA_BLOCK), :],
                                            ref[0, pl.ds(next_start, SWA_BLOCK), :]], axis=0)
        k3 = band(k_ref)
        v3 = band(v_ref)
        k3r = pltpu.roll(k3, HEAD_DIM, 1)
        v3r = pltpu.roll(v3, HEAD_DIM, 1)
        zero = jnp.zeros_like(k3)
        kcol = lax.broadcasted_iota(jnp.int32, (SWA_BLOCK, 3 * SWA_BLOCK), 1)
        first_key = jnp.where(blk == 0, SWA_BLOCK, 0)
        end_key = jnp.where(blk == nb - 1, 2 * SWA_BLOCK, 3 * SWA_BLOCK)
        outside = (kcol < first_key) | (kcol >= end_key)
        for hk in range(2):
            a, b = (k3, k3r) if hk == 0 else (k3r, k3)
            k2 = jnp.concatenate([jnp.where(low, a, zero), jnp.where(low, zero, b)], axis=0)
            a, b = (v3, v3r) if hk == 0 else (v3r, v3)
            v2 = jnp.concatenate([jnp.where(low, a, zero), jnp.where(low, zero, b)], axis=0)
            q2 = q_ref[0, i * SWA_BLOCK:(i + 1) * SWA_BLOCK, hk * LANES:(hk + 1) * LANES]
            s = _dot_nt(q2, k2)
            probs = []
            for g in range(2):
                head = 2 * hk + g
                sh = s[:, g * 3 * SWA_BLOCK:(g + 1) * 3 * SWA_BLOCK] + tab_ref[head]
                sh = jnp.where(outside, NEG_INF, sh)
                sink = sink_ref[head]
                m = jnp.maximum(jnp.max(sh, axis=-1, keepdims=True), sink)
                e = jnp.exp(sh - m)
                denom = jnp.sum(e, axis=-1, keepdims=True) + jnp.exp(sink - m)
                probs.append((e * (1.0 / denom)).astype(BF16))
            o_ref[0, i * SWA_BLOCK:(i + 1) * SWA_BLOCK, hk * LANES:(hk + 1) * LANES] = (
                _dot(jnp.concatenate(probs, axis=1), v2).astype(BF16))


def _swa_call(q, k, v, tab, sink):
    B, T, _ = q.shape
    tq = TQ_SWA
    return pl.pallas_call(
        _swa_kernel,
        grid=(B, T // tq),
        in_specs=[
            pl.BlockSpec(memory_space=pltpu.SMEM),
            pl.BlockSpec((1, tq, BRANCH_WIDTH), lambda b, i: (b, i, 0)),
            pl.BlockSpec((1, T, LANES), lambda b, i: (b, 0, 0)),
            pl.BlockSpec((1, T, LANES), lambda b, i: (b, 0, 0)),
            _resident(tab.shape),
        ],
        out_specs=pl.BlockSpec((1, tq, BRANCH_WIDTH), lambda b, i: (b, i, 0)),
        out_shape=jax.ShapeDtypeStruct((B, T, BRANCH_WIDTH), BF16),
        compiler_params=_params(("arbitrary", "arbitrary")),
        name="swa",
    )(sink.astype(F32), q, k, v, tab)


def _merge_kernel(x_ref, mod_ref, gain_ref, b0_ref, b1_ref, b2_ref, b3_ref, wm_ref, wb_ref, wo_ref, o_ref):
    x = x_ref[0]
    h = _modulated_rmsnorm(x, gain_ref[...], mod_ref[0, 1:2, :], mod_ref[0, 0:1, :]).astype(BF16)
    merged = None
    for i, br in enumerate((b0_ref, b1_ref, b2_ref, b3_ref)):
        term = jax.nn.sigmoid(_dot(h, wm_ref[i])) * _dot(br[0], wb_ref[i])
        merged = term if merged is None else merged + term
    o_ref[0] = x + mod_ref[0, 2:3, :] * _dot(merged.astype(BF16), wo_ref[...])


def _merge_call(x, mod, gain, branches, w_merge, w_branch, w_out):
    B, T, _ = x.shape
    tm = TM_MERGE
    tok = lambda w: pl.BlockSpec((1, tm, w), lambda b, i: (b, i, 0))
    whole = pl.BlockSpec(memory_space=pltpu.VMEM)
    return pl.pallas_call(
        _merge_kernel,
        grid=(B, T // tm),
        in_specs=[tok(D_MODEL), pl.BlockSpec((1, 6, D_MODEL), lambda b, i: (b, 0, 0)), _resident((1, D_MODEL))]
                 + [tok(BRANCH_WIDTH)] * N_BRANCHES + [whole, whole, whole],
        out_specs=tok(D_MODEL),
        out_shape=jax.ShapeDtypeStruct((B, T, D_MODEL), F32),
        compiler_params=_params(("arbitrary", "arbitrary")),
        name="merge",
    )(x, mod, gain, *branches, w_merge, w_branch, w_out)


def _ffn_kernel(x_ref, mod_ref, gain_ref, fin_ref, w1_ref, w2_ref, o_ref, *, final_norm):
    x = x_ref[0]
    h = _modulated_rmsnorm(x, gain_ref[...], mod_ref[0, 4:5, :], mod_ref[0, 3:4, :]).astype(BF16)
    a = jnp.maximum(_dot(h, w1_ref[...]), 0.0)
    y = x + mod_ref[0, 5:6, :] * _dot((a * a).astype(BF16), w2_ref[...])
    if final_norm:
        y = y * lax.rsqrt(jnp.mean(y * y, axis=-1, keepdims=True) + NORM_EPS) * fin_ref[...]
    o_ref[0] = y


def _ffn_call(x, mod, gain, final_gain, w1, w2, final_norm):
    B, T, _ = x.shape
    tm = TM_FFN
    tok = pl.BlockSpec((1, tm, D_MODEL), lambda b, i: (b, i, 0))
    whole = pl.BlockSpec(memory_space=pltpu.VMEM)
    return pl.pallas_call(
        functools.partial(_ffn_kernel, final_norm=final_norm),
        grid=(B, T // tm),
        in_specs=[tok, pl.BlockSpec((1, 6, D_MODEL), lambda b, i: (b, 0, 0)), _resident((1, D_MODEL)),
                  _resident((1, D_MODEL)), whole, whole],
        out_specs=tok,
        out_shape=jax.ShapeDtypeStruct((B, T, D_MODEL), F32),
        compiler_params=_params(("arbitrary", "arbitrary")),
        name="ffn",
    )(x, mod, gain, final_gain, w1, w2)


def _rotary_tables(T):
    half = HEAD_DIM // 2
    inv = (ROPE_BASE ** (-np.arange(half, dtype=np.float32) / half)).astype(np.float32)
    ang = (np.arange(T, dtype=np.float32)[:, None] * inv[None, :]).astype(np.float32).astype(np.float64)
    lane = np.arange(LANES) % HEAD_DIM
    cos = np.cos(ang)[:, lane % half]
    sin = np.sin(ang)[:, lane % half] * np.where(lane < half, -1.0, 1.0)[None, :]
    return jnp.asarray(cos, F32), jnp.asarray(sin, F32)


def _trunk(x, mods, layer_params, shared):
    B, T, _ = x.shape
    cos_t, sin_t = _rotary_tables(T)
    rows = T // GRID_W
    for l, lp in enumerate(layer_params):
        mod = mods[l]
        ret_in, u, naq, nak, nav, sq, sk, sv = _proj_call(x, mod, lp["gain1"], lp["w_in"], cos_t, sin_t)
        ret = _ret_call(ret_in, lp["ret_tables"], lp["ret_gn_gain"])
        conv = _conv_call(u, lp["conv_dw_kernel"], lp["conv_dw_bias"], lp["conv_ln_gain"], lp["conv_ln_bias"])
        na = _na_call(naq, nak, nav, lp["na_tables"][rows])
        swa = _swa_call(sq, sk, sv, shared["swa_table"], lp["swa_sink"])
        x = _merge_call(x, mod, lp["gain1"], (ret, conv, na, swa), lp["w_merge"], lp["w_branch"], lp["w_out"])
        x = _ffn_call(x, mod, lp["gain2"], shared["final_gain"], lp["w_ff1"], lp["w_ff2"], l == len(layer_params) - 1)
    return x


def kernel(x_prompt, x_sample, c_prompt, c_sample, w_ada, b_ada, norm_gain, w_in, ret_decay_logit, ret_gn_gain,
           conv_dw_kernel, conv_dw_bias, conv_ln_gain, conv_ln_bias, na_rpb, swa_sink, t5_bias,
           w_branch, w_merge, w_out, w_ff1, w_ff2, final_gain):
    nbp = c_prompt.shape[0]
    c_all = jnp.concatenate([c_prompt, c_sample], axis=0)
    mods = _ada_call(c_all, w_ada.astype(BF16), b_ada.astype(F32))
    mods = mods.reshape(DEPTH, c_all.shape[0], 6, D_MODEL)
    row_counts = sorted({x_prompt.shape[1] // GRID_W, x_sample.shape[1] // GRID_W})
    layer_params = []
    for l in range(DEPTH):
        layer_params.append(dict(
            gain1=norm_gain[l, 0].reshape(1, D_MODEL).astype(F32),
            gain2=norm_gain[l, 1].reshape(1, D_MODEL).astype(F32),
            w_in=w_in[l].astype(BF16),
            ret_tables=_ret_tables(ret_decay_logit[l]),
            ret_gn_gain=ret_gn_gain[l],
            conv_dw_kernel=conv_dw_kernel[l], conv_dw_bias=conv_dw_bias[l],
            conv_ln_gain=conv_ln_gain[l], conv_ln_bias=conv_ln_bias[l],
            na_tables={r: _na_bias_table(na_rpb[l], r) for r in row_counts},
            swa_sink=swa_sink[l],
            w_merge=w_merge[l].astype(BF16), w_branch=w_branch[l].astype(BF16), w_out=w_out[l].astype(BF16),
            w_ff1=w_ff1[l].astype(BF16), w_ff2=w_ff2[l].astype(BF16),
        ))
    shared = dict(swa_table=_swa_bias_table(t5_bias), final_gain=final_gain.reshape(1, D_MODEL).astype(F32))
    y_prompt = _trunk(x_prompt, [m[:nbp] for m in mods], layer_params, shared)
    y_sample = _trunk(x_sample, [m[nbp:] for m in mods], layer_params, shared)
    return (y_prompt, y_sample)
```

```python
import functools
import math

import jax
import jax.numpy as jnp
import numpy as np
from jax import lax
from jax.experimental import pallas as pl
from jax.experimental.pallas import tpu as pltpu

F32 = jnp.float32
BF16 = jnp.bfloat16

D_MODEL = 1024
DEPTH = 2
HEAD_DIM = 64
BRANCH_WIDTH = 256
N_BRANCHES = 4
RET_CHUNK = 128
ROPE_BASE = 10000.0
CONV_WIDTH = 31
CONV_HALO = 16
GRID_W = 64
NA_WIN_ROWS = 8
NA_WIN_COLS = 16
NA_QROWS = 2
SWA_WINDOW = 128
SWA_BLOCK = 128
T5_BUCKETS = 32
T5_MAX_DIST = 128
D_FF = 4 * D_MODEL
NORM_EPS = 1e-6
NEG_INF = -1e30
IN_COLS = 2816
LANES = 128

TM_PROJ = 512
TM_MERGE = 512
TM_FFN = 512
TT_CONV = 256
TQ_SWA = 512
VMEM_LIMIT = 56 * 1024 * 1024


def _params(sem):
    return pltpu.CompilerParams(dimension_semantics=sem, vmem_limit_bytes=VMEM_LIMIT)


def _resident(shape):
    nd = len(shape)
    return pl.BlockSpec(shape, lambda *_: (0,) * nd)


def _modulated_rmsnorm(x, gain, scale, shift):
    y = x * lax.rsqrt(jnp.mean(x * x, axis=-1, keepdims=True) + NORM_EPS)
    return (y * gain) * (1.0 + scale) + shift


def _dot(a, b):
    return jnp.dot(a, b, preferred_element_type=F32)


def _dot_nt(a, b):
    return lax.dot_general(a, b, (((1,), (1,)), ((), ())), preferred_element_type=F32)


def _dot_tn(a, b):
    return lax.dot_general(a, b, (((0,), (0,)), ((), ())), preferred_element_type=F32)


def _dot_split(x, w):
    hi = x.astype(BF16)
    lo = (x - hi.astype(F32)).astype(BF16)
    return _dot(hi, w) + _dot(lo, w)


def _ada_kernel(c_ref, w_ref, b_ref, o_ref):
    c = c_ref[...]
    a = (c * jax.nn.sigmoid(c)).astype(BF16)
    o_ref[0] = _dot(a, w_ref[0]) + b_ref[0]


def _ada_call(c_all, w_ada, b_ada):
    nb = c_all.shape[0]
    return pl.pallas_call(
        _ada_kernel,
        grid=(DEPTH, 6),
        in_specs=[
            pl.BlockSpec((nb, D_MODEL), lambda l, j: (0, 0)),
            pl.BlockSpec((1, D_MODEL, D_MODEL), lambda l, j: (l, 0, j)),
            pl.BlockSpec((1, 1, D_MODEL), lambda l, j: (l, 0, j)),
        ],
        out_specs=pl.BlockSpec((1, nb, D_MODEL), lambda l, j: (l, 0, j)),
        out_shape=jax.ShapeDtypeStruct((DEPTH, nb, 6 * D_MODEL), F32),
        compiler_params=_params(("arbitrary", "arbitrary")),
        name="ada",
    )(c_all, w_ada, b_ada.reshape(DEPTH, 1, 6 * D_MODEL))


def _proj_kernel(x_ref, mod_ref, gain_ref, w_ref, cos_ref, sin_ref,
                 ret_ref, u_ref, naq_ref, nak_ref, nav_ref, sq_ref, sk_ref, sv_ref):
    tm = x_ref.shape[1]
    h = _modulated_rmsnorm(x_ref[0], gain_ref[...], mod_ref[0, 1:2, :], mod_ref[0, 0:1, :]).astype(BF16)
    cos = cos_ref[...]
    sin = sin_ref[...]
    lane = lax.broadcasted_iota(jnp.int32, (tm, LANES), 1)
    first_half = (lane & (HEAD_DIM // 2)) == 0
    qscale = HEAD_DIM ** -0.5

    def rot(y):
        sw = jnp.where(first_half, pltpu.roll(y, LANES - HEAD_DIM // 2, 1), pltpu.roll(y, HEAD_DIM // 2, 1))
        return y * cos + sw * sin

    y = _dot(h, w_ref[:, 0:1024])
    for t in range(2):
        ret_ref[0, :, t * LANES:(t + 1) * LANES] = (rot(y[:, t * LANES:(t + 1) * LANES]) * qscale).astype(BF16)
    for t in range(2, 4):
        ret_ref[0, :, t * LANES:(t + 1) * LANES] = rot(y[:, t * LANES:(t + 1) * LANES]).astype(BF16)
    ret_ref[0, :, 512:1024] = y[:, 512:1024].astype(BF16)
    y = _dot(h, w_ref[:, 1024:1536])
    u_ref[0] = (y[:, 0:256] * jax.nn.sigmoid(y[:, 256:512])).astype(BF16)
    y = _dot(h, w_ref[:, 1536:2304])
    naq_ref[0] = (y[:, 0:256] * qscale).astype(BF16)
    nak_ref[0] = y[:, 256:512].astype(BF16)
    nav_ref[0] = y[:, 512:768].astype(BF16)
    y = _dot(h, w_ref[:, 2304:2816])
    sq_ref[0] = (y[:, 0:256] * qscale).astype(BF16)
    sk_ref[0] = y[:, 256:384].astype(BF16)
    sv_ref[0] = y[:, 384:512].astype(BF16)


def _proj_call(x, mod, gain, w_in, cos_t, sin_t):
    B, T, _ = x.shape
    tm = TM_PROJ
    tok = lambda w: pl.BlockSpec((1, tm, w), lambda b, i: (b, i, 0))
    widths = (1024, 256, 256, 256, 256, 256, 128, 128)
    return pl.pallas_call(
        _proj_kernel,
        grid=(B, T // tm),
        in_specs=[
            tok(D_MODEL),
            pl.BlockSpec((1, 6, D_MODEL), lambda b, i: (b, 0, 0)),
            _resident((1, D_MODEL)),
            _resident((D_MODEL, IN_COLS)),
            pl.BlockSpec((tm, LANES), lambda b, i: (i, 0)),
            pl.BlockSpec((tm, LANES), lambda b, i: (i, 0)),
        ],
        out_specs=[tok(w) for w in widths],
        out_shape=[jax.ShapeDtypeStruct((B, T, w), BF16) for w in widths],
        compiler_params=_params(("arbitrary", "arbitrary")),
        name="proj",
    )(x, mod, gain, w_in, cos_t, sin_t)


def _ret_kernel(x_ref, dec_ref, xif_ref, xib_ref, zf_ref, zb_ref, gf_ref, gb_ref, bd_ref, avg_ref, gn_ref,
                o_ref, rf_ref, rb_ref, stash_ref):
    p = pl.program_id(1)
    n = pl.program_id(2)
    nchunks = pl.num_programs(2)
    k = x_ref[0, :, 256:512]
    v = x_ref[0, :, 512:768]

    @pl.when(p == 0)
    def _forward_states():
        @pl.when(n == 0)
        def _():
            rf_ref[...] = jnp.zeros_like(rf_ref)

        rf = rf_ref[...]
        stash_ref[n] = rf.astype(BF16)
        kz = (k.astype(F32) * zf_ref[...]).astype(BF16)
        rf_ref[...] = gf_ref[...] * rf + _dot_tn(kz, v) * bd_ref[...]

    @pl.when(p == 1)
    def _outputs():
        @pl.when(n == 0)
        def _():
            rb_ref[...] = jnp.zeros_like(rb_ref)

        c = nchunks - 1 - n
        q = x_ref[0, :, 0:256]
        g = x_ref[0, :, 768:1024].astype(F32)
        lane = lax.broadcasted_iota(jnp.int32, k.shape, 1) // HEAD_DIM
        heads = BRANCH_WIDTH // HEAD_DIM
        k4 = jnp.concatenate([jnp.where(lane == hh, k, jnp.zeros_like(k)) for hh in range(heads)], axis=0)
        v4 = jnp.concatenate([jnp.where(lane == hh, v, jnp.zeros_like(v)) for hh in range(heads)], axis=0)
        s = _dot_nt(q, k4)
        pm = (s * dec_ref[...]).astype(BF16)
        qf = q.astype(F32)
        rb = rb_ref[...]
        o = _dot(pm, v4)
        o = o + _dot((qf * xif_ref[...]).astype(BF16), stash_ref[c])
        o = o + _dot((qf * xib_ref[...]).astype(BF16), rb.astype(BF16))
        kz = (k.astype(F32) * zb_ref[...]).astype(BF16)
        rb_ref[...] = gb_ref[...] * rb + _dot_tn(kz, v) * bd_ref[...]
        avg = avg_ref[...]
        mu = _dot_split(o, avg)
        d = o - mu
        var = _dot_split(d * d, avg)
        on = d * lax.rsqrt(var + NORM_EPS) * gn_ref[...]
        o_ref[0] = (on * (g * jax.nn.sigmoid(g))).astype(BF16)


def _ret_tables(decay_logit):
    C = RET_CHUNK
    heads = BRANCH_WIDTH // HEAD_DIM
    lg = jax.nn.log_sigmoid(decay_logit.astype(F32))
    pos = np.arange(C, dtype=np.float32)
    diff = pos[:, None] - pos[None, :]
    fwd = jnp.exp(jnp.where(diff >= 0, diff[None] * lg[0][:, None, None], -jnp.inf))
    bwd = jnp.exp(jnp.where(diff < 0, -diff[None] * lg[1][:, None, None], -jnp.inf))
    dec = (fwd + bwd).transpose(1, 0, 2).reshape(C, heads * C)
    lane_lg = jnp.repeat(lg, HEAD_DIM, axis=1)
    xif = jnp.exp((pos + 1.0)[:, None] * lane_lg[0][None, :])
    xib = jnp.exp((C - pos)[:, None] * lane_lg[1][None, :])
    zf = jnp.exp((C - 1.0 - pos)[:, None] * lane_lg[0][None, :])
    zb = jnp.exp(pos[:, None] * lane_lg[1][None, :])
    gf = jnp.exp(C * lane_lg[0])[None, :]
    gb = jnp.exp(C * lane_lg[1])[None, :]
    return dec, xif, xib, zf, zb, gf, gb


def _head_block_constants():
    hid = np.arange(BRANCH_WIDTH) // HEAD_DIM
    same = (hid[:, None] == hid[None, :])
    return jnp.asarray(same.astype(np.float32)), jnp.asarray(same.astype(np.float32) / HEAD_DIM, dtype=BF16)


def _ret_call(ret_in, tables, gn_gain):
    B, T, _ = ret_in.shape
    C = RET_CHUNK
    N = T // C
    bd, avg = _head_block_constants()
    chunk = lambda b, p, n: (b, jnp.where(p == 0, n, N - 1 - n), 0)
    out_chunk = lambda b, p, n: (b, jnp.where(p == 0, N - 1, N - 1 - n), 0)
    consts = list(tables) + [bd, avg, gn_gain.reshape(1, BRANCH_WIDTH).astype(F32)]
    return pl.pallas_call(
        _ret_kernel,
        grid=(B, 2, N),
        in_specs=[pl.BlockSpec((1, C, 4 * BRANCH_WIDTH), chunk)] + [_resident(t.shape) for t in consts],
        out_specs=pl.BlockSpec((1, C, BRANCH_WIDTH), out_chunk),
        out_shape=jax.ShapeDtypeStruct((B, T, BRANCH_WIDTH), BF16),
        scratch_shapes=[
            pltpu.VMEM((BRANCH_WIDTH, BRANCH_WIDTH), F32),
            pltpu.VMEM((BRANCH_WIDTH, BRANCH_WIDTH), F32),
            pltpu.VMEM((N, BRANCH_WIDTH, BRANCH_WIDTH), BF16),
        ],
        compiler_params=_params(("arbitrary", "arbitrary", "arbitrary")),
        name="ret",
    )(ret_in, *consts)


def _conv_kernel(u_ref, w_ref, b_ref, lg_ref, lb_ref, o_ref):
    i = pl.program_id(1)
    nt = pl.num_programs(1)
    tt = o_ref.shape[1]
    T = u_ref.shape[1]
    t0 = pl.multiple_of(i * tt, tt)
    left_start = pl.multiple_of(jnp.maximum(t0 - CONV_HALO, 0), CONV_HALO)
    right_start = pl.multiple_of(jnp.minimum(t0 + tt, T - CONV_HALO), CONV_HALO)
    left = u_ref[0, pl.ds(left_start, CONV_HALO), :].astype(F32) * (i > 0).astype(F32)
    right = u_ref[0, pl.ds(right_start, CONV_HALO), :].astype(F32) * (i < nt - 1).astype(F32)
    main = u_ref[0, pl.ds(t0, tt), :].astype(F32)
    win = jnp.concatenate([left, main, right], axis=0)
    w = w_ref[...]
    acc = jnp.zeros((tt, BRANCH_WIDTH), F32) + b_ref[...]
    off = CONV_HALO - CONV_WIDTH // 2
    for tap in range(CONV_WIDTH):
        acc = acc + win[tap + off:tap + off + tt, :] * w[tap:tap + 1, :]
    mu = jnp.mean(acc, axis=-1, keepdims=True)
    d = acc - mu
    var = jnp.mean(d * d, axis=-1, keepdims=True)
    y = d * lax.rsqrt(var + NORM_EPS) * lg_ref[...] + lb_ref[...]
    o_ref[0] = (y * jax.nn.sigmoid(y)).astype(BF16)


def _conv_call(u, dw_kernel, dw_bias, ln_gain, ln_bias):
    B, T, _ = u.shape
    tt = TT_CONV
    row = lambda a: a.reshape(1, BRANCH_WIDTH).astype(F32)
    return pl.pallas_call(
        _conv_kernel,
        grid=(B, T // tt),
        in_specs=[
            pl.BlockSpec((1, T, BRANCH_WIDTH), lambda b, i: (b, 0, 0)),
            _resident((CONV_WIDTH, BRANCH_WIDTH)),
            _resident((1, BRANCH_WIDTH)),
            _resident((1, BRANCH_WIDTH)),
            _resident((1, BRANCH_WIDTH)),
        ],
        out_specs=pl.BlockSpec((1, tt, BRANCH_WIDTH), lambda b, i: (b, i, 0)),
        out_shape=jax.ShapeDtypeStruct((B, T, BRANCH_WIDTH), BF16),
        compiler_params=_params(("arbitrary", "arbitrary")),
        name="conv",
    )(u, dw_kernel.astype(F32), row(dw_bias), row(ln_gain), row(ln_bias))


def _na_layout(rows):
    R = NA_QROWS
    KR = R + NA_WIN_ROWS
    nblk = rows // R
    qc = np.arange(GRID_W)
    ws = np.clip(qc - NA_WIN_COLS // 2, 0, GRID_W - NA_WIN_COLS)
    col_ok = (qc[None, :] >= ws[:, None]) & (qc[None, :] < ws[:, None] + NA_WIN_COLS)
    dc = qc[None, :] - qc[:, None] + NA_WIN_COLS - 1
    n_dr = 2 * NA_WIN_ROWS - 1
    variants, var_of_block, kr0s = [], [], []
    for j in range(nblk):
        r0 = j * R
        kr0 = int(np.clip(r0 - NA_WIN_ROWS // 2, 0, rows - KR))
        r = r0 + np.arange(R)
        start = np.clip(r - NA_WIN_ROWS // 2, 0, rows - NA_WIN_ROWS)
        kr = kr0 + np.arange(KR)
        row_ok = (kr[None, :] >= start[:, None]) & (kr[None, :] < start[:, None] + NA_WIN_ROWS)
        tile = np.where(row_ok, kr[None, :] - r[:, None] + NA_WIN_ROWS - 1, n_dr)
        key = tile.tobytes()
        if key not in [v[0] for v in variants]:
            variants.append((key, tile))
        var_of_block.append([v[0] for v in variants].index(key))
        kr0s.append(kr0)
    tiles = np.stack([v[1] for v in variants])
    onehot = (dc[None] == np.arange(2 * NA_WIN_COLS - 1)[:, None, None]) & col_ok[None]
    return tiles, onehot, col_ok, np.asarray(var_of_block, np.int32), np.asarray(kr0s, np.int32)


def _na_bias_table(rpb, rows):
    tiles, onehot, col_ok, _, _ = _na_layout(rows)
    H, n_dr, n_dc = rpb.shape
    oh = jnp.asarray(onehot.reshape(n_dc, GRID_W * GRID_W), F32)
    tz = jnp.dot(rpb.astype(F32).reshape(H * n_dr, n_dc), oh, precision=lax.Precision.HIGHEST)
    tz = jnp.where(col_ok[None, None], tz.reshape(H, n_dr, GRID_W, GRID_W), NEG_INF)
    tz = jnp.concatenate([tz, jnp.full((H, 1, GRID_W, GRID_W), NEG_INF, F32)], axis=1)
    nv, R, KR = tiles.shape
    per_variant = []
    for vv in range(nv):
        slabs = [jnp.concatenate([tz[:, int(tiles[vv, rl, kl])] for kl in range(KR)], axis=-1) for rl in range(R)]
        per_variant.append(jnp.concatenate(slabs, axis=1))
    return jnp.stack(per_variant)


def _na_bias_tables(rpb, row_counts):
    by_layout, out = {}, {}
    for rows in row_counts:
        key = _na_layout(rows)[0].tobytes()
        if key not in by_layout:
            by_layout[key] = _na_bias_table(rpb, rows)
        out[rows] = by_layout[key]
    return out


def _na_kernel(var_ref, kr0_ref, q_ref, k_ref, v_ref, tab_ref, o_ref):
    j = pl.program_id(1)
    nk = tab_ref.shape[3]
    kstart = pl.multiple_of(kr0_ref[j] * GRID_W, GRID_W)
    lane = lax.broadcasted_iota(jnp.int32, (nk, LANES), 1)
    low = lane < HEAD_DIM
    for pair in range(BRANCH_WIDTH // LANES):
        cols = slice(pair * LANES, (pair + 1) * LANES)
        q2 = q_ref[0, :, cols]
        kb = k_ref[0, pl.ds(kstart, nk), cols]
        vb = v_ref[0, pl.ds(kstart, nk), cols]
        zero = jnp.zeros_like(kb)
        k2 = jnp.concatenate([jnp.where(low, kb, zero), jnp.where(low, zero, kb)], axis=0)
        v2 = jnp.concatenate([jnp.where(low, vb, zero), jnp.where(low, zero, vb)], axis=0)
        s = _dot_nt(q2, k2)
        probs = []
        for hh in range(2):
            sh = s[:, hh * nk:(hh + 1) * nk] + tab_ref[0, 2 * pair + hh]
            m = jnp.max(sh, axis=-1, keepdims=True)
            e = jnp.exp(sh - m)
            probs.append((e * (1.0 / jnp.sum(e, axis=-1, keepdims=True))).astype(BF16))
        o_ref[0, :, cols] = _dot(jnp.concatenate(probs, axis=1), v2).astype(BF16)


def _na_call(q, k, v, tab):
    B, T, _ = q.shape
    rows = T // GRID_W
    _, _, _, var_of_block, kr0s = _na_layout(rows)
    nq = NA_QROWS * GRID_W
    nblk = rows // NA_QROWS
    heads = BRANCH_WIDTH // HEAD_DIM
    grid_spec = pltpu.PrefetchScalarGridSpec(
        num_scalar_prefetch=2,
        grid=(B, nblk),
        in_specs=[
            pl.BlockSpec((1, nq, BRANCH_WIDTH), lambda b, j, var, kr0: (b, j, 0)),
            pl.BlockSpec((1, T, BRANCH_WIDTH), lambda b, j, var, kr0: (b, 0, 0)),
            pl.BlockSpec((1, T, BRANCH_WIDTH), lambda b, j, var, kr0: (b, 0, 0)),
            pl.BlockSpec((1, heads) + tab.shape[2:], lambda b, j, var, kr0: (var[j], 0, 0, 0)),
        ],
        out_specs=pl.BlockSpec((1, nq, BRANCH_WIDTH), lambda b, j, var, kr0: (b, j, 0)),
    )
    return pl.pallas_call(
        _na_kernel,
        grid_spec=grid_spec,
        out_shape=jax.ShapeDtypeStruct((B, T, BRANCH_WIDTH), BF16),
        compiler_params=_params(("arbitrary", "arbitrary")),
        name="na",
    )(jnp.asarray(var_of_block), jnp.asarray(kr0s), q, k, v, tab)


def _t5_bucket(rel):
    half = T5_BUCKETS // 2
    exact = half // 2
    n = np.abs(rel)
    large = exact + (np.log(np.maximum(n, 1) / exact) / math.log(T5_MAX_DIST / exact) * (half - exact)).astype(np.int64)
    large = np.minimum(large, half - 1)
    return (rel > 0).astype(np.int64) * half + np.where(n < exact, n, large)


def _swa_bias_table(t5_bias):
    kpos = np.arange(3 * SWA_BLOCK) - SWA_BLOCK
    rel = kpos[None, :] - np.arange(SWA_BLOCK)[:, None]
    onehot = (_t5_bucket(rel)[None] == np.arange(T5_BUCKETS)[:, None, None])
    oh = jnp.asarray(onehot.reshape(T5_BUCKETS, -1), F32)
    tab = jnp.dot(t5_bias.astype(F32).T, oh, precision=lax.Precision.HIGHEST).reshape((-1,) + rel.shape)
    return jnp.where((np.abs(rel) <= SWA_WINDOW)[None], tab, NEG_INF)


def _swa_kernel(sink_ref, q_ref, k_ref, v_ref, tab_ref, o_ref):
    n = pl.program_id(1)
    tq = q_ref.shape[1]
    T = k_ref.shape[1]
    nb = T // SWA_BLOCK
    per_step = tq // SWA_BLOCK
    lane = lax.broadcasted_iota(jnp.int32, (3 * SWA_BLOCK, LANES), 1)
    low = lane < HEAD_DIM
    for i in range(per_step):
        blk = n * per_step + i
        prev_start = pl.multiple_of(jnp.maximum(blk - 1, 0) * SWA_BLOCK, SWA_BLOCK)
        cur_start = pl.multiple_of(blk * SWA_BLOCK, SWA_BLOCK)
        next_start = pl.multiple_of(jnp.minimum(blk + 1, nb - 1) * SWA_BLOCK, SWA_BLOCK)
        band = lambda ref: jnp.concatenate([ref[0, pl.ds(prev_start, SWA_BLOCK), :],
                                            ref[0, pl.ds(cur_start, SWA_BLOCK), :],
                                            ref[0, pl.ds(next_start, SWA_BLOCK), :]], axis=0)
        k3 = band(k_ref)
        v3 = band(v_ref)
        k3r = pltpu.roll(k3, HEAD_DIM, 1)
        v3r = pltpu.roll(v3, HEAD_DIM, 1)
        zero = jnp.zeros_like(k3)
        kcol = lax.broadcasted_iota(jnp.int32, (SWA_BLOCK, 3 * SWA_BLOCK), 1)
        first_key = jnp.where(blk == 0, SWA_BLOCK, 0)
        end_key = jnp.where(blk == nb - 1, 2 * SWA_BLOCK, 3 * SWA_BLOCK)
        outside = (kcol < first_key) | (kcol >= end_key)
        for hk in range(2):
            a, b = (k3, k3r) if hk == 0 else (k3r, k3)
            k2 = jnp.concatenate([jnp.where(low, a, zero), jnp.where(low, zero, b)], axis=0)
            a, b = (v3, v3r) if hk == 0 else (v3r, v3)
            v2 = jnp.concatenate([jnp.where(low, a, zero), jnp.where(low, zero, b)], axis=0)
            q2 = q_ref[0, i * SWA_BLOCK:(i + 1) * SWA_BLOCK, hk * LANES:(hk + 1) * LANES]
            s = _dot_nt(q2, k2)
            probs = []
            for g in range(2):
                head = 2 * hk + g
                sh = s[:, g * 3 * SWA_BLOCK:(g + 1) * 3 * SWA_BLOCK] + tab_ref[head]
                sh = jnp.where(outside, NEG_INF, sh)
                sink = sink_ref[head]
                m = jnp.maximum(jnp.max(sh, axis=-1, keepdims=True), sink)
                e = jnp.exp(sh - m)
                denom = jnp.sum(e, axis=-1, keepdims=True) + jnp.exp(sink - m)
                probs.append((e * (1.0 / denom)).astype(BF16))
            o_ref[0, i * SWA_BLOCK:(i + 1) * SWA_BLOCK, hk * LANES:(hk + 1) * LANES] = (
                _dot(jnp.concatenate(probs, axis=1), v2).astype(BF16))


def _swa_call(q, k, v, tab, sink):
    B, T, _ = q.shape
    tq = TQ_SWA
    return pl.pallas_call(
        _swa_kernel,
        grid=(B, T // tq),
        in_specs=[
            pl.BlockSpec(memory_space=pltpu.SMEM),
            pl.BlockSpec((1, tq, BRANCH_WIDTH), lambda b, i: (b, i, 0)),
            pl.BlockSpec((1, T, LANES), lambda b, i: (b, 0, 0)),
            pl.BlockSpec((1, T, LANES), lambda b, i: (b, 0, 0)),
            _resident(tab.shape),
        ],
        out_specs=pl.BlockSpec((1, tq, BRANCH_WIDTH), lambda b, i: (b, i, 0)),
        out_shape=jax.ShapeDtypeStruct((B, T, BRANCH_WIDTH), BF16),
        compiler_params=_params(("arbitrary", "arbitrary")),
        name="swa",
    )(sink.astype(F32), q, k, v, tab)


def _merge_kernel(x_ref, mod_ref, gain_ref, b0_ref, b1_ref, b2_ref, b3_ref, wm_ref, wb_ref, wo_ref, o_ref):
    x = x_ref[0]
    h = _modulated_rmsnorm(x, gain_ref[...], mod_ref[0, 1:2, :], mod_ref[0, 0:1, :]).astype(BF16)
    merged = None
    for i, br in enumerate((b0_ref, b1_ref, b2_ref, b3_ref)):
        term = jax.nn.sigmoid(_dot(h, wm_ref[i])) * _dot(br[0], wb_ref[i])
        merged = term if merged is None else merged + term
    o_ref[0] = x + mod_ref[0, 2:3, :] * _dot(merged.astype(BF16), wo_ref[...])


def _merge_call(x, mod, gain, branches, w_merge, w_branch, w_out):
    B, T, _ = x.shape
    tm = TM_MERGE
    tok = lambda w: pl.BlockSpec((1, tm, w), lambda b, i: (b, i, 0))
    whole = pl.BlockSpec(memory_space=pltpu.VMEM)
    return pl.pallas_call(
        _merge_kernel,
        grid=(B, T // tm),
        in_specs=[tok(D_MODEL), pl.BlockSpec((1, 6, D_MODEL), lambda b, i: (b, 0, 0)), _resident((1, D_MODEL))]
                 + [tok(BRANCH_WIDTH)] * N_BRANCHES + [whole, whole, whole],
        out_specs=tok(D_MODEL),
        out_shape=jax.ShapeDtypeStruct((B, T, D_MODEL), F32),
        compiler_params=_params(("arbitrary", "arbitrary")),
        name="merge",
    )(x, mod, gain, *branches, w_merge, w_branch, w_out)


def _ffn_kernel(x_ref, mod_ref, gain_ref, fin_ref, w1_ref, w2_ref, o_ref, *, final_norm):
    x = x_ref[0]
    h = _modulated_rmsnorm(x, gain_ref[...], mod_ref[0, 4:5, :], mod_ref[0, 3:4, :]).astype(BF16)
    a = jnp.maximum(_dot(h, w1_ref[...]), 0.0)
    y = x + mod_ref[0, 5:6, :] * _dot((a * a).astype(BF16), w2_ref[...])
    if final_norm:
        y = y * lax.rsqrt(jnp.mean(y * y, axis=-1, keepdims=True) + NORM_EPS) * fin_ref[...]
    o_ref[0] = y


def _ffn_call(x, mod, gain, final_gain, w1, w2, final_norm):
    B, T, _ = x.shape
    tm = TM_FFN
    tok = pl.BlockSpec((1, tm, D_MODEL), lambda b, i: (b, i, 0))
    whole = pl.BlockSpec(memory_space=pltpu.VMEM)
    return pl.pallas_call(
        functools.partial(_ffn_kernel, final_norm=final_norm),
        grid=(B, T // tm),
        in_specs=[tok, pl.BlockSpec((1, 6, D_MODEL), lambda b, i: (b, 0, 0)), _resident((1, D_MODEL)),
                  _resident((1, D_MODEL)), whole, whole],
        out_specs=tok,
        out_shape=jax.ShapeDtypeStruct((B, T, D_MODEL), F32),
        compiler_params=_params(("arbitrary", "arbitrary")),
        name="ffn",
    )(x, mod, gain, final_gain, w1, w2)


def _rotary_tables(T):
    half = HEAD_DIM // 2
    inv = (ROPE_BASE ** (-np.arange(half, dtype=np.float32) / half)).astype(np.float32)
    ang = (np.arange(T, dtype=np.float32)[:, None] * inv[None, :]).astype(np.float32).astype(np.float64)
    lane = np.arange(LANES) % HEAD_DIM
    cos = np.cos(ang)[:, lane % half]
    sin = np.sin(ang)[:, lane % half] * np.where(lane < half, -1.0, 1.0)[None, :]
    return jnp.asarray(cos, F32), jnp.asarray(sin, F32)


def _trunk(x, mods, layer_params, shared):
    B, T, _ = x.shape
    cos_t, sin_t = _rotary_tables(T)
    rows = T // GRID_W
    for l, lp in enumerate(layer_params):
        mod = mods[l]
        ret_in, u, naq, nak, nav, sq, sk, sv = _proj_call(x, mod, lp["gain1"], lp["w_in"], cos_t, sin_t)
        ret = _ret_call(ret_in, lp["ret_tables"], lp["ret_gn_gain"])
        conv = _conv_call(u, lp["conv_dw_kernel"], lp["conv_dw_bias"], lp["conv_ln_gain"], lp["conv_ln_bias"])
        na = _na_call(naq, nak, nav, lp["na_tables"][rows])
        swa = _swa_call(sq, sk, sv, shared["swa_table"], lp["swa_sink"])
        x = _merge_call(x, mod, lp["gain1"], (ret, conv, na, swa), lp["w_merge"], lp["w_branch"], lp["w_out"])
        x = _ffn_call(x, mod, lp["gain2"], shared["final_gain"], lp["w_ff1"], lp["w_ff2"], l == len(layer_params) - 1)
    return x


def kernel(x_prompt, x_sample, c_prompt, c_sample, w_ada, b_ada, norm_gain, w_in, ret_decay_logit, ret_gn_gain,
           conv_dw_kernel, conv_dw_bias, conv_ln_gain, conv_ln_bias, na_rpb, swa_sink, t5_bias,
           w_branch, w_merge, w_out, w_ff1, w_ff2, final_gain):
    nbp = c_prompt.shape[0]
    c_all = jnp.concatenate([c_prompt, c_sample], axis=0)
    mods = _ada_call(c_all, w_ada.astype(BF16), b_ada.astype(F32))
    mods = mods.reshape(DEPTH, c_all.shape[0], 6, D_MODEL)
    row_counts = sorted({x_prompt.shape[1] // GRID_W, x_sample.shape[1] // GRID_W})
    layer_params = []
    for l in range(DEPTH):
        layer_params.append(dict(
            gain1=norm_gain[l, 0].reshape(1, D_MODEL).astype(F32),
            gain2=norm_gain[l, 1].reshape(1, D_MODEL).astype(F32),
            w_in=w_in[l].astype(BF16),
            ret_tables=_ret_tables(ret_decay_logit[l]),
            ret_gn_gain=ret_gn_gain[l],
            conv_dw_kernel=conv_dw_kernel[l], conv_dw_bias=conv_dw_bias[l],
            conv_ln_gain=conv_ln_gain[l], conv_ln_bias=conv_ln_bias[l],
            na_tables=_na_bias_tables(na_rpb[l], row_counts),
            swa_sink=swa_sink[l],
            w_merge=w_merge[l].astype(BF16), w_branch=w_branch[l].astype(BF16), w_out=w_out[l].astype(BF16),
            w_ff1=w_ff1[l].astype(BF16), w_ff2=w_ff2[l].astype(BF16),
        ))
    shared = dict(swa_table=_swa_bias_table(t5_bias), final_gain=final_gain.reshape(1, D_MODEL).astype(F32))
    y_prompt = _trunk(x_prompt, [m[:nbp] for m in mods], layer_params, shared)
    y_sample = _trunk(x_sample, [m[nbp:] for m in mods], layer_params, shared)
    return (y_prompt, y_sample)
```

```python
import functools
import math

import jax
import jax.numpy as jnp
import numpy as np
from jax import lax
from jax.experimental import pallas as pl
from jax.experimental.pallas import tpu as pltpu

F32 = jnp.float32
BF16 = jnp.bfloat16

D_MODEL = 1024
DEPTH = 2
HEAD_DIM = 64
BRANCH_WIDTH = 256
N_BRANCHES = 4
RET_CHUNK = 128
RET_CHUNKS_PER_STEP = 4
ROPE_BASE = 10000.0
CONV_WIDTH = 31
CONV_HALO = 16
GRID_W = 64
NA_WIN_ROWS = 8
NA_WIN_COLS = 16
NA_QROWS = 2
NA_BLOCKS_PER_STEP = 2
SWA_WINDOW = 128
SWA_BLOCK = 128
T5_BUCKETS = 32
T5_MAX_DIST = 128
D_FF = 4 * D_MODEL
NORM_EPS = 1e-6
NEG_INF = -1e30
IN_COLS = 2816
LANES = 128
SUBLANES = 8

TM_PROJ = 512
TM_MERGE = 512
TM_FFN = 512
TT_CONV = 512
CONV_ROWS = 32
TQ_SWA = 512
VMEM_LIMIT = 56 * 1024 * 1024


def _params(sem):
    return pltpu.CompilerParams(dimension_semantics=sem, vmem_limit_bytes=VMEM_LIMIT)


def _resident(shape):
    nd = len(shape)
    return pl.BlockSpec(shape, lambda *_: (0,) * nd)


def _modulated_rmsnorm(x, gain, scale, shift):
    y = x * lax.rsqrt(jnp.mean(x * x, axis=-1, keepdims=True) + NORM_EPS)
    return (y * gain) * (1.0 + scale) + shift


def _dot(a, b):
    return jnp.dot(a, b, preferred_element_type=F32)


def _dot_nt(a, b):
    return lax.dot_general(a, b, (((1,), (1,)), ((), ())), preferred_element_type=F32)


def _dot_tn(a, b):
    return lax.dot_general(a, b, (((0,), (0,)), ((), ())), preferred_element_type=F32)


def _dot_split(x, w):
    hi = x.astype(BF16)
    lo = (x - hi.astype(F32)).astype(BF16)
    return _dot(hi, w) + _dot(lo, w)


def _ada_kernel(c_ref, w_ref, b_ref, o_ref):
    c = c_ref[...]
    a = (c * jax.nn.sigmoid(c)).astype(BF16)
    o_ref[0] = _dot(a, w_ref[0]) + b_ref[0]


def _ada_call(c_all, w_ada, b_ada):
    nb = c_all.shape[0]
    return pl.pallas_call(
        _ada_kernel,
        grid=(DEPTH, 6),
        in_specs=[
            pl.BlockSpec((nb, D_MODEL), lambda l, j: (0, 0)),
            pl.BlockSpec((1, D_MODEL, D_MODEL), lambda l, j: (l, 0, j)),
            pl.BlockSpec((1, 1, D_MODEL), lambda l, j: (l, 0, j)),
        ],
        out_specs=pl.BlockSpec((1, nb, D_MODEL), lambda l, j: (l, 0, j)),
        out_shape=jax.ShapeDtypeStruct((DEPTH, nb, 6 * D_MODEL), F32),
        compiler_params=_params(("arbitrary", "arbitrary")),
        name="ada",
    )(c_all, w_ada, b_ada.reshape(DEPTH, 1, 6 * D_MODEL))


def _proj_kernel(x_ref, mod_ref, gain_ref, w_ref, cos_ref, sin_ref,
                 ret_ref, u_ref, naq_ref, nak_ref, nav_ref, sq_ref, sk_ref, sv_ref):
    tm = x_ref.shape[1]
    h = _modulated_rmsnorm(x_ref[0], gain_ref[...], mod_ref[0, 1:2, :], mod_ref[0, 0:1, :]).astype(BF16)
    cos = cos_ref[...]
    sin = sin_ref[...]
    lane = lax.broadcasted_iota(jnp.int32, (tm, LANES), 1)
    first_half = (lane & (HEAD_DIM // 2)) == 0
    qscale = HEAD_DIM ** -0.5

    def rot(y):
        sw = jnp.where(first_half, pltpu.roll(y, LANES - HEAD_DIM // 2, 1), pltpu.roll(y, HEAD_DIM // 2, 1))
        return y * cos + sw * sin

    y = _dot(h, w_ref[:, 0:1024])
    for t in range(2):
        ret_ref[0, :, t * LANES:(t + 1) * LANES] = (rot(y[:, t * LANES:(t + 1) * LANES]) * qscale).astype(BF16)
    for t in range(2, 4):
        ret_ref[0, :, t * LANES:(t + 1) * LANES] = rot(y[:, t * LANES:(t + 1) * LANES]).astype(BF16)
    ret_ref[0, :, 512:1024] = y[:, 512:1024].astype(BF16)
    y = _dot(h, w_ref[:, 1024:1536])
    u_ref[0] = (y[:, 0:256] * jax.nn.sigmoid(y[:, 256:512])).astype(BF16)
    y = _dot(h, w_ref[:, 1536:2304])
    naq_ref[0] = (y[:, 0:256] * qscale).astype(BF16)
    nak_ref[0] = y[:, 256:512].astype(BF16)
    nav_ref[0] = y[:, 512:768].astype(BF16)
    y = _dot(h, w_ref[:, 2304:2816])
    sq_ref[0] = (y[:, 0:256] * qscale).astype(BF16)
    sk_ref[0] = y[:, 256:384].astype(BF16)
    sv_ref[0] = y[:, 384:512].astype(BF16)


def _proj_call(x, mod, gain, w_in, cos_t, sin_t):
    B, T, _ = x.shape
    tm = TM_PROJ
    tok = lambda w: pl.BlockSpec((1, tm, w), lambda b, i: (b, i, 0))
    widths = (1024, 256, 256, 256, 256, 256, 128, 128)
    return pl.pallas_call(
        _proj_kernel,
        grid=(B, T // tm),
        in_specs=[
            tok(D_MODEL),
            pl.BlockSpec((1, 6, D_MODEL), lambda b, i: (b, 0, 0)),
            _resident((1, D_MODEL)),
            _resident((D_MODEL, IN_COLS)),
            pl.BlockSpec((tm, LANES), lambda b, i: (i, 0)),
            pl.BlockSpec((tm, LANES), lambda b, i: (i, 0)),
        ],
        out_specs=[tok(w) for w in widths],
        out_shape=[jax.ShapeDtypeStruct((B, T, w), BF16) for w in widths],
        compiler_params=_params(("arbitrary", "arbitrary")),
        name="proj",
    )(x, mod, gain, w_in, cos_t, sin_t)


def _ret_kernel(x_ref, dec_ref, xif_ref, xib_ref, zf_ref, zb_ref, gf_ref, gb_ref, bd_ref, avg_ref, gn_ref,
                o_ref, rf_ref, rb_ref, stash_ref):
    p = pl.program_id(1)
    n = pl.program_id(2)
    nsteps = pl.num_programs(2)
    C = RET_CHUNK
    W = BRANCH_WIDTH
    G = x_ref.shape[1] // C
    heads = W // HEAD_DIM

    def chunk_kv(j):
        rows = slice(j * C, (j + 1) * C)
        return x_ref[0, rows, W:2 * W], x_ref[0, rows, 2 * W:3 * W]

    def chunk_state(k, v, zeta):
        return _dot_tn((k.astype(F32) * zeta).astype(BF16), v) * bd_ref[...]

    @pl.when(p == 0)
    def _forward_states():
        @pl.when(n == 0)
        def _():
            rf_ref[...] = jnp.zeros_like(rf_ref)

        rf = rf_ref[...]
        for j in range(G):
            k, v = chunk_kv(j)
            stash_ref[n * G + j] = rf.astype(BF16)
            rf = gf_ref[...] * rf + chunk_state(k, v, zf_ref[...])
        rf_ref[...] = rf

    @pl.when(p == 1)
    def _outputs():
        @pl.when(n == 0)
        def _():
            rb_ref[...] = jnp.zeros_like(rb_ref)

        first_chunk = (nsteps - 1 - n) * G
        lane_head = lax.broadcasted_iota(jnp.int32, (C, W), 1) // HEAD_DIM
        rb = rb_ref[...]
        outs = [None] * G
        for j in reversed(range(G)):
            k, v = chunk_kv(j)
            q = x_ref[0, j * C:(j + 1) * C, 0:W]
            k4 = jnp.concatenate([jnp.where(lane_head == hh, k, jnp.zeros_like(k)) for hh in range(heads)], axis=0)
            v4 = jnp.concatenate([jnp.where(lane_head == hh, v, jnp.zeros_like(v)) for hh in range(heads)], axis=0)
            s = _dot_nt(q, k4)
            qf = q.astype(F32)
            o = _dot((s * dec_ref[...]).astype(BF16), v4)
            o = o + _dot((qf * xif_ref[...]).astype(BF16), stash_ref[first_chunk + j])
            outs[j] = o + _dot((qf * xib_ref[...]).astype(BF16), rb.astype(BF16))
            rb = gb_ref[...] * rb + chunk_state(k, v, zb_ref[...])
        rb_ref[...] = rb
        o = jnp.concatenate(outs, axis=0)
        avg = avg_ref[...]
        d = o - _dot_split(o, avg)
        var = _dot_split(d * d, avg)
        g = x_ref[0, :, 3 * W:4 * W].astype(F32)
        o_ref[0] = (d * lax.rsqrt(var + NORM_EPS) * gn_ref[...] * (g * jax.nn.sigmoid(g))).astype(BF16)


def _ret_tables(decay_logit):
    C = RET_CHUNK
    heads = BRANCH_WIDTH // HEAD_DIM
    lg = jax.nn.log_sigmoid(decay_logit.astype(F32))
    pos = np.arange(C, dtype=np.float32)
    diff = pos[:, None] - pos[None, :]
    fwd = jnp.exp(jnp.where(diff >= 0, diff[None] * lg[0][:, None, None], -jnp.inf))
    bwd = jnp.exp(jnp.where(diff < 0, -diff[None] * lg[1][:, None, None], -jnp.inf))
    dec = (fwd + bwd).transpose(1, 0, 2).reshape(C, heads * C)
    lane_lg = jnp.repeat(lg, HEAD_DIM, axis=1)
    xif = jnp.exp((pos + 1.0)[:, None] * lane_lg[0][None, :])
    xib = jnp.exp((C - pos)[:, None] * lane_lg[1][None, :])
    zf = jnp.exp((C - 1.0 - pos)[:, None] * lane_lg[0][None, :])
    zb = jnp.exp(pos[:, None] * lane_lg[1][None, :])
    gf = jnp.exp(C * lane_lg[0])[None, :]
    gb = jnp.exp(C * lane_lg[1])[None, :]
    return dec, xif, xib, zf, zb, gf, gb


def _head_block_constants():
    hid = np.arange(BRANCH_WIDTH) // HEAD_DIM
    same = (hid[:, None] == hid[None, :])
    return jnp.asarray(same.astype(np.float32)), jnp.asarray(same.astype(np.float32) / HEAD_DIM, dtype=BF16)


def _ret_call(ret_in, tables, gn_gain):
    B, T, _ = ret_in.shape
    rows = RET_CHUNK * RET_CHUNKS_PER_STEP
    N = T // RET_CHUNK
    S = T // rows
    bd, avg = _head_block_constants()
    step_rows = lambda b, p, n: (b, jnp.where(p == 0, n, S - 1 - n), 0)
    out_rows = lambda b, p, n: (b, jnp.where(p == 0, S - 1, S - 1 - n), 0)
    consts = list(tables) + [bd, avg, gn_gain.reshape(1, BRANCH_WIDTH).astype(F32)]
    return pl.pallas_call(
        _ret_kernel,
        grid=(B, 2, S),
        in_specs=[pl.BlockSpec((1, rows, 4 * BRANCH_WIDTH), step_rows)] + [_resident(t.shape) for t in consts],
        out_specs=pl.BlockSpec((1, rows, BRANCH_WIDTH), out_rows),
        out_shape=jax.ShapeDtypeStruct((B, T, BRANCH_WIDTH), BF16),
        scratch_shapes=[
            pltpu.VMEM((BRANCH_WIDTH, BRANCH_WIDTH), F32),
            pltpu.VMEM((BRANCH_WIDTH, BRANCH_WIDTH), F32),
            pltpu.VMEM((N, BRANCH_WIDTH, BRANCH_WIDTH), BF16),
        ],
        compiler_params=_params(("arbitrary", "arbitrary", "arbitrary")),
        name="ret",
    )(ret_in, *consts)


def _conv_kernel(u_ref, w_ref, b_ref, lg_ref, lb_ref, o_ref, win_ref):
    i = pl.program_id(1)
    nt = pl.num_programs(1)
    tt = o_ref.shape[1]
    T = u_ref.shape[1]
    t0 = pl.multiple_of(i * tt, tt)
    left_start = pl.multiple_of(jnp.maximum(t0 - CONV_HALO, 0), CONV_HALO)
    right_start = pl.multiple_of(jnp.minimum(t0 + tt, T - CONV_HALO), CONV_HALO)
    left = u_ref[0, pl.ds(left_start, CONV_HALO), :].astype(F32) * (i > 0).astype(F32)
    right = u_ref[0, pl.ds(right_start, CONV_HALO), :].astype(F32) * (i < nt - 1).astype(F32)
    win = jnp.concatenate([left, u_ref[0, pl.ds(t0, tt), :].astype(F32), right], axis=0)
    span = win_ref.shape[1]
    for s in range(SUBLANES):
        win_ref[s] = win[s:s + span]
    off = CONV_HALO - CONV_WIDTH // 2
    for r in range(0, tt, CONV_ROWS):
        acc = jnp.zeros((CONV_ROWS // SUBLANES, SUBLANES, BRANCH_WIDTH), F32)
        for tap in range(CONV_WIDTH):
            s, a = (tap + off) % SUBLANES, (tap + off) // SUBLANES * SUBLANES
            rows = win_ref[s, r + a:r + a + CONV_ROWS, :].reshape(acc.shape)
            acc = acc + rows * w_ref[tap][None]
        acc = acc.reshape(CONV_ROWS, BRANCH_WIDTH) + b_ref[...]
        mu = jnp.mean(acc, axis=-1, keepdims=True)
        d = acc - mu
        var = jnp.mean(d * d, axis=-1, keepdims=True)
        y = d * lax.rsqrt(var + NORM_EPS) * lg_ref[...] + lb_ref[...]
        o_ref[0, r:r + CONV_ROWS, :] = (y * jax.nn.sigmoid(y)).astype(BF16)


def _conv_call(u, dw_kernel, dw_bias, ln_gain, ln_bias):
    B, T, _ = u.shape
    tt = TT_CONV
    row = lambda a: a.reshape(1, BRANCH_WIDTH).astype(F32)
    return pl.pallas_call(
        _conv_kernel,
        grid=(B, T // tt),
        in_specs=[
            pl.BlockSpec((1, T, BRANCH_WIDTH), lambda b, i: (b, 0, 0)),
            _resident((CONV_WIDTH, SUBLANES, BRANCH_WIDTH)),
            _resident((1, BRANCH_WIDTH)),
            _resident((1, BRANCH_WIDTH)),
            _resident((1, BRANCH_WIDTH)),
        ],
        out_specs=pl.BlockSpec((1, tt, BRANCH_WIDTH), lambda b, i: (b, i, 0)),
        out_shape=jax.ShapeDtypeStruct((B, T, BRANCH_WIDTH), BF16),
        scratch_shapes=[pltpu.VMEM((SUBLANES, tt + 2 * CONV_HALO - SUBLANES, BRANCH_WIDTH), F32)],
        compiler_params=_params(("arbitrary", "arbitrary")),
        name="conv",
    )(u, jnp.broadcast_to(dw_kernel.astype(F32)[:, None, :], (CONV_WIDTH, SUBLANES, BRANCH_WIDTH)),
      row(dw_bias), row(ln_gain), row(ln_bias))


def _na_layout(rows):
    R = NA_QROWS
    KR = R + NA_WIN_ROWS
    nblk = rows // R
    qc = np.arange(GRID_W)
    ws = np.clip(qc - NA_WIN_COLS // 2, 0, GRID_W - NA_WIN_COLS)
    col_ok = (qc[None, :] >= ws[:, None]) & (qc[None, :] < ws[:, None] + NA_WIN_COLS)
    dc = qc[None, :] - qc[:, None] + NA_WIN_COLS - 1
    n_dr = 2 * NA_WIN_ROWS - 1
    variants, var_of_block, kr0s = [], [], []
    for j in range(nblk):
        r0 = j * R
        kr0 = int(np.clip(r0 - NA_WIN_ROWS // 2, 0, rows - KR))
        r = r0 + np.arange(R)
        start = np.clip(r - NA_WIN_ROWS // 2, 0, rows - NA_WIN_ROWS)
        kr = kr0 + np.arange(KR)
        row_ok = (kr[None, :] >= start[:, None]) & (kr[None, :] < start[:, None] + NA_WIN_ROWS)
        tile = np.where(row_ok, kr[None, :] - r[:, None] + NA_WIN_ROWS - 1, n_dr)
        key = tile.tobytes()
        if key not in [v[0] for v in variants]:
            variants.append((key, tile))
        var_of_block.append([v[0] for v in variants].index(key))
        kr0s.append(kr0)
    tiles = np.stack([v[1] for v in variants])
    onehot = (dc[None] == np.arange(2 * NA_WIN_COLS - 1)[:, None, None]) & col_ok[None]
    return tiles, onehot, col_ok, np.asarray(var_of_block, np.int32), np.asarray(kr0s, np.int32)


def _na_bias_table(rpb, rows):
    tiles, onehot, col_ok, _, _ = _na_layout(rows)
    H, n_dr, n_dc = rpb.shape
    oh = jnp.asarray(onehot.reshape(n_dc, GRID_W * GRID_W), F32)
    tz = jnp.dot(rpb.astype(F32).reshape(H * n_dr, n_dc), oh, precision=lax.Precision.HIGHEST)
    tz = jnp.where(col_ok[None, None], tz.reshape(H, n_dr, GRID_W, GRID_W), NEG_INF)
    tz = jnp.concatenate([tz, jnp.full((H, 1, GRID_W, GRID_W), NEG_INF, F32)], axis=1)
    nv, R, KR = tiles.shape
    per_variant = []
    for vv in range(nv):
        slabs = [jnp.concatenate([tz[:, int(tiles[vv, rl, kl])] for kl in range(KR)], axis=-1) for rl in range(R)]
        per_variant.append(jnp.concatenate(slabs, axis=1))
    return jnp.stack(per_variant)


def _na_bias_tables(rpb, row_counts):
    by_layout, out = {}, {}
    for rows in row_counts:
        key = _na_layout(rows)[0].tobytes()
        if key not in by_layout:
            by_layout[key] = _na_bias_table(rpb, rows)
        out[rows] = by_layout[key]
    return out


def _na_kernel(var_ref, kr0_ref, q_ref, k_ref, v_ref, *rest):
    tab_refs, o_ref = rest[:-1], rest[-1]
    step = pl.program_id(1)
    nq = q_ref.shape[1] // NA_BLOCKS_PER_STEP
    nk = tab_refs[0].shape[3]
    low = lax.broadcasted_iota(jnp.int32, (nq, LANES), 1) < HEAD_DIM
    for sb, tab_ref in enumerate(tab_refs):
        kstart = pl.multiple_of(kr0_ref[step * NA_BLOCKS_PER_STEP + sb] * GRID_W, GRID_W)
        rows = slice(sb * nq, (sb + 1) * nq)
        for pair in range(BRANCH_WIDTH // LANES):
            cols = slice(pair * LANES, (pair + 1) * LANES)
            q2 = q_ref[0, rows, cols]
            zero = jnp.zeros_like(q2)
            qs = jnp.concatenate([jnp.where(low, q2, zero), jnp.where(low, zero, q2)], axis=0)
            s = _dot_nt(qs, k_ref[0, pl.ds(kstart, nk), cols]) + tab_ref[0, pair]
            m = jnp.max(s, axis=-1, keepdims=True)
            e = jnp.exp(s - m)
            o2 = _dot(e.astype(BF16), v_ref[0, pl.ds(kstart, nk), cols]) * (1.0 / jnp.sum(e, axis=-1, keepdims=True))
            o_ref[0, rows, cols] = jnp.where(low, o2[:nq], o2[nq:]).astype(BF16)


def _na_call(q, k, v, tab):
    B, T, _ = q.shape
    rows = T // GRID_W
    _, _, _, var_of_block, kr0s = _na_layout(rows)
    nq = NA_QROWS * GRID_W
    per_step = NA_BLOCKS_PER_STEP
    nblk = rows // NA_QROWS
    assert nblk % per_step == 0
    nv, heads, _, nk = tab.shape
    tab = tab.reshape(nv, heads // 2, 2 * nq, nk)
    tab_spec = lambda sb: pl.BlockSpec((1, heads // 2, 2 * nq, nk),
                                       lambda b, j, var, kr0: (var[j * per_step + sb], 0, 0, 0))
    grid_spec = pltpu.PrefetchScalarGridSpec(
        num_scalar_prefetch=2,
        grid=(B, nblk // per_step),
        in_specs=[
            pl.BlockSpec((1, per_step * nq, BRANCH_WIDTH), lambda b, j, var, kr0: (b, j, 0)),
            pl.BlockSpec((1, T, BRANCH_WIDTH), lambda b, j, var, kr0: (b, 0, 0)),
            pl.BlockSpec((1, T, BRANCH_WIDTH), lambda b, j, var, kr0: (b, 0, 0)),
        ] + [tab_spec(sb) for sb in range(per_step)],
        out_specs=pl.BlockSpec((1, per_step * nq, BRANCH_WIDTH), lambda b, j, var, kr0: (b, j, 0)),
    )
    return pl.pallas_call(
        _na_kernel,
        grid_spec=grid_spec,
        out_shape=jax.ShapeDtypeStruct((B, T, BRANCH_WIDTH), BF16),
        compiler_params=_params(("arbitrary", "arbitrary")),
        name="na",
    )(jnp.asarray(var_of_block), jnp.asarray(kr0s), q, k, v, *([tab] * per_step))


def _t5_bucket(rel):
    half = T5_BUCKETS // 2
    exact = half // 2
    n = np.abs(rel)
    large = exact + (np.log(np.maximum(n, 1) / exact) / math.log(T5_MAX_DIST / exact) * (half - exact)).astype(np.int64)
    large = np.minimum(large, half - 1)
    return (rel > 0).astype(np.int64) * half + np.where(n < exact, n, large)


def _swa_bias_table(t5_bias):
    kpos = np.arange(3 * SWA_BLOCK) - SWA_BLOCK
    rel = kpos[None, :] - np.arange(SWA_BLOCK)[:, None]
    onehot = (_t5_bucket(rel)[None] == np.arange(T5_BUCKETS)[:, None, None])
    oh = jnp.asarray(onehot.reshape(T5_BUCKETS, -1), F32)
    tab = jnp.dot(t5_bias.astype(F32).T, oh, precision=lax.Precision.HIGHEST).reshape((-1,) + rel.shape)
    tab = jnp.where((np.abs(rel) <= SWA_WINDOW)[None], tab, NEG_INF)
    before = (kpos < 0)[None, None, :]
    after = (kpos >= SWA_BLOCK)[None, None, :]
    tabs = jnp.stack([jnp.where(before, NEG_INF, tab), tab, jnp.where(after, NEG_INF, tab)])
    hq = tab.shape[0]
    return tabs.reshape(3, hq // 2, 2 * SWA_BLOCK, 3 * SWA_BLOCK)


def _swa_kernel(sink_ref, q_ref, k_ref, v_ref, tab_ref, o_ref):
    n = pl.program_id(1)
    tq = q_ref.shape[1]
    T = k_ref.shape[1]
    nb = T // SWA_BLOCK
    per_step = tq // SWA_BLOCK
    low = lax.broadcasted_iota(jnp.int32, (SWA_BLOCK, LANES), 1) < HEAD_DIM
    first_rows = lax.broadcasted_iota(jnp.int32, (2 * SWA_BLOCK, 1), 0) < SWA_BLOCK
    for i in range(per_step):
        blk = n * per_step + i
        prev_start = pl.multiple_of(jnp.maximum(blk - 1, 0) * SWA_BLOCK, SWA_BLOCK)
        cur_start = pl.multiple_of(blk * SWA_BLOCK, SWA_BLOCK)
        next_start = pl.multiple_of(jnp.minimum(blk + 1, nb - 1) * SWA_BLOCK, SWA_BLOCK)
        band = lambda ref: jnp.concatenate([ref[0, pl.ds(prev_start, SWA_BLOCK), :],
                                            ref[0, pl.ds(cur_start, SWA_BLOCK), :],
                                            ref[0, pl.ds(next_start, SWA_BLOCK), :]], axis=0)
        k3 = band(k_ref)
        v3 = band(v_ref)
        variant = jnp.where(blk == 0, 0, jnp.where(blk == nb - 1, 2, 1))
        rows = slice(i * SWA_BLOCK, (i + 1) * SWA_BLOCK)
        for hk in range(2):
            cols = slice(hk * LANES, (hk + 1) * LANES)
            q2 = q_ref[0, rows, cols]
            q2r = pltpu.roll(q2, HEAD_DIM, 1)
            zero = jnp.zeros_like(q2)
            if hk == 0:
                qs = jnp.concatenate([jnp.where(low, q2, zero), jnp.where(low, q2r, zero)], axis=0)
            else:
                qs = jnp.concatenate([jnp.where(low, zero, q2r), jnp.where(low, zero, q2)], axis=0)
            s = _dot_nt(qs, k3) + tab_ref[variant, hk]
            sink = jnp.where(first_rows, sink_ref[2 * hk], sink_ref[2 * hk + 1])
            m = jnp.maximum(jnp.max(s, axis=-1, keepdims=True), sink)
            e = jnp.exp(s - m)
            denom = jnp.sum(e, axis=-1, keepdims=True) + jnp.exp(sink - m)
            o2 = _dot(e.astype(BF16), v3) * (1.0 / denom)
            top, bot = o2[:SWA_BLOCK], o2[SWA_BLOCK:]
            if hk == 0:
                out = jnp.where(low, top, pltpu.roll(bot, HEAD_DIM, 1))
            else:
                out = jnp.where(low, pltpu.roll(top, HEAD_DIM, 1), bot)
            o_ref[0, rows, cols] = out.astype(BF16)


def _swa_call(q, k, v, tab, sink):
    B, T, _ = q.shape
    tq = TQ_SWA
    assert T // SWA_BLOCK >= 2
    return pl.pallas_call(
        _swa_kernel,
        grid=(B, T // tq),
        in_specs=[
            pl.BlockSpec(memory_space=pltpu.SMEM),
            pl.BlockSpec((1, tq, BRANCH_WIDTH), lambda b, i: (b, i, 0)),
            pl.BlockSpec((1, T, LANES), lambda b, i: (b, 0, 0)),
            pl.BlockSpec((1, T, LANES), lambda b, i: (b, 0, 0)),
            _resident(tab.shape),
        ],
        out_specs=pl.BlockSpec((1, tq, BRANCH_WIDTH), lambda b, i: (b, i, 0)),
        out_shape=jax.ShapeDtypeStruct((B, T, BRANCH_WIDTH), BF16),
        compiler_params=_params(("arbitrary", "arbitrary")),
        name="swa",
    )(sink.astype(F32), q, k, v, tab)


def _merge_kernel(x_ref, mod_ref, gain_ref, b0_ref, b1_ref, b2_ref, b3_ref, wm_ref, wb_ref, wo_ref, o_ref):
    x = x_ref[0]
    h = _modulated_rmsnorm(x, gain_ref[...], mod_ref[0, 1:2, :], mod_ref[0, 0:1, :]).astype(BF16)
    merged = None
    for i, br in enumerate((b0_ref, b1_ref, b2_ref, b3_ref)):
        term = jax.nn.sigmoid(_dot(h, wm_ref[i])) * _dot(br[0], wb_ref[i])
        merged = term if merged is None else merged + term
    o_ref[0] = x + mod_ref[0, 2:3, :] * _dot(merged.astype(BF16), wo_ref[...])


def _merge_call(x, mod, gain, branches, w_merge, w_branch, w_out):
    B, T, _ = x.shape
    tm = TM_MERGE
    tok = lambda w: pl.BlockSpec((1, tm, w), lambda b, i: (b, i, 0))
    whole = pl.BlockSpec(memory_space=pltpu.VMEM)
    return pl.pallas_call(
        _merge_kernel,
        grid=(B, T // tm),
        in_specs=[tok(D_MODEL), pl.BlockSpec((1, 6, D_MODEL), lambda b, i: (b, 0, 0)), _resident((1, D_MODEL))]
                 + [tok(BRANCH_WIDTH)] * N_BRANCHES + [whole, whole, whole],
        out_specs=tok(D_MODEL),
        out_shape=jax.ShapeDtypeStruct((B, T, D_MODEL), F32),
        compiler_params=_params(("arbitrary", "arbitrary")),
        name="merge",
    )(x, mod, gain, *branches, w_merge, w_branch, w_out)


def _ffn_kernel(x_ref, mod_ref, gain_ref, fin_ref, w1_ref, w2_ref, o_ref, *, final_norm):
    x = x_ref[0]
    h = _modulated_rmsnorm(x, gain_ref[...], mod_ref[0, 4:5, :], mod_ref[0, 3:4, :]).astype(BF16)
    a = jnp.maximum(_dot(h, w1_ref[...]), 0.0)
    y = x + mod_ref[0, 5:6, :] * _dot((a * a).astype(BF16), w2_ref[...])
    if final_norm:
        y = y * lax.rsqrt(jnp.mean(y * y, axis=-1, keepdims=True) + NORM_EPS) * fin_ref[...]
    o_ref[0] = y


def _ffn_call(x, mod, gain, final_gain, w1, w2, final_norm):
    B, T, _ = x.shape
    tm = TM_FFN
    tok = pl.BlockSpec((1, tm, D_MODEL), lambda b, i: (b, i, 0))
    whole = pl.BlockSpec(memory_space=pltpu.VMEM)
    return pl.pallas_call(
        functools.partial(_ffn_kernel, final_norm=final_norm),
        grid=(B, T // tm),
        in_specs=[tok, pl.BlockSpec((1, 6, D_MODEL), lambda b, i: (b, 0, 0)), _resident((1, D_MODEL)),
                  _resident((1, D_MODEL)), whole, whole],
        out_specs=tok,
        out_shape=jax.ShapeDtypeStruct((B, T, D_MODEL), F32),
        compiler_params=_params(("arbitrary", "arbitrary")),
        name="ffn",
    )(x, mod, gain, final_gain, w1, w2)


def _rotary_tables(T):
    half = HEAD_DIM // 2
    inv = (ROPE_BASE ** (-np.arange(half, dtype=np.float32) / half)).astype(np.float32)
    ang = (np.arange(T, dtype=np.float32)[:, None] * inv[None, :]).astype(np.float32).astype(np.float64)
    lane = np.arange(LANES) % HEAD_DIM
    cos = np.cos(ang)[:, lane % half]
    sin = np.sin(ang)[:, lane % half] * np.where(lane < half, -1.0, 1.0)[None, :]
    return jnp.asarray(cos, F32), jnp.asarray(sin, F32)


def _trunk(x, mods, layer_params, shared):
    B, T, _ = x.shape
    cos_t, sin_t = _rotary_tables(T)
    rows = T // GRID_W
    for l, lp in enumerate(layer_params):
        mod = mods[l]
        ret_in, u, naq, nak, nav, sq, sk, sv = _proj_call(x, mod, lp["gain1"], lp["w_in"], cos_t, sin_t)
        ret = _ret_call(ret_in, lp["ret_tables"], lp["ret_gn_gain"])
        conv = _conv_call(u, lp["conv_dw_kernel"], lp["conv_dw_bias"], lp["conv_ln_gain"], lp["conv_ln_bias"])
        na = _na_call(naq, nak, nav, lp["na_tables"][rows])
        swa = _swa_call(sq, sk, sv, shared["swa_table"], lp["swa_sink"])
        x = _merge_call(x, mod, lp["gain1"], (ret, conv, na, swa), lp["w_merge"], lp["w_branch"], lp["w_out"])
        x = _ffn_call(x, mod, lp["gain2"], shared["final_gain"], lp["w_ff1"], lp["w_ff2"], l == len(layer_params) - 1)
    return x


def kernel(x_prompt, x_sample, c_prompt, c_sample, w_ada, b_ada, norm_gain, w_in, ret_decay_logit, ret_gn_gain,
           conv_dw_kernel, conv_dw_bias, conv_ln_gain, conv_ln_bias, na_rpb, swa_sink, t5_bias,
           w_branch, w_merge, w_out, w_ff1, w_ff2, final_gain):
    nbp = c_prompt.shape[0]
    c_all = jnp.concatenate([c_prompt, c_sample], axis=0)
    mods = _ada_call(c_all, w_ada.astype(BF16), b_ada.astype(F32))
    mods = mods.reshape(DEPTH, c_all.shape[0], 6, D_MODEL)
    row_counts = sorted({x_prompt.shape[1] // GRID_W, x_sample.shape[1] // GRID_W})
    layer_params = []
    for l in range(DEPTH):
        layer_params.append(dict(
            gain1=norm_gain[l, 0].reshape(1, D_MODEL).astype(F32),
            gain2=norm_gain[l, 1].reshape(1, D_MODEL).astype(F32),
            w_in=w_in[l].astype(BF16),
            ret_tables=_ret_tables(ret_decay_logit[l]),
            ret_gn_gain=ret_gn_gain[l],
            conv_dw_kernel=conv_dw_kernel[l], conv_dw_bias=conv_dw_bias[l],
            conv_ln_gain=conv_ln_gain[l], conv_ln_bias=conv_ln_bias[l],
            na_tables=_na_bias_tables(na_rpb[l], row_counts),
            swa_sink=swa_sink[l],
            w_merge=w_merge[l].astype(BF16), w_branch=w_branch[l].astype(BF16), w_out=w_out[l].astype(BF16),
            w_ff1=w_ff1[l].astype(BF16), w_ff2=w_ff2[l].astype(BF16),
        ))
    shared = dict(swa_table=_swa_bias_table(t5_bias), final_gain=final_gain.reshape(1, D_MODEL).astype(F32))
    y_prompt = _trunk(x_prompt, [m[:nbp] for m in mods], layer_params, shared)
    y_sample = _trunk(x_sample, [m[nbp:] for m in mods], layer_params, shared)
    return (y_prompt, y_sample)
```

```python
import functools
import math

import jax
import jax.numpy as jnp
import numpy as np
from jax import lax
from jax.experimental import pallas as pl
from jax.experimental.pallas import tpu as pltpu

F32 = jnp.float32
BF16 = jnp.bfloat16

D_MODEL = 1024
DEPTH = 2
HEAD_DIM = 64
BRANCH_WIDTH = 256
N_BRANCHES = 4
RET_CHUNK = 128
RET_CHUNKS_PER_STEP = 8
ROPE_BASE = 10000.0
CONV_WIDTH = 31
CONV_HALO = 16
GRID_W = 64
NA_WIN_ROWS = 8
NA_WIN_COLS = 16
NA_QROWS = 2
NA_BLOCKS_PER_STEP = 4
SWA_WINDOW = 128
SWA_BLOCK = 128
T5_BUCKETS = 32
T5_MAX_DIST = 128
D_FF = 4 * D_MODEL
NORM_EPS = 1e-6
NEG_INF = -1e30
IN_COLS = 2816
LANES = 128
SUBLANES = 8

TM_PROJ = 512
TM_MERGE = 512
TM_FFN = 512
TT_CONV = 512
CONV_ROWS = 32
TQ_SWA = 512
VMEM_LIMIT = 56 * 1024 * 1024


def _params(sem):
    return pltpu.CompilerParams(dimension_semantics=sem, vmem_limit_bytes=VMEM_LIMIT)


def _resident(shape):
    nd = len(shape)
    return pl.BlockSpec(shape, lambda *_: (0,) * nd)


def _modulated_rmsnorm(x, gain, scale, shift):
    y = x * lax.rsqrt(jnp.mean(x * x, axis=-1, keepdims=True) + NORM_EPS)
    return (y * gain) * (1.0 + scale) + shift


def _dot(a, b):
    return jnp.dot(a, b, preferred_element_type=F32)


def _dot_nt(a, b):
    return lax.dot_general(a, b, (((1,), (1,)), ((), ())), preferred_element_type=F32)


def _dot_tn(a, b):
    return lax.dot_general(a, b, (((0,), (0,)), ((), ())), preferred_element_type=F32)


def _dot_split(x, w):
    hi = x.astype(BF16)
    lo = (x - hi.astype(F32)).astype(BF16)
    return _dot(hi, w) + _dot(lo, w)


def _ada_kernel(c_ref, w_ref, b_ref, o_ref):
    c = c_ref[...]
    a = (c * jax.nn.sigmoid(c)).astype(BF16)
    o_ref[0] = _dot(a, w_ref[0]) + b_ref[0]


def _ada_call(c_all, w_ada, b_ada):
    nb = c_all.shape[0]
    return pl.pallas_call(
        _ada_kernel,
        grid=(DEPTH, 6),
        in_specs=[
            pl.BlockSpec((nb, D_MODEL), lambda l, j: (0, 0)),
            pl.BlockSpec((1, D_MODEL, D_MODEL), lambda l, j: (l, 0, j)),
            pl.BlockSpec((1, 1, D_MODEL), lambda l, j: (l, 0, j)),
        ],
        out_specs=pl.BlockSpec((1, nb, D_MODEL), lambda l, j: (l, 0, j)),
        out_shape=jax.ShapeDtypeStruct((DEPTH, nb, 6 * D_MODEL), F32),
        compiler_params=_params(("arbitrary", "arbitrary")),
        name="ada",
    )(c_all, w_ada, b_ada.reshape(DEPTH, 1, 6 * D_MODEL))


def _proj_kernel(x_ref, mod_ref, gain_ref, w_ref, cos_ref, sin_ref,
                 ret_ref, u_ref, naq_ref, nak_ref, nav_ref, sq_ref, sk_ref, sv_ref):
    tm = x_ref.shape[1]
    h = _modulated_rmsnorm(x_ref[0], gain_ref[...], mod_ref[0, 1:2, :], mod_ref[0, 0:1, :]).astype(BF16)
    cos = cos_ref[...]
    sin = sin_ref[...]
    lane = lax.broadcasted_iota(jnp.int32, (tm, LANES), 1)
    first_half = (lane & (HEAD_DIM // 2)) == 0
    qscale = HEAD_DIM ** -0.5

    def rot(y):
        sw = jnp.where(first_half, pltpu.roll(y, LANES - HEAD_DIM // 2, 1), pltpu.roll(y, HEAD_DIM // 2, 1))
        return y * cos + sw * sin

    y = _dot(h, w_ref[:, 0:1024])
    for t in range(2):
        ret_ref[0, :, t * LANES:(t + 1) * LANES] = (rot(y[:, t * LANES:(t + 1) * LANES]) * qscale).astype(BF16)
    for t in range(2, 4):
        ret_ref[0, :, t * LANES:(t + 1) * LANES] = rot(y[:, t * LANES:(t + 1) * LANES]).astype(BF16)
    ret_ref[0, :, 512:1024] = y[:, 512:1024].astype(BF16)
    y = _dot(h, w_ref[:, 1024:1536])
    u_ref[0] = (y[:, 0:256] * jax.nn.sigmoid(y[:, 256:512])).astype(BF16)
    y = _dot(h, w_ref[:, 1536:2304])
    naq_ref[0] = (y[:, 0:256] * qscale).astype(BF16)
    nak_ref[0] = y[:, 256:512].astype(BF16)
    nav_ref[0] = y[:, 512:768].astype(BF16)
    y = _dot(h, w_ref[:, 2304:2816])
    sq_ref[0] = (y[:, 0:256] * qscale).astype(BF16)
    sk_ref[0] = y[:, 256:384].astype(BF16)
    sv_ref[0] = y[:, 384:512].astype(BF16)


def _proj_call(x, mod, gain, w_in, cos_t, sin_t):
    B, T, _ = x.shape
    tm = TM_PROJ
    tok = lambda w: pl.BlockSpec((1, tm, w), lambda b, i: (b, i, 0))
    widths = (1024, 256, 256, 256, 256, 256, 128, 128)
    return pl.pallas_call(
        _proj_kernel,
        grid=(B, T // tm),
        in_specs=[
            tok(D_MODEL),
            pl.BlockSpec((1, 6, D_MODEL), lambda b, i: (b, 0, 0)),
            _resident((1, D_MODEL)),
            _resident((D_MODEL, IN_COLS)),
            pl.BlockSpec((tm, LANES), lambda b, i: (i, 0)),
            pl.BlockSpec((tm, LANES), lambda b, i: (i, 0)),
        ],
        out_specs=[tok(w) for w in widths],
        out_shape=[jax.ShapeDtypeStruct((B, T, w), BF16) for w in widths],
        compiler_params=_params(("arbitrary", "arbitrary")),
        name="proj",
    )(x, mod, gain, w_in, cos_t, sin_t)


def _ret_kernel(x_ref, dec_ref, xif_ref, xib_ref, zf_ref, zb_ref, gf_ref, gb_ref, bd_ref, avg_ref, gn_ref,
                o_ref, rf_ref, rb_ref, stash_ref):
    p = pl.program_id(1)
    n = pl.program_id(2)
    nsteps = pl.num_programs(2)
    C = RET_CHUNK
    W = BRANCH_WIDTH
    G = x_ref.shape[1] // C
    heads = W // HEAD_DIM

    def chunk_kv(j):
        rows = slice(j * C, (j + 1) * C)
        return x_ref[0, rows, W:2 * W], x_ref[0, rows, 2 * W:3 * W]

    def chunk_state(k, v, zeta):
        return _dot_tn((k.astype(F32) * zeta).astype(BF16), v) * bd_ref[...]

    @pl.when(p == 0)
    def _forward_states():
        @pl.when(n == 0)
        def _():
            rf_ref[...] = jnp.zeros_like(rf_ref)

        updates = [chunk_state(*chunk_kv(j), zf_ref[...]) for j in range(G)]
        rf = rf_ref[...]
        for j in range(G):
            stash_ref[n * G + j] = rf.astype(BF16)
            rf = gf_ref[...] * rf + updates[j]
        rf_ref[...] = rf

    @pl.when(p == 1)
    def _outputs():
        @pl.when(n == 0)
        def _():
            rb_ref[...] = jnp.zeros_like(rb_ref)

        first_chunk = (nsteps - 1 - n) * G
        lane_head = lax.broadcasted_iota(jnp.int32, (C, W), 1) // HEAD_DIM

        def head_rows(a):
            return jnp.concatenate([jnp.where(lane_head == hh, a, jnp.zeros_like(a)) for hh in range(heads)], axis=0)

        qs = [x_ref[0, j * C:(j + 1) * C, 0:W] for j in range(G)]
        weights = [(_dot_nt(qs[j], head_rows(chunk_kv(j)[0])) * dec_ref[...]).astype(BF16) for j in range(G)]
        updates = [chunk_state(*chunk_kv(j), zb_ref[...]) for j in range(G)]
        rb = rb_ref[...]
        later = [None] * G
        for j in reversed(range(G)):
            later[j] = rb.astype(BF16)
            rb = gb_ref[...] * rb + updates[j]
        rb_ref[...] = rb
        outs = []
        for j in range(G):
            qf = qs[j].astype(F32)
            o = _dot(weights[j], head_rows(chunk_kv(j)[1]))
            o = o + _dot((qf * xif_ref[...]).astype(BF16), stash_ref[first_chunk + j])
            outs.append(o + _dot((qf * xib_ref[...]).astype(BF16), later[j]))
        o = jnp.concatenate(outs, axis=0)
        avg = avg_ref[...]
        d = o - _dot_split(o, avg)
        var = _dot_split(d * d, avg)
        g = x_ref[0, :, 3 * W:4 * W].astype(F32)
        o_ref[0] = (d * lax.rsqrt(var + NORM_EPS) * gn_ref[...] * (g * jax.nn.sigmoid(g))).astype(BF16)


def _ret_tables(decay_logit):
    C = RET_CHUNK
    heads = BRANCH_WIDTH // HEAD_DIM
    lg = jax.nn.log_sigmoid(decay_logit.astype(F32))
    pos = np.arange(C, dtype=np.float32)
    diff = pos[:, None] - pos[None, :]
    fwd = jnp.exp(jnp.where(diff >= 0, diff[None] * lg[0][:, None, None], -jnp.inf))
    bwd = jnp.exp(jnp.where(diff < 0, -diff[None] * lg[1][:, None, None], -jnp.inf))
    dec = (fwd + bwd).transpose(1, 0, 2).reshape(C, heads * C)
    lane_lg = jnp.repeat(lg, HEAD_DIM, axis=1)
    xif = jnp.exp((pos + 1.0)[:, None] * lane_lg[0][None, :])
    xib = jnp.exp((C - pos)[:, None] * lane_lg[1][None, :])
    zf = jnp.exp((C - 1.0 - pos)[:, None] * lane_lg[0][None, :])
    zb = jnp.exp(pos[:, None] * lane_lg[1][None, :])
    gf = jnp.exp(C * lane_lg[0])[None, :]
    gb = jnp.exp(C * lane_lg[1])[None, :]
    return dec, xif, xib, zf, zb, gf, gb


def _head_block_constants():
    hid = np.arange(BRANCH_WIDTH) // HEAD_DIM
    same = (hid[:, None] == hid[None, :])
    return jnp.asarray(same.astype(np.float32)), jnp.asarray(same.astype(np.float32) / HEAD_DIM, dtype=BF16)


def _ret_call(ret_in, tables, gn_gain):
    B, T, _ = ret_in.shape
    rows = RET_CHUNK * RET_CHUNKS_PER_STEP
    N = T // RET_CHUNK
    S = T // rows
    bd, avg = _head_block_constants()
    step_rows = lambda b, p, n: (b, jnp.where(p == 0, n, S - 1 - n), 0)
    out_rows = lambda b, p, n: (b, jnp.where(p == 0, S - 1, S - 1 - n), 0)
    consts = list(tables) + [bd, avg, gn_gain.reshape(1, BRANCH_WIDTH).astype(F32)]
    return pl.pallas_call(
        _ret_kernel,
        grid=(B, 2, S),
        in_specs=[pl.BlockSpec((1, rows, 4 * BRANCH_WIDTH), step_rows)] + [_resident(t.shape) for t in consts],
        out_specs=pl.BlockSpec((1, rows, BRANCH_WIDTH), out_rows),
        out_shape=jax.ShapeDtypeStruct((B, T, BRANCH_WIDTH), BF16),
        scratch_shapes=[
            pltpu.VMEM((BRANCH_WIDTH, BRANCH_WIDTH), F32),
            pltpu.VMEM((BRANCH_WIDTH, BRANCH_WIDTH), F32),
            pltpu.VMEM((N, BRANCH_WIDTH, BRANCH_WIDTH), BF16),
        ],
        compiler_params=_params(("arbitrary", "arbitrary", "arbitrary")),
        name="ret",
    )(ret_in, *consts)


def _conv_kernel(u_ref, w_ref, b_ref, lg_ref, lb_ref, o_ref, win_ref):
    i = pl.program_id(1)
    nt = pl.num_programs(1)
    tt = o_ref.shape[1]
    T = u_ref.shape[1]
    t0 = pl.multiple_of(i * tt, tt)
    left_start = pl.multiple_of(jnp.maximum(t0 - CONV_HALO, 0), CONV_HALO)
    right_start = pl.multiple_of(jnp.minimum(t0 + tt, T - CONV_HALO), CONV_HALO)
    left = u_ref[0, pl.ds(left_start, CONV_HALO), :].astype(F32) * (i > 0).astype(F32)
    right = u_ref[0, pl.ds(right_start, CONV_HALO), :].astype(F32) * (i < nt - 1).astype(F32)
    win = jnp.concatenate([left, u_ref[0, pl.ds(t0, tt), :].astype(F32), right], axis=0)
    span = win_ref.shape[1]
    for s in range(SUBLANES):
        win_ref[s] = win[s:s + span]
    off = CONV_HALO - CONV_WIDTH // 2
    for r in range(0, tt, CONV_ROWS):
        acc = jnp.zeros((CONV_ROWS // SUBLANES, SUBLANES, BRANCH_WIDTH), F32)
        for tap in range(CONV_WIDTH):
            s, a = (tap + off) % SUBLANES, (tap + off) // SUBLANES * SUBLANES
            rows = win_ref[s, r + a:r + a + CONV_ROWS, :].reshape(acc.shape)
            acc = acc + rows * w_ref[tap][None]
        acc = acc.reshape(CONV_ROWS, BRANCH_WIDTH) + b_ref[...]
        mu = jnp.mean(acc, axis=-1, keepdims=True)
        d = acc - mu
        var = jnp.mean(d * d, axis=-1, keepdims=True)
        y = d * lax.rsqrt(var + NORM_EPS) * lg_ref[...] + lb_ref[...]
        o_ref[0, r:r + CONV_ROWS, :] = (y * jax.nn.sigmoid(y)).astype(BF16)


def _conv_call(u, dw_kernel, dw_bias, ln_gain, ln_bias):
    B, T, _ = u.shape
    tt = TT_CONV
    row = lambda a: a.reshape(1, BRANCH_WIDTH).astype(F32)
    return pl.pallas_call(
        _conv_kernel,
        grid=(B, T // tt),
        in_specs=[
            pl.BlockSpec((1, T, BRANCH_WIDTH), lambda b, i: (b, 0, 0)),
            _resident((CONV_WIDTH, SUBLANES, BRANCH_WIDTH)),
            _resident((1, BRANCH_WIDTH)),
            _resident((1, BRANCH_WIDTH)),
            _resident((1, BRANCH_WIDTH)),
        ],
        out_specs=pl.BlockSpec((1, tt, BRANCH_WIDTH), lambda b, i: (b, i, 0)),
        out_shape=jax.ShapeDtypeStruct((B, T, BRANCH_WIDTH), BF16),
        scratch_shapes=[pltpu.VMEM((SUBLANES, tt + 2 * CONV_HALO - SUBLANES, BRANCH_WIDTH), F32)],
        compiler_params=_params(("arbitrary", "arbitrary")),
        name="conv",
    )(u, jnp.broadcast_to(dw_kernel.astype(F32)[:, None, :], (CONV_WIDTH, SUBLANES, BRANCH_WIDTH)),
      row(dw_bias), row(ln_gain), row(ln_bias))


def _na_layout(rows):
    R = NA_QROWS
    KR = R + NA_WIN_ROWS
    nblk = rows // R
    qc = np.arange(GRID_W)
    ws = np.clip(qc - NA_WIN_COLS // 2, 0, GRID_W - NA_WIN_COLS)
    col_ok = (qc[None, :] >= ws[:, None]) & (qc[None, :] < ws[:, None] + NA_WIN_COLS)
    dc = qc[None, :] - qc[:, None] + NA_WIN_COLS - 1
    n_dr = 2 * NA_WIN_ROWS - 1
    variants, var_of_block, kr0s = [], [], []
    for j in range(nblk):
        r0 = j * R
        kr0 = int(np.clip(r0 - NA_WIN_ROWS // 2, 0, rows - KR))
        r = r0 + np.arange(R)
        start = np.clip(r - NA_WIN_ROWS // 2, 0, rows - NA_WIN_ROWS)
        kr = kr0 + np.arange(KR)
        row_ok = (kr[None, :] >= start[:, None]) & (kr[None, :] < start[:, None] + NA_WIN_ROWS)
        tile = np.where(row_ok, kr[None, :] - r[:, None] + NA_WIN_ROWS - 1, n_dr)
        key = tile.tobytes()
        if key not in [v[0] for v in variants]:
            variants.append((key, tile))
        var_of_block.append([v[0] for v in variants].index(key))
        kr0s.append(kr0)
    tiles = np.stack([v[1] for v in variants])
    onehot = (dc[None] == np.arange(2 * NA_WIN_COLS - 1)[:, None, None]) & col_ok[None]
    return tiles, onehot, col_ok, np.asarray(var_of_block, np.int32), np.asarray(kr0s, np.int32)


def _na_bias_table(rpb, rows):
    tiles, onehot, col_ok, _, _ = _na_layout(rows)
    H, n_dr, n_dc = rpb.shape
    oh = jnp.asarray(onehot.reshape(n_dc, GRID_W * GRID_W), F32)
    tz = jnp.dot(rpb.astype(F32).reshape(H * n_dr, n_dc), oh, precision=lax.Precision.HIGHEST)
    tz = jnp.where(col_ok[None, None], tz.reshape(H, n_dr, GRID_W, GRID_W), NEG_INF)
    tz = jnp.concatenate([tz, jnp.full((H, 1, GRID_W, GRID_W), NEG_INF, F32)], axis=1)
    nv, R, KR = tiles.shape
    per_variant = []
    for vv in range(nv):
        slabs = [jnp.concatenate([tz[:, int(tiles[vv, rl, kl])] for kl in range(KR)], axis=-1) for rl in range(R)]
        per_variant.append(jnp.concatenate(slabs, axis=1))
    return jnp.stack(per_variant)


def _na_bias_tables(rpb, row_counts):
    by_layout, out = {}, {}
    for rows in row_counts:
        key = _na_layout(rows)[0].tobytes()
        if key not in by_layout:
            by_layout[key] = _na_bias_table(rpb, rows)
        out[rows] = by_layout[key]
    return out


def _na_kernel(var_ref, kr0_ref, q_ref, k_ref, v_ref, *rest):
    tab_refs, o_ref = rest[:-1], rest[-1]
    step = pl.program_id(1)
    nq = q_ref.shape[1] // NA_BLOCKS_PER_STEP
    nk = tab_refs[0].shape[3]
    low = lax.broadcasted_iota(jnp.int32, (nq, LANES), 1) < HEAD_DIM
    units = [(sb, pair) for sb in range(len(tab_refs)) for pair in range(BRANCH_WIDTH // LANES)]
    kstart = [pl.multiple_of(kr0_ref[step * NA_BLOCKS_PER_STEP + sb] * GRID_W, GRID_W) for sb in range(len(tab_refs))]
    scores = []
    for sb, pair in units:
        cols = slice(pair * LANES, (pair + 1) * LANES)
        q2 = q_ref[0, sb * nq:(sb + 1) * nq, cols]
        zero = jnp.zeros_like(q2)
        qs = jnp.concatenate([jnp.where(low, q2, zero), jnp.where(low, zero, q2)], axis=0)
        scores.append(_dot_nt(qs, k_ref[0, pl.ds(kstart[sb], nk), cols]) + tab_refs[sb][0, pair])
    weights = []
    for s in scores:
        e = jnp.exp(s - jnp.max(s, axis=-1, keepdims=True))
        weights.append((e.astype(BF16), 1.0 / jnp.sum(e, axis=-1, keepdims=True)))
    for (sb, pair), (e, inv) in zip(units, weights):
        cols = slice(pair * LANES, (pair + 1) * LANES)
        o2 = _dot(e, v_ref[0, pl.ds(kstart[sb], nk), cols]) * inv
        o_ref[0, sb * nq:(sb + 1) * nq, cols] = jnp.where(low, o2[:nq], o2[nq:]).astype(BF16)


def _na_call(q, k, v, tab):
    B, T, _ = q.shape
    rows = T // GRID_W
    _, _, _, var_of_block, kr0s = _na_layout(rows)
    nq = NA_QROWS * GRID_W
    per_step = NA_BLOCKS_PER_STEP
    nblk = rows // NA_QROWS
    assert nblk % per_step == 0
    nv, heads, _, nk = tab.shape
    tab = tab.reshape(nv, heads // 2, 2 * nq, nk)
    tab_spec = lambda sb: pl.BlockSpec((1, heads // 2, 2 * nq, nk),
                                       lambda b, j, var, kr0: (var[j * per_step + sb], 0, 0, 0))
    grid_spec = pltpu.PrefetchScalarGridSpec(
        num_scalar_prefetch=2,
        grid=(B, nblk // per_step),
        in_specs=[
            pl.BlockSpec((1, per_step * nq, BRANCH_WIDTH), lambda b, j, var, kr0: (b, j, 0)),
            pl.BlockSpec((1, T, BRANCH_WIDTH), lambda b, j, var, kr0: (b, 0, 0)),
            pl.BlockSpec((1, T, BRANCH_WIDTH), lambda b, j, var, kr0: (b, 0, 0)),
        ] + [tab_spec(sb) for sb in range(per_step)],
        out_specs=pl.BlockSpec((1, per_step * nq, BRANCH_WIDTH), lambda b, j, var, kr0: (b, j, 0)),
    )
    return pl.pallas_call(
        _na_kernel,
        grid_spec=grid_spec,
        out_shape=jax.ShapeDtypeStruct((B, T, BRANCH_WIDTH), BF16),
        compiler_params=_params(("arbitrary", "arbitrary")),
        name="na",
    )(jnp.asarray(var_of_block), jnp.asarray(kr0s), q, k, v, *([tab] * per_step))


def _t5_bucket(rel):
    half = T5_BUCKETS // 2
    exact = half // 2
    n = np.abs(rel)
    large = exact + (np.log(np.maximum(n, 1) / exact) / math.log(T5_MAX_DIST / exact) * (half - exact)).astype(np.int64)
    large = np.minimum(large, half - 1)
    return (rel > 0).astype(np.int64) * half + np.where(n < exact, n, large)


def _swa_bias_table(t5_bias):
    kpos = np.arange(3 * SWA_BLOCK) - SWA_BLOCK
    rel = kpos[None, :] - np.arange(SWA_BLOCK)[:, None]
    onehot = (_t5_bucket(rel)[None] == np.arange(T5_BUCKETS)[:, None, None])
    oh = jnp.asarray(onehot.reshape(T5_BUCKETS, -1), F32)
    tab = jnp.dot(t5_bias.astype(F32).T, oh, precision=lax.Precision.HIGHEST).reshape((-1,) + rel.shape)
    tab = jnp.where((np.abs(rel) <= SWA_WINDOW)[None], tab, NEG_INF)
    before = (kpos < 0)[None, None, :]
    after = (kpos >= SWA_BLOCK)[None, None, :]
    tabs = jnp.stack([jnp.where(before, NEG_INF, tab), tab, jnp.where(after, NEG_INF, tab)])
    hq = tab.shape[0]
    return tabs.reshape(3, hq // 2, 2 * SWA_BLOCK, 3 * SWA_BLOCK)


def _swa_kernel(sink_ref, q_ref, k_ref, v_ref, tab_ref, o_ref):
    n = pl.program_id(1)
    tq = q_ref.shape[1]
    T = k_ref.shape[1]
    nb = T // SWA_BLOCK
    per_step = tq // SWA_BLOCK
    low = lax.broadcasted_iota(jnp.int32, (SWA_BLOCK, LANES), 1) < HEAD_DIM
    first_rows = lax.broadcasted_iota(jnp.int32, (2 * SWA_BLOCK, 1), 0) < SWA_BLOCK
    units = [(i, hk) for i in range(per_step) for hk in range(2)]

    def band(ref, blk):
        prev_start = pl.multiple_of(jnp.maximum(blk - 1, 0) * SWA_BLOCK, SWA_BLOCK)
        cur_start = pl.multiple_of(blk * SWA_BLOCK, SWA_BLOCK)
        next_start = pl.multiple_of(jnp.minimum(blk + 1, nb - 1) * SWA_BLOCK, SWA_BLOCK)
        return jnp.concatenate([ref[0, pl.ds(prev_start, SWA_BLOCK), :],
                                ref[0, pl.ds(cur_start, SWA_BLOCK), :],
                                ref[0, pl.ds(next_start, SWA_BLOCK), :]], axis=0)

    scores = []
    for i, hk in units:
        blk = n * per_step + i
        variant = jnp.where(blk == 0, 0, jnp.where(blk == nb - 1, 2, 1))
        q2 = q_ref[0, i * SWA_BLOCK:(i + 1) * SWA_BLOCK, hk * LANES:(hk + 1) * LANES]
        q2r = pltpu.roll(q2, HEAD_DIM, 1)
        zero = jnp.zeros_like(q2)
        if hk == 0:
            qs = jnp.concatenate([jnp.where(low, q2, zero), jnp.where(low, q2r, zero)], axis=0)
        else:
            qs = jnp.concatenate([jnp.where(low, zero, q2r), jnp.where(low, zero, q2)], axis=0)
        scores.append(_dot_nt(qs, band(k_ref, blk)) + tab_ref[variant, hk])
    weights = []
    for (i, hk), s in zip(units, scores):
        sink = jnp.where(first_rows, sink_ref[2 * hk], sink_ref[2 * hk + 1])
        m = jnp.maximum(jnp.max(s, axis=-1, keepdims=True), sink)
        e = jnp.exp(s - m)
        denom = jnp.sum(e, axis=-1, keepdims=True) + jnp.exp(sink - m)
        weights.append((e.astype(BF16), 1.0 / denom))
    for (i, hk), (e, inv) in zip(units, weights):
        o2 = _dot(e, band(v_ref, n * per_step + i)) * inv
        top, bot = o2[:SWA_BLOCK], o2[SWA_BLOCK:]
        if hk == 0:
            out = jnp.where(low, top, pltpu.roll(bot, HEAD_DIM, 1))
        else:
            out = jnp.where(low, pltpu.roll(top, HEAD_DIM, 1), bot)
        o_ref[0, i * SWA_BLOCK:(i + 1) * SWA_BLOCK, hk * LANES:(hk + 1) * LANES] = out.astype(BF16)


def _swa_call(q, k, v, tab, sink):
    B, T, _ = q.shape
    tq = TQ_SWA
    assert T // SWA_BLOCK >= 2
    return pl.pallas_call(
        _swa_kernel,
        grid=(B, T // tq),
        in_specs=[
            pl.BlockSpec(memory_space=pltpu.SMEM),
            pl.BlockSpec((1, tq, BRANCH_WIDTH), lambda b, i: (b, i, 0)),
            pl.BlockSpec((1, T, LANES), lambda b, i: (b, 0, 0)),
            pl.BlockSpec((1, T, LANES), lambda b, i: (b, 0, 0)),
            _resident(tab.shape),
        ],
        out_specs=pl.BlockSpec((1, tq, BRANCH_WIDTH), lambda b, i: (b, i, 0)),
        out_shape=jax.ShapeDtypeStruct((B, T, BRANCH_WIDTH), BF16),
        compiler_params=_params(("arbitrary", "arbitrary")),
        name="swa",
    )(sink.astype(F32), q, k, v, tab)


def _merge_kernel(x_ref, mod_ref, gain_ref, b0_ref, b1_ref, b2_ref, b3_ref, wm_ref, wb_ref, wo_ref, o_ref):
    x = x_ref[0]
    h = _modulated_rmsnorm(x, gain_ref[...], mod_ref[0, 1:2, :], mod_ref[0, 0:1, :]).astype(BF16)
    merged = None
    for i, br in enumerate((b0_ref, b1_ref, b2_ref, b3_ref)):
        term = jax.nn.sigmoid(_dot(h, wm_ref[i])) * _dot(br[0], wb_ref[i])
        merged = term if merged is None else merged + term
    o_ref[0] = x + mod_ref[0, 2:3, :] * _dot(merged.astype(BF16), wo_ref[...])


def _merge_call(x, mod, gain, branches, w_merge, w_branch, w_out):
    B, T, _ = x.shape
    tm = TM_MERGE
    tok = lambda w: pl.BlockSpec((1, tm, w), lambda b, i: (b, i, 0))
    whole = pl.BlockSpec(memory_space=pltpu.VMEM)
    return pl.pallas_call(
        _merge_kernel,
        grid=(B, T // tm),
        in_specs=[tok(D_MODEL), pl.BlockSpec((1, 6, D_MODEL), lambda b, i: (b, 0, 0)), _resident((1, D_MODEL))]
                 + [tok(BRANCH_WIDTH)] * N_BRANCHES + [whole, whole, whole],
        out_specs=tok(D_MODEL),
        out_shape=jax.ShapeDtypeStruct((B, T, D_MODEL), F32),
        compiler_params=_params(("arbitrary", "arbitrary")),
        name="merge",
    )(x, mod, gain, *branches, w_merge, w_branch, w_out)


def _ffn_kernel(x_ref, mod_ref, gain_ref, fin_ref, w1_ref, w2_ref, o_ref, *, final_norm):
    x = x_ref[0]
    h = _modulated_rmsnorm(x, gain_ref[...], mod_ref[0, 4:5, :], mod_ref[0, 3:4, :]).astype(BF16)
    a = jnp.maximum(_dot(h, w1_ref[...]), 0.0)
    y = x + mod_ref[0, 5:6, :] * _dot((a * a).astype(BF16), w2_ref[...])
    if final_norm:
        y = y * lax.rsqrt(jnp.mean(y * y, axis=-1, keepdims=True) + NORM_EPS) * fin_ref[...]
    o_ref[0] = y


def _ffn_call(x, mod, gain, final_gain, w1, w2, final_norm):
    B, T, _ = x.shape
    tm = TM_FFN
    tok = pl.BlockSpec((1, tm, D_MODEL), lambda b, i: (b, i, 0))
    whole = pl.BlockSpec(memory_space=pltpu.VMEM)
    return pl.pallas_call(
        functools.partial(_ffn_kernel, final_norm=final_norm),
        grid=(B, T // tm),
        in_specs=[tok, pl.BlockSpec((1, 6, D_MODEL), lambda b, i: (b, 0, 0)), _resident((1, D_MODEL)),
                  _resident((1, D_MODEL)), whole, whole],
        out_specs=tok,
        out_shape=jax.ShapeDtypeStruct((B, T, D_MODEL), F32),
        compiler_params=_params(("arbitrary", "arbitrary")),
        name="ffn",
    )(x, mod, gain, final_gain, w1, w2)


def _rotary_tables(T):
    half = HEAD_DIM // 2
    inv = (ROPE_BASE ** (-np.arange(half, dtype=np.float32) / half)).astype(np.float32)
    ang = (np.arange(T, dtype=np.float32)[:, None] * inv[None, :]).astype(np.float32).astype(np.float64)
    lane = np.arange(LANES) % HEAD_DIM
    cos = np.cos(ang)[:, lane % half]
    sin = np.sin(ang)[:, lane % half] * np.where(lane < half, -1.0, 1.0)[None, :]
    return jnp.asarray(cos, F32), jnp.asarray(sin, F32)


def _trunk(x, mods, layer_params, shared):
    B, T, _ = x.shape
    cos_t, sin_t = _rotary_tables(T)
    rows = T // GRID_W
    for l, lp in enumerate(layer_params):
        mod = mods[l]
        ret_in, u, naq, nak, nav, sq, sk, sv = _proj_call(x, mod, lp["gain1"], lp["w_in"], cos_t, sin_t)
        ret = _ret_call(ret_in, lp["ret_tables"], lp["ret_gn_gain"])
        conv = _conv_call(u, lp["conv_dw_kernel"], lp["conv_dw_bias"], lp["conv_ln_gain"], lp["conv_ln_bias"])
        na = _na_call(naq, nak, nav, lp["na_tables"][rows])
        swa = _swa_call(sq, sk, sv, shared["swa_table"], lp["swa_sink"])
        x = _merge_call(x, mod, lp["gain1"], (ret, conv, na, swa), lp["w_merge"], lp["w_branch"], lp["w_out"])
        x = _ffn_call(x, mod, lp["gain2"], shared["final_gain"], lp["w_ff1"], lp["w_ff2"], l == len(layer_params) - 1)
    return x


def kernel(x_prompt, x_sample, c_prompt, c_sample, w_ada, b_ada, norm_gain, w_in, ret_decay_logit, ret_gn_gain,
           conv_dw_kernel, conv_dw_bias, conv_ln_gain, conv_ln_bias, na_rpb, swa_sink, t5_bias,
           w_branch, w_merge, w_out, w_ff1, w_ff2, final_gain):
    nbp = c_prompt.shape[0]
    c_all = jnp.concatenate([c_prompt, c_sample], axis=0)
    mods = _ada_call(c_all, w_ada.astype(BF16), b_ada.astype(F32))
    mods = mods.reshape(DEPTH, c_all.shape[0], 6, D_MODEL)
    row_counts = sorted({x_prompt.shape[1] // GRID_W, x_sample.shape[1] // GRID_W})
    layer_params = []
    for l in range(DEPTH):
        layer_params.append(dict(
            gain1=norm_gain[l, 0].reshape(1, D_MODEL).astype(F32),
            gain2=norm_gain[l, 1].reshape(1, D_MODEL).astype(F32),
            w_in=w_in[l].astype(BF16),
            ret_tables=_ret_tables(ret_decay_logit[l]),
            ret_gn_gain=ret_gn_gain[l],
            conv_dw_kernel=conv_dw_kernel[l], conv_dw_bias=conv_dw_bias[l],
            conv_ln_gain=conv_ln_gain[l], conv_ln_bias=conv_ln_bias[l],
            na_tables=_na_bias_tables(na_rpb[l], row_counts),
            swa_sink=swa_sink[l],
            w_merge=w_merge[l].astype(BF16), w_branch=w_branch[l].astype(BF16), w_out=w_out[l].astype(BF16),
            w_ff1=w_ff1[l].astype(BF16), w_ff2=w_ff2[l].astype(BF16),
        ))
    shared = dict(swa_table=_swa_bias_table(t5_bias), final_gain=final_gain.reshape(1, D_MODEL).astype(F32))
    y_prompt = _trunk(x_prompt, [m[:nbp] for m in mods], layer_params, shared)
    y_sample = _trunk(x_sample, [m[nbp:] for m in mods], layer_params, shared)
    return (y_prompt, y_sample)
```

```python
import functools
import math

import jax
import jax.numpy as jnp
import numpy as np
from jax import lax
from jax.experimental import pallas as pl
from jax.experimental.pallas import tpu as pltpu

F32 = jnp.float32
BF16 = jnp.bfloat16

D_MODEL = 1024
DEPTH = 2
HEAD_DIM = 64
BRANCH_WIDTH = 256
N_BRANCHES = 4
RET_CHUNK = 128
RET_CHUNKS_PER_STEP = 8
ROPE_BASE = 10000.0
CONV_WIDTH = 31
CONV_HALO = 16
GRID_W = 64
NA_WIN_ROWS = 8
NA_WIN_COLS = 16
NA_QROWS = 2
NA_BLOCKS_PER_STEP = 4
SWA_WINDOW = 128
SWA_BLOCK = 128
T5_BUCKETS = 32
T5_MAX_DIST = 128
D_FF = 4 * D_MODEL
NORM_EPS = 1e-6
NEG_INF = -1e30
SWA_KV_WIDTH = 128
RET_COL0 = 0
CONV_COL0 = RET_COL0 + 4 * BRANCH_WIDTH
NA_COL0 = CONV_COL0 + 2 * BRANCH_WIDTH
SWA_COL0 = NA_COL0 + 3 * BRANCH_WIDTH
IN_COLS = SWA_COL0 + BRANCH_WIDTH + 2 * SWA_KV_WIDTH
LANES = 128
SUBLANES = 8

TM_PROJ = 512
PROJ_ROW_SPLITS = 2
TM_MERGE = 512
TM_FFN = 512
CONV_ROWS = 32
TQ_SWA = 512
VMEM_LIMIT = 56 * 1024 * 1024


def _params(sem, flags=None):
    return pltpu.CompilerParams(dimension_semantics=sem, vmem_limit_bytes=VMEM_LIMIT, flags=flags)


def _resident(shape):
    nd = len(shape)
    return pl.BlockSpec(shape, lambda *_: (0,) * nd)


def _modulated_rmsnorm(x, gain, scale, shift):
    y = x * lax.rsqrt(jnp.mean(x * x, axis=-1, keepdims=True) + NORM_EPS)
    return (y * gain) * (1.0 + scale) + shift


def _dot(a, b):
    return jnp.dot(a, b, preferred_element_type=F32)


def _dot_nt(a, b):
    return lax.dot_general(a, b, (((1,), (1,)), ((), ())), preferred_element_type=F32)


def _dot_tn(a, b):
    return lax.dot_general(a, b, (((0,), (0,)), ((), ())), preferred_element_type=F32)


def _dot_split(x, w):
    hi = x.astype(BF16)
    lo = (x - hi.astype(F32)).astype(BF16)
    return _dot(hi, w) + _dot(lo, w)


def _ada_kernel(c_ref, w_ref, b_ref, o_ref):
    c = c_ref[...]
    a = (c * jax.nn.sigmoid(c)).astype(BF16)
    o_ref[0] = _dot(a, w_ref[0]) + b_ref[0]


def _ada_call(c_all, w_ada, b_ada):
    nb = c_all.shape[0]
    return pl.pallas_call(
        _ada_kernel,
        grid=(DEPTH, 6),
        in_specs=[
            pl.BlockSpec((nb, D_MODEL), lambda l, j: (0, 0)),
            pl.BlockSpec((1, D_MODEL, D_MODEL), lambda l, j: (l, 0, j)),
            pl.BlockSpec((1, 1, D_MODEL), lambda l, j: (l, 0, j)),
        ],
        out_specs=pl.BlockSpec((1, nb, D_MODEL), lambda l, j: (l, 0, j)),
        out_shape=jax.ShapeDtypeStruct((DEPTH, nb, 6 * D_MODEL), F32),
        compiler_params=_params(("arbitrary", "arbitrary")),
        name="ada",
    )(c_all, w_ada, b_ada.reshape(DEPTH, 1, 6 * D_MODEL))


def _proj_kernel(x_ref, mod_ref, gain_ref, w_ref, cos_ref, sin_ref,
                 ret_ref, u_ref, naq_ref, nak_ref, nav_ref, sq_ref, sk_ref, sv_ref):
    tm = x_ref.shape[1]
    W = BRANCH_WIDTH
    half = tm // PROJ_ROW_SPLITS
    lane = lax.broadcasted_iota(jnp.int32, (half, LANES), 1)
    first_half = (lane & (HEAD_DIM // 2)) == 0
    qscale = HEAD_DIM ** -0.5
    blocks = [slice(n * half, (n + 1) * half) for n in range(PROJ_ROW_SPLITS)]
    hs = [_modulated_rmsnorm(x_ref[0, rows, :], gain_ref[...], mod_ref[0, 1:2, :], mod_ref[0, 0:1, :]).astype(BF16)
          for rows in blocks]
    for rows, h in zip(blocks, hs):
        cos = cos_ref[rows, :]
        sin = sin_ref[rows, :]

        def rot(y):
            sw = jnp.where(first_half, pltpu.roll(y, LANES - HEAD_DIM // 2, 1), pltpu.roll(y, HEAD_DIM // 2, 1))
            return y * cos + sw * sin

        y = _dot(h, w_ref[:, RET_COL0:RET_COL0 + 4 * W])
        for t in range(W // LANES):
            ret_ref[0, rows, t * LANES:(t + 1) * LANES] = (rot(y[:, t * LANES:(t + 1) * LANES]) * qscale).astype(BF16)
        for t in range(W // LANES, 2 * W // LANES):
            ret_ref[0, rows, t * LANES:(t + 1) * LANES] = rot(y[:, t * LANES:(t + 1) * LANES]).astype(BF16)
        ret_ref[0, rows, 2 * W:4 * W] = y[:, 2 * W:4 * W].astype(BF16)
        y = _dot(h, w_ref[:, CONV_COL0:CONV_COL0 + 2 * W])
        u_ref[0, rows, :] = (y[:, 0:W] * jax.nn.sigmoid(y[:, W:2 * W])).astype(BF16)
        y = _dot(h, w_ref[:, NA_COL0:NA_COL0 + 3 * W])
        naq_ref[0, rows, :] = (y[:, 0:W] * qscale).astype(BF16)
        nak_ref[0, rows, :] = y[:, W:2 * W].astype(BF16)
        nav_ref[0, rows, :] = y[:, 2 * W:3 * W].astype(BF16)
        y = _dot(h, w_ref[:, SWA_COL0:IN_COLS])
        sq_ref[0, rows, :] = (y[:, 0:W] * qscale).astype(BF16)
        sk_ref[0, rows, :] = y[:, W:W + SWA_KV_WIDTH].astype(BF16)
        sv_ref[0, rows, :] = y[:, W + SWA_KV_WIDTH:W + 2 * SWA_KV_WIDTH].astype(BF16)


def _proj_call(x, mod, gain, w_in, cos_t, sin_t):
    B, T, _ = x.shape
    tm = TM_PROJ
    tok = lambda w: pl.BlockSpec((1, tm, w), lambda b, i: (b, i, 0))
    W = BRANCH_WIDTH
    widths = (4 * W, W, W, W, W, W, SWA_KV_WIDTH, SWA_KV_WIDTH)
    return pl.pallas_call(
        _proj_kernel,
        grid=(B, T // tm),
        in_specs=[
            tok(D_MODEL),
            pl.BlockSpec((1, 6, D_MODEL), lambda b, i: (b, 0, 0)),
            _resident((1, D_MODEL)),
            _resident((D_MODEL, IN_COLS)),
            pl.BlockSpec((tm, LANES), lambda b, i: (i, 0)),
            pl.BlockSpec((tm, LANES), lambda b, i: (i, 0)),
        ],
        out_specs=[tok(w) for w in widths],
        out_shape=[jax.ShapeDtypeStruct((B, T, w), BF16) for w in widths],
        compiler_params=_params(("arbitrary", "arbitrary")),
        name="proj",
    )(x, mod, gain, w_in, cos_t, sin_t)


def _ret_kernel(x_ref, dec_ref, xif_ref, xib_ref, zf_ref, zb_ref, gf_ref, gb_ref, bd_ref, avg_ref, gn_ref,
                o_ref, rf_ref, rb_ref, stash_ref):
    p = pl.program_id(1)
    n = pl.program_id(2)
    nsteps = pl.num_programs(2)
    C = RET_CHUNK
    W = BRANCH_WIDTH
    G = x_ref.shape[1] // C
    heads = W // HEAD_DIM

    def chunk_kv(j):
        rows = slice(j * C, (j + 1) * C)
        return x_ref[0, rows, W:2 * W], x_ref[0, rows, 2 * W:3 * W]

    def chunk_state(k, v, zeta):
        return _dot_tn((k.astype(F32) * zeta).astype(BF16), v) * bd_ref[...]

    @pl.when(p == 0)
    def _forward_states():
        @pl.when(n == 0)
        def _():
            rf_ref[...] = jnp.zeros_like(rf_ref)

        updates = [chunk_state(*chunk_kv(j), zf_ref[...]) for j in range(G)]
        rf = rf_ref[...]
        for j in range(G):
            stash_ref[n * G + j] = rf.astype(BF16)
            rf = gf_ref[...] * rf + updates[j]
        rf_ref[...] = rf

    @pl.when(p == 1)
    def _outputs():
        @pl.when(n == 0)
        def _():
            rb_ref[...] = jnp.zeros_like(rb_ref)

        first_chunk = (nsteps - 1 - n) * G
        lane_head = lax.broadcasted_iota(jnp.int32, (C, W), 1) // HEAD_DIM

        def head_rows(a):
            return jnp.concatenate([jnp.where(lane_head == hh, a, jnp.zeros_like(a)) for hh in range(heads)], axis=0)

        qs = [x_ref[0, j * C:(j + 1) * C, 0:W] for j in range(G)]
        weights = [(_dot_nt(qs[j], head_rows(chunk_kv(j)[0])) * dec_ref[...]).astype(BF16) for j in range(G)]
        updates = [chunk_state(*chunk_kv(j), zb_ref[...]) for j in range(G)]
        rb = rb_ref[...]
        later = [None] * G
        for j in reversed(range(G)):
            later[j] = rb.astype(BF16)
            rb = gb_ref[...] * rb + updates[j]
        rb_ref[...] = rb
        outs = []
        for j in range(G):
            qf = qs[j].astype(F32)
            o = _dot(weights[j], head_rows(chunk_kv(j)[1]))
            o = o + _dot((qf * xif_ref[...]).astype(BF16), stash_ref[first_chunk + j])
            outs.append(o + _dot((qf * xib_ref[...]).astype(BF16), later[j]))
        o = jnp.concatenate(outs, axis=0)
        avg = avg_ref[...]
        d = o - _dot_split(o, avg)
        var = _dot_split(d * d, avg)
        g = x_ref[0, :, 3 * W:4 * W].astype(F32)
        o_ref[0] = (d * lax.rsqrt(var + NORM_EPS) * gn_ref[...] * (g * jax.nn.sigmoid(g))).astype(BF16)


def _ret_tables(decay_logit):
    C = RET_CHUNK
    heads = BRANCH_WIDTH // HEAD_DIM
    lg = jax.nn.log_sigmoid(decay_logit.astype(F32))
    pos = np.arange(C, dtype=np.float32)
    diff = pos[:, None] - pos[None, :]
    fwd = jnp.exp(jnp.where(diff >= 0, diff[None] * lg[0][:, None, None], -jnp.inf))
    bwd = jnp.exp(jnp.where(diff < 0, -diff[None] * lg[1][:, None, None], -jnp.inf))
    dec = (fwd + bwd).transpose(1, 0, 2).reshape(C, heads * C)
    lane_lg = jnp.repeat(lg, HEAD_DIM, axis=1)
    xif = jnp.exp((pos + 1.0)[:, None] * lane_lg[0][None, :])
    xib = jnp.exp((C - pos)[:, None] * lane_lg[1][None, :])
    zf = jnp.exp((C - 1.0 - pos)[:, None] * lane_lg[0][None, :])
    zb = jnp.exp(pos[:, None] * lane_lg[1][None, :])
    gf = jnp.exp(C * lane_lg[0])[None, :]
    gb = jnp.exp(C * lane_lg[1])[None, :]
    return dec, xif, xib, zf, zb, gf, gb


def _head_block_constants():
    hid = np.arange(BRANCH_WIDTH) // HEAD_DIM
    same = (hid[:, None] == hid[None, :])
    return jnp.asarray(same.astype(np.float32)), jnp.asarray(same.astype(np.float32) / HEAD_DIM, dtype=BF16)


def _ret_call(ret_in, tables, gn_gain):
    B, T, _ = ret_in.shape
    rows = RET_CHUNK * RET_CHUNKS_PER_STEP
    N = T // RET_CHUNK
    S = T // rows
    bd, avg = _head_block_constants()
    step_rows = lambda b, p, n: (b, jnp.where(p == 0, n, S - 1 - n), 0)
    out_rows = lambda b, p, n: (b, jnp.where(p == 0, S - 1, S - 1 - n), 0)
    consts = list(tables) + [bd, avg, gn_gain.reshape(1, BRANCH_WIDTH).astype(F32)]
    return pl.pallas_call(
        _ret_kernel,
        grid=(B, 2, S),
        in_specs=[pl.BlockSpec((1, rows, 4 * BRANCH_WIDTH), step_rows)] + [_resident(t.shape) for t in consts],
        out_specs=pl.BlockSpec((1, rows, BRANCH_WIDTH), out_rows),
        out_shape=jax.ShapeDtypeStruct((B, T, BRANCH_WIDTH), BF16),
        scratch_shapes=[
            pltpu.VMEM((BRANCH_WIDTH, BRANCH_WIDTH), F32),
            pltpu.VMEM((BRANCH_WIDTH, BRANCH_WIDTH), F32),
            pltpu.VMEM((N, BRANCH_WIDTH, BRANCH_WIDTH), BF16),
        ],
        compiler_params=_params(("arbitrary", "arbitrary", "arbitrary")),
        name="ret",
    )(ret_in, *consts)


def _conv_fill_window(u_ref, win_ref, i, nt, tt):
    T = u_ref.shape[1]
    t0 = pl.multiple_of(i * tt, tt)
    left_start = pl.multiple_of(jnp.maximum(t0 - CONV_HALO, 0), CONV_HALO)
    right_start = pl.multiple_of(jnp.minimum(t0 + tt, T - CONV_HALO), CONV_HALO)
    left = u_ref[0, pl.ds(left_start, CONV_HALO), :].astype(F32) * (i > 0).astype(F32)
    right = u_ref[0, pl.ds(right_start, CONV_HALO), :].astype(F32) * (i < nt - 1).astype(F32)
    win = jnp.concatenate([left, u_ref[0, pl.ds(t0, tt), :].astype(F32), right], axis=0)
    span = win_ref.shape[1]
    for s in range(SUBLANES):
        win_ref[s] = win[s:s + span]


def _conv_rows(win_ref, w_ref, b_ref, lg_ref, lb_ref, r):
    off = CONV_HALO - CONV_WIDTH // 2
    acc = jnp.zeros((CONV_ROWS // SUBLANES, SUBLANES, BRANCH_WIDTH), F32)
    for tap in range(CONV_WIDTH):
        s, a = (tap + off) % SUBLANES, (tap + off) // SUBLANES * SUBLANES
        rows = win_ref[s, r + a:r + a + CONV_ROWS, :].reshape(acc.shape)
        acc = acc + rows * w_ref[tap][None]
    acc = acc.reshape(CONV_ROWS, BRANCH_WIDTH) + b_ref[...]
    mu = jnp.mean(acc, axis=-1, keepdims=True)
    d = acc - mu
    var = jnp.mean(d * d, axis=-1, keepdims=True)
    y = d * lax.rsqrt(var + NORM_EPS) * lg_ref[...] + lb_ref[...]
    return (y * jax.nn.sigmoid(y)).astype(BF16)


def _na_layout(rows):
    R = NA_QROWS
    KR = R + NA_WIN_ROWS
    nblk = rows // R
    qc = np.arange(GRID_W)
    ws = np.clip(qc - NA_WIN_COLS // 2, 0, GRID_W - NA_WIN_COLS)
    col_ok = (qc[None, :] >= ws[:, None]) & (qc[None, :] < ws[:, None] + NA_WIN_COLS)
    dc = qc[None, :] - qc[:, None] + NA_WIN_COLS - 1
    n_dr = 2 * NA_WIN_ROWS - 1
    variants, var_of_block, kr0s = [], [], []
    for j in range(nblk):
        r0 = j * R
        kr0 = int(np.clip(r0 - NA_WIN_ROWS // 2, 0, rows - KR))
        r = r0 + np.arange(R)
        start = np.clip(r - NA_WIN_ROWS // 2, 0, rows - NA_WIN_ROWS)
        kr = kr0 + np.arange(KR)
        row_ok = (kr[None, :] >= start[:, None]) & (kr[None, :] < start[:, None] + NA_WIN_ROWS)
        tile = np.where(row_ok, kr[None, :] - r[:, None] + NA_WIN_ROWS - 1, n_dr)
        key = tile.tobytes()
        if key not in [v[0] for v in variants]:
            variants.append((key, tile))
        var_of_block.append([v[0] for v in variants].index(key))
        kr0s.append(kr0)
    tiles = np.stack([v[1] for v in variants])
    onehot = (dc[None] == np.arange(2 * NA_WIN_COLS - 1)[:, None, None]) & col_ok[None]
    return tiles, onehot, col_ok, np.asarray(var_of_block, np.int32), np.asarray(kr0s, np.int32)


def _na_bias_table(rpb, rows):
    tiles, onehot, col_ok, _, _ = _na_layout(rows)
    H, n_dr, n_dc = rpb.shape
    oh = jnp.asarray(onehot.reshape(n_dc, GRID_W * GRID_W), F32)
    tz = jnp.dot(rpb.astype(F32).reshape(H * n_dr, n_dc), oh, precision=lax.Precision.HIGHEST)
    tz = jnp.where(col_ok[None, None], tz.reshape(H, n_dr, GRID_W, GRID_W), NEG_INF)
    tz = jnp.concatenate([tz, jnp.full((H, 1, GRID_W, GRID_W), NEG_INF, F32)], axis=1)
    nv, R, KR = tiles.shape
    per_variant = []
    for vv in range(nv):
        slabs = [jnp.concatenate([tz[:, int(tiles[vv, rl, kl])] for kl in range(KR)], axis=-1) for rl in range(R)]
        per_variant.append(jnp.concatenate(slabs, axis=1))
    return jnp.stack(per_variant)


def _na_bias_tables(rpb, row_counts):
    by_layout, out = {}, {}
    for rows in row_counts:
        key = _na_layout(rows)[0].tobytes()
        if key not in by_layout:
            by_layout[key] = _na_bias_table(rpb, rows)
        out[rows] = by_layout[key]
    return out


def _na_kernel(var_ref, kr0_ref, q_ref, k_ref, v_ref, *rest):
    tab_refs, o_ref = rest[:-1], rest[-1]
    step = pl.program_id(1)
    nq = q_ref.shape[1] // NA_BLOCKS_PER_STEP
    nk = tab_refs[0].shape[3]
    low = lax.broadcasted_iota(jnp.int32, (nq, LANES), 1) < HEAD_DIM
    units = [(sb, pair) for sb in range(len(tab_refs)) for pair in range(BRANCH_WIDTH // LANES)]
    kstart = [pl.multiple_of(kr0_ref[step * NA_BLOCKS_PER_STEP + sb] * GRID_W, GRID_W) for sb in range(len(tab_refs))]
    scores = []
    for sb, pair in units:
        cols = slice(pair * LANES, (pair + 1) * LANES)
        q2 = q_ref[0, sb * nq:(sb + 1) * nq, cols]
        zero = jnp.zeros_like(q2)
        qs = jnp.concatenate([jnp.where(low, q2, zero), jnp.where(low, zero, q2)], axis=0)
        scores.append(_dot_nt(qs, k_ref[0, pl.ds(kstart[sb], nk), cols]) + tab_refs[sb][0, pair])
    weights = []
    for s in scores:
        e = jnp.exp(s - jnp.max(s, axis=-1, keepdims=True))
        weights.append((e.astype(BF16), 1.0 / jnp.sum(e, axis=-1, keepdims=True)))
    for (sb, pair), (e, inv) in zip(units, weights):
        cols = slice(pair * LANES, (pair + 1) * LANES)
        o2 = _dot(e, v_ref[0, pl.ds(kstart[sb], nk), cols]) * inv
        o_ref[0, sb * nq:(sb + 1) * nq, cols] = jnp.where(low, o2[:nq], o2[nq:]).astype(BF16)


def _na_call(q, k, v, tab):
    B, T, _ = q.shape
    rows = T // GRID_W
    _, _, _, var_of_block, kr0s = _na_layout(rows)
    nq = NA_QROWS * GRID_W
    per_step = NA_BLOCKS_PER_STEP
    nblk = rows // NA_QROWS
    assert nblk % per_step == 0
    nv, heads, _, nk = tab.shape
    tab = tab.reshape(nv, heads // 2, 2 * nq, nk)
    tab_spec = lambda sb: pl.BlockSpec((1, heads // 2, 2 * nq, nk),
                                       lambda b, j, var, kr0: (var[j * per_step + sb], 0, 0, 0))
    grid_spec = pltpu.PrefetchScalarGridSpec(
        num_scalar_prefetch=2,
        grid=(B, nblk // per_step),
        in_specs=[
            pl.BlockSpec((1, per_step * nq, BRANCH_WIDTH), lambda b, j, var, kr0: (b, j, 0)),
            pl.BlockSpec((1, T, BRANCH_WIDTH), lambda b, j, var, kr0: (b, 0, 0)),
            pl.BlockSpec((1, T, BRANCH_WIDTH), lambda b, j, var, kr0: (b, 0, 0)),
        ] + [tab_spec(sb) for sb in range(per_step)],
        out_specs=pl.BlockSpec((1, per_step * nq, BRANCH_WIDTH), lambda b, j, var, kr0: (b, j, 0)),
    )
    return pl.pallas_call(
        _na_kernel,
        grid_spec=grid_spec,
        out_shape=jax.ShapeDtypeStruct((B, T, BRANCH_WIDTH), BF16),
        compiler_params=_params(("arbitrary", "arbitrary")),
        name="na",
    )(jnp.asarray(var_of_block), jnp.asarray(kr0s), q, k, v, *([tab] * per_step))


def _t5_bucket(rel):
    half = T5_BUCKETS // 2
    exact = half // 2
    n = np.abs(rel)
    large = exact + (np.log(np.maximum(n, 1) / exact) / math.log(T5_MAX_DIST / exact) * (half - exact)).astype(np.int64)
    large = np.minimum(large, half - 1)
    return (rel > 0).astype(np.int64) * half + np.where(n < exact, n, large)


def _swa_bias_table(t5_bias):
    kpos = np.arange(3 * SWA_BLOCK) - SWA_BLOCK
    rel = kpos[None, :] - np.arange(SWA_BLOCK)[:, None]
    onehot = (_t5_bucket(rel)[None] == np.arange(T5_BUCKETS)[:, None, None])
    oh = jnp.asarray(onehot.reshape(T5_BUCKETS, -1), F32)
    tab = jnp.dot(t5_bias.astype(F32).T, oh, precision=lax.Precision.HIGHEST).reshape((-1,) + rel.shape)
    tab = jnp.where((np.abs(rel) <= SWA_WINDOW)[None], tab, NEG_INF)
    before = (kpos < 0)[None, None, :]
    after = (kpos >= SWA_BLOCK)[None, None, :]
    tabs = jnp.stack([jnp.where(before, NEG_INF, tab), tab, jnp.where(after, NEG_INF, tab)])
    hq = tab.shape[0]
    return tabs.reshape(3, hq // 2, 2 * SWA_BLOCK, 3 * SWA_BLOCK)


def _swa_kernel(sink_ref, q_ref, k_ref, v_ref, tab_ref, o_ref):
    n = pl.program_id(1)
    tq = q_ref.shape[1]
    T = k_ref.shape[1]
    nb = T // SWA_BLOCK
    per_step = tq // SWA_BLOCK
    low = lax.broadcasted_iota(jnp.int32, (SWA_BLOCK, LANES), 1) < HEAD_DIM
    first_rows = lax.broadcasted_iota(jnp.int32, (2 * SWA_BLOCK, 1), 0) < SWA_BLOCK
    units = [(i, hk) for i in range(per_step) for hk in range(2)]

    def band(ref, blk):
        prev_start = pl.multiple_of(jnp.maximum(blk - 1, 0) * SWA_BLOCK, SWA_BLOCK)
        cur_start = pl.multiple_of(blk * SWA_BLOCK, SWA_BLOCK)
        next_start = pl.multiple_of(jnp.minimum(blk + 1, nb - 1) * SWA_BLOCK, SWA_BLOCK)
        return jnp.concatenate([ref[0, pl.ds(prev_start, SWA_BLOCK), :],
                                ref[0, pl.ds(cur_start, SWA_BLOCK), :],
                                ref[0, pl.ds(next_start, SWA_BLOCK), :]], axis=0)

    scores = []
    for i, hk in units:
        blk = n * per_step + i
        variant = jnp.where(blk == 0, 0, jnp.where(blk == nb - 1, 2, 1))
        q2 = q_ref[0, i * SWA_BLOCK:(i + 1) * SWA_BLOCK, hk * LANES:(hk + 1) * LANES]
        q2r = pltpu.roll(q2, HEAD_DIM, 1)
        zero = jnp.zeros_like(q2)
        if hk == 0:
            qs = jnp.concatenate([jnp.where(low, q2, zero), jnp.where(low, q2r, zero)], axis=0)
        else:
            qs = jnp.concatenate([jnp.where(low, zero, q2r), jnp.where(low, zero, q2)], axis=0)
        scores.append(_dot_nt(qs, band(k_ref, blk)) + tab_ref[variant, hk])
    weights = []
    for (i, hk), s in zip(units, scores):
        sink = jnp.where(first_rows, sink_ref[2 * hk], sink_ref[2 * hk + 1])
        m = jnp.maximum(jnp.max(s, axis=-1, keepdims=True), sink)
        e = jnp.exp(s - m)
        denom = jnp.sum(e, axis=-1, keepdims=True) + jnp.exp(sink - m)
        weights.append((e.astype(BF16), 1.0 / denom))
    for (i, hk), (e, inv) in zip(units, weights):
        o2 = _dot(e, band(v_ref, n * per_step + i)) * inv
        top, bot = o2[:SWA_BLOCK], o2[SWA_BLOCK:]
        if hk == 0:
            out = jnp.where(low, top, pltpu.roll(bot, HEAD_DIM, 1))
        else:
            out = jnp.where(low, pltpu.roll(top, HEAD_DIM, 1), bot)
        o_ref[0, i * SWA_BLOCK:(i + 1) * SWA_BLOCK, hk * LANES:(hk + 1) * LANES] = out.astype(BF16)


def _swa_call(q, k, v, tab, sink):
    B, T, _ = q.shape
    tq = TQ_SWA
    assert T // SWA_BLOCK >= 2
    return pl.pallas_call(
        _swa_kernel,
        grid=(B, T // tq),
        in_specs=[
            pl.BlockSpec(memory_space=pltpu.SMEM),
            pl.BlockSpec((1, tq, BRANCH_WIDTH), lambda b, i: (b, i, 0)),
            pl.BlockSpec((1, T, LANES), lambda b, i: (b, 0, 0)),
            pl.BlockSpec((1, T, LANES), lambda b, i: (b, 0, 0)),
            _resident(tab.shape),
        ],
        out_specs=pl.BlockSpec((1, tq, BRANCH_WIDTH), lambda b, i: (b, i, 0)),
        out_shape=jax.ShapeDtypeStruct((B, T, BRANCH_WIDTH), BF16),
        compiler_params=_params(("arbitrary", "arbitrary")),
        name="swa",
    )(sink.astype(F32), q, k, v, tab)


def _merge_kernel(x_ref, mod_ref, gain_ref, ret_ref, u_ref, na_ref, swa_ref, cw_ref, cb_ref, clg_ref, clb_ref,
                  wm_ref, wb_ref, wo_ref, o_ref, win_ref, conv_ref):
    tm = x_ref.shape[1]
    x = x_ref[0]
    h = _modulated_rmsnorm(x, gain_ref[...], mod_ref[0, 1:2, :], mod_ref[0, 0:1, :]).astype(BF16)
    _conv_fill_window(u_ref, win_ref, pl.program_id(1), pl.num_programs(1), tm)
    others = ((0, ret_ref), (2, na_ref), (3, swa_ref))
    halves = (slice(0, D_MODEL // 2), slice(D_MODEL // 2, D_MODEL))
    slots = [(i, br, cols) for cols in halves for i, br in others]
    groups = list(range(0, tm, CONV_ROWS))
    share = -(-len(groups) // len(slots))
    merged = [None, None]
    for n, (i, br, cols) in enumerate(slots):
        gate = _dot(h, wm_ref[i, :, cols])
        proj = _dot(br[0], wb_ref[i, :, cols])
        for r in groups[n * share:(n + 1) * share]:
            conv_ref[r:r + CONV_ROWS, :] = _conv_rows(win_ref, cw_ref, cb_ref, clg_ref, clb_ref, r)
        term = jax.nn.sigmoid(gate) * proj
        side = halves.index(cols)
        merged[side] = term if merged[side] is None else merged[side] + term
    conv = conv_ref[...]
    for side, cols in enumerate(halves):
        merged[side] = merged[side] + jax.nn.sigmoid(_dot(h, wm_ref[1, :, cols])) * _dot(conv, wb_ref[1, :, cols])
    merged = jnp.concatenate(merged, axis=1).astype(BF16)
    o_ref[0] = x + mod_ref[0, 2:3, :] * _dot(merged, wo_ref[...])


def _merge_call(x, mod, gain, ret, u, na, swa, conv_params, w_merge, w_branch, w_out):
    B, T, _ = x.shape
    tm = TM_MERGE
    tok = lambda w: pl.BlockSpec((1, tm, w), lambda b, i: (b, i, 0))
    whole = pl.BlockSpec(memory_space=pltpu.VMEM)
    dw_kernel, dw_bias, ln_gain, ln_bias = conv_params
    row = lambda a: a.reshape(1, BRANCH_WIDTH).astype(F32)
    taps = jnp.broadcast_to(dw_kernel.astype(F32)[:, None, :], (CONV_WIDTH, SUBLANES, BRANCH_WIDTH))
    return pl.pallas_call(
        _merge_kernel,
        grid=(B, T // tm),
        in_specs=[tok(D_MODEL), pl.BlockSpec((1, 6, D_MODEL), lambda b, i: (b, 0, 0)), _resident((1, D_MODEL)),
                  tok(BRANCH_WIDTH), pl.BlockSpec((1, T, BRANCH_WIDTH), lambda b, i: (b, 0, 0)),
                  tok(BRANCH_WIDTH), tok(BRANCH_WIDTH),
                  _resident((CONV_WIDTH, SUBLANES, BRANCH_WIDTH)),
                  _resident((1, BRANCH_WIDTH)), _resident((1, BRANCH_WIDTH)), _resident((1, BRANCH_WIDTH)),
                  whole, whole, whole],
        out_specs=tok(D_MODEL),
        out_shape=jax.ShapeDtypeStruct((B, T, D_MODEL), F32),
        scratch_shapes=[pltpu.VMEM((SUBLANES, tm + 2 * CONV_HALO - SUBLANES, BRANCH_WIDTH), F32),
                        pltpu.VMEM((tm, BRANCH_WIDTH), BF16)],
        compiler_params=_params(("arbitrary", "arbitrary")),
        name="merge",
    )(x, mod, gain, ret, u, na, swa, taps, row(dw_bias), row(ln_gain), row(ln_bias), w_merge, w_branch, w_out)


def _ffn_kernel(x_ref, mod_ref, gain_ref, fin_ref, w1_ref, w2_ref, o_ref, *, final_norm):
    x = x_ref[0]
    h = _modulated_rmsnorm(x, gain_ref[...], mod_ref[0, 4:5, :], mod_ref[0, 3:4, :]).astype(BF16)
    a = jnp.maximum(_dot(h, w1_ref[...]), 0.0)
    y = x + mod_ref[0, 5:6, :] * _dot((a * a).astype(BF16), w2_ref[...])
    if final_norm:
        y = y * lax.rsqrt(jnp.mean(y * y, axis=-1, keepdims=True) + NORM_EPS) * fin_ref[...]
    o_ref[0] = y


def _ffn_call(x, mod, gain, final_gain, w1, w2, final_norm):
    B, T, _ = x.shape
    tm = TM_FFN
    tok = pl.BlockSpec((1, tm, D_MODEL), lambda b, i: (b, i, 0))
    whole = pl.BlockSpec(memory_space=pltpu.VMEM)
    return pl.pallas_call(
        functools.partial(_ffn_kernel, final_norm=final_norm),
        grid=(B, T // tm),
        in_specs=[tok, pl.BlockSpec((1, 6, D_MODEL), lambda b, i: (b, 0, 0)), _resident((1, D_MODEL)),
                  _resident((1, D_MODEL)), whole, whole],
        out_specs=tok,
        out_shape=jax.ShapeDtypeStruct((B, T, D_MODEL), F32),
        compiler_params=_params(("arbitrary", "arbitrary")),
        name="ffn",
    )(x, mod, gain, final_gain, w1, w2)


def _rotary_tables(T):
    half = HEAD_DIM // 2
    inv = (ROPE_BASE ** (-np.arange(half, dtype=np.float32) / half)).astype(np.float32)
    ang = (np.arange(T, dtype=np.float32)[:, None] * inv[None, :]).astype(np.float32).astype(np.float64)
    lane = np.arange(LANES) % HEAD_DIM
    cos = np.cos(ang)[:, lane % half]
    sin = np.sin(ang)[:, lane % half] * np.where(lane < half, -1.0, 1.0)[None, :]
    return jnp.asarray(cos, F32), jnp.asarray(sin, F32)


def _trunk(x, mods, layer_params, shared):
    B, T, _ = x.shape
    cos_t, sin_t = _rotary_tables(T)
    rows = T // GRID_W
    for l, lp in enumerate(layer_params):
        mod = mods[l]
        ret_in, u, naq, nak, nav, sq, sk, sv = _proj_call(x, mod, lp["gain1"], lp["w_in"], cos_t, sin_t)
        ret = _ret_call(ret_in, lp["ret_tables"], lp["ret_gn_gain"])
        na = _na_call(naq, nak, nav, lp["na_tables"][rows])
        swa = _swa_call(sq, sk, sv, shared["swa_table"], lp["swa_sink"])
        conv_params = (lp["conv_dw_kernel"], lp["conv_dw_bias"], lp["conv_ln_gain"], lp["conv_ln_bias"])
        x = _merge_call(x, mod, lp["gain1"], ret, u, na, swa, conv_params, lp["w_merge"], lp["w_branch"], lp["w_out"])
        x = _ffn_call(x, mod, lp["gain2"], shared["final_gain"], lp["w_ff1"], lp["w_ff2"], l == len(layer_params) - 1)
    return x


def kernel(x_prompt, x_sample, c_prompt, c_sample, w_ada, b_ada, norm_gain, w_in, ret_decay_logit, ret_gn_gain,
           conv_dw_kernel, conv_dw_bias, conv_ln_gain, conv_ln_bias, na_rpb, swa_sink, t5_bias,
           w_branch, w_merge, w_out, w_ff1, w_ff2, final_gain):
    nbp = c_prompt.shape[0]
    c_all = jnp.concatenate([c_prompt, c_sample], axis=0)
    mods = _ada_call(c_all, w_ada.astype(BF16), b_ada.astype(F32))
    mods = mods.reshape(DEPTH, c_all.shape[0], 6, D_MODEL)
    row_counts = sorted({x_prompt.shape[1] // GRID_W, x_sample.shape[1] // GRID_W})
    layer_params = []
    for l in range(DEPTH):
        layer_params.append(dict(
            gain1=norm_gain[l, 0].reshape(1, D_MODEL).astype(F32),
            gain2=norm_gain[l, 1].reshape(1, D_MODEL).astype(F32),
            w_in=w_in[l].astype(BF16),
            ret_tables=_ret_tables(ret_decay_logit[l]),
            ret_gn_gain=ret_gn_gain[l],
            conv_dw_kernel=conv_dw_kernel[l], conv_dw_bias=conv_dw_bias[l],
            conv_ln_gain=conv_ln_gain[l], conv_ln_bias=conv_ln_bias[l],
            na_tables=_na_bias_tables(na_rpb[l], row_counts),
            swa_sink=swa_sink[l],
            w_merge=w_merge[l].astype(BF16), w_branch=w_branch[l].astype(BF16), w_out=w_out[l].astype(BF16),
            w_ff1=w_ff1[l].astype(BF16), w_ff2=w_ff2[l].astype(BF16),
        ))
    shared = dict(swa_table=_swa_bias_table(t5_bias), final_gain=final_gain.reshape(1, D_MODEL).astype(F32))
    y_prompt = _trunk(x_prompt, [m[:nbp] for m in mods], layer_params, shared)
    y_sample = _trunk(x_sample, [m[nbp:] for m in mods], layer_params, shared)
    return (y_prompt, y_sample)
```

```python
import functools
import math

import jax
import jax.numpy as jnp
import numpy as np
from jax import lax
from jax.experimental import pallas as pl
from jax.experimental.pallas import tpu as pltpu

F32 = jnp.float32
BF16 = jnp.bfloat16

D_MODEL = 1024
DEPTH = 2
HEAD_DIM = 64
BRANCH_WIDTH = 256
N_BRANCHES = 4
RET_CHUNK = 128
RET_CHUNKS_PER_STEP = 16
ROPE_BASE = 10000.0
CONV_WIDTH = 31
CONV_HALO = 16
GRID_W = 64
NA_WIN_ROWS = 8
NA_WIN_COLS = 16
NA_QROWS = 2
NA_BLOCKS_PER_STEP = 8
SWA_WINDOW = 128
SWA_BLOCK = 128
T5_BUCKETS = 32
T5_MAX_DIST = 128
D_FF = 4 * D_MODEL
NORM_EPS = 1e-6
NEG_INF = -1e30
SWA_KV_WIDTH = 128
RET_COL0 = 0
CONV_COL0 = RET_COL0 + 4 * BRANCH_WIDTH
NA_COL0 = CONV_COL0 + 2 * BRANCH_WIDTH
SWA_COL0 = NA_COL0 + 3 * BRANCH_WIDTH
IN_COLS = SWA_COL0 + BRANCH_WIDTH + 2 * SWA_KV_WIDTH
LANES = 128
SUBLANES = 8

TM_PROJ = 1024
PROJ_ROW_SPLITS = 4
TM_MERGE = 512
TM_FFN = 1024
FFN_CHUNK = 1024
CONV_ROWS = 32
TQ_SWA = 1024
VMEM_LIMIT = 56 * 1024 * 1024


def _params(sem, flags=None):
    return pltpu.CompilerParams(dimension_semantics=sem, vmem_limit_bytes=VMEM_LIMIT, flags=flags)


def _resident(shape):
    nd = len(shape)
    return pl.BlockSpec(shape, lambda *_: (0,) * nd)


def _modulated_rmsnorm(x, gain, scale, shift):
    y = x * lax.rsqrt(jnp.mean(x * x, axis=-1, keepdims=True) + NORM_EPS)
    return (y * gain) * (1.0 + scale) + shift


def _dot(a, b):
    return jnp.dot(a, b, preferred_element_type=F32)


def _dot_nt(a, b):
    return lax.dot_general(a, b, (((1,), (1,)), ((), ())), preferred_element_type=F32)


def _dot_tn(a, b):
    return lax.dot_general(a, b, (((0,), (0,)), ((), ())), preferred_element_type=F32)


def _dot_split(x, w):
    hi = x.astype(BF16)
    lo = (x - hi.astype(F32)).astype(BF16)
    return _dot(hi, w) + _dot(lo, w)


def _ada_kernel(c_ref, w_ref, b_ref, o_ref):
    c = c_ref[...]
    a = (c * jax.nn.sigmoid(c)).astype(BF16)
    o_ref[0] = _dot(a, w_ref[0].astype(BF16)) + b_ref[0]


def _ada_call(c_all, w_ada, b_ada):
    nb = c_all.shape[0]
    return pl.pallas_call(
        _ada_kernel,
        grid=(DEPTH, 6),
        in_specs=[
            pl.BlockSpec((nb, D_MODEL), lambda l, j: (0, 0)),
            pl.BlockSpec((1, D_MODEL, D_MODEL), lambda l, j: (l, 0, j)),
            pl.BlockSpec((1, 1, D_MODEL), lambda l, j: (l, 0, j)),
        ],
        out_specs=pl.BlockSpec((1, nb, D_MODEL), lambda l, j: (l, 0, j)),
        out_shape=jax.ShapeDtypeStruct((DEPTH, nb, 6 * D_MODEL), F32),
        compiler_params=_params(("arbitrary", "arbitrary")),
        name="ada",
    )(c_all, w_ada, b_ada.reshape(DEPTH, 1, 6 * D_MODEL))


def _proj_kernel(x_ref, mod_ref, gain_ref, w_ref, cos_ref, sin_ref,
                 ret_ref, u_ref, naq_ref, nak_ref, nav_ref, sq_ref, sk_ref, sv_ref):
    tm = x_ref.shape[1]
    W = BRANCH_WIDTH
    half = tm // PROJ_ROW_SPLITS
    lane = lax.broadcasted_iota(jnp.int32, (half, LANES), 1)
    first_half = (lane & (HEAD_DIM // 2)) == 0
    qscale = HEAD_DIM ** -0.5
    blocks = [slice(n * half, (n + 1) * half) for n in range(PROJ_ROW_SPLITS)]
    hs = [_modulated_rmsnorm(x_ref[0, rows, :], gain_ref[...], mod_ref[0, 1:2, :], mod_ref[0, 0:1, :]).astype(BF16)
          for rows in blocks]
    for rows, h in zip(blocks, hs):
        cos = cos_ref[rows, :]
        sin = sin_ref[rows, :]

        def rot(y):
            sw = jnp.where(first_half, pltpu.roll(y, LANES - HEAD_DIM // 2, 1), pltpu.roll(y, HEAD_DIM // 2, 1))
            return y * cos + sw * sin

        y = _dot(h, w_ref[:, RET_COL0:RET_COL0 + 4 * W])
        for t in range(W // LANES):
            ret_ref[0, rows, t * LANES:(t + 1) * LANES] = (rot(y[:, t * LANES:(t + 1) * LANES]) * qscale).astype(BF16)
        for t in range(W // LANES, 2 * W // LANES):
            ret_ref[0, rows, t * LANES:(t + 1) * LANES] = rot(y[:, t * LANES:(t + 1) * LANES]).astype(BF16)
        ret_ref[0, rows, 2 * W:4 * W] = y[:, 2 * W:4 * W].astype(BF16)
        y = _dot(h, w_ref[:, CONV_COL0:CONV_COL0 + 2 * W])
        u_ref[0, rows, :] = (y[:, 0:W] * jax.nn.sigmoid(y[:, W:2 * W])).astype(BF16)
        y = _dot(h, w_ref[:, NA_COL0:NA_COL0 + 3 * W])
        naq_ref[0, rows, :] = (y[:, 0:W] * qscale).astype(BF16)
        nak_ref[0, rows, :] = y[:, W:2 * W].astype(BF16)
        nav_ref[0, rows, :] = y[:, 2 * W:3 * W].astype(BF16)
        y = _dot(h, w_ref[:, SWA_COL0:IN_COLS])
        sq_ref[0, rows, :] = (y[:, 0:W] * qscale).astype(BF16)
        sk_ref[0, rows, :] = y[:, W:W + SWA_KV_WIDTH].astype(BF16)
        sv_ref[0, rows, :] = y[:, W + SWA_KV_WIDTH:W + 2 * SWA_KV_WIDTH].astype(BF16)


def _proj_call(x, mod, gain, w_in, cos_t, sin_t):
    B, T, _ = x.shape
    tm = TM_PROJ
    tok = lambda w: pl.BlockSpec((1, tm, w), lambda b, i: (b, i, 0))
    W = BRANCH_WIDTH
    widths = (4 * W, W, W, W, W, W, SWA_KV_WIDTH, SWA_KV_WIDTH)
    return pl.pallas_call(
        _proj_kernel,
        grid=(B, T // tm),
        in_specs=[
            tok(D_MODEL),
            pl.BlockSpec((1, 6, D_MODEL), lambda b, i: (b, 0, 0)),
            _resident((1, D_MODEL)),
            _resident((D_MODEL, IN_COLS)),
            pl.BlockSpec((tm, LANES), lambda b, i: (i, 0)),
            pl.BlockSpec((tm, LANES), lambda b, i: (i, 0)),
        ],
        out_specs=[tok(w) for w in widths],
        out_shape=[jax.ShapeDtypeStruct((B, T, w), BF16) for w in widths],
        compiler_params=_params(("arbitrary", "arbitrary")),
        name="proj",
    )(x, mod, gain, w_in, cos_t, sin_t)


def _ret_kernel(x_ref, dec_ref, xif_ref, xib_ref, zf_ref, zb_ref, gf_ref, gb_ref, bd_ref, avg_ref, gn_ref,
                o_ref, rf_ref, rb_ref, stash_ref):
    p = pl.program_id(1)
    n = pl.program_id(2)
    nsteps = pl.num_programs(2)
    C = RET_CHUNK
    W = BRANCH_WIDTH
    G = x_ref.shape[1] // C
    heads = W // HEAD_DIM

    def chunk_kv(j):
        rows = slice(j * C, (j + 1) * C)
        return x_ref[0, rows, W:2 * W], x_ref[0, rows, 2 * W:3 * W]

    def chunk_state(k, v, zeta):
        return _dot_tn((k.astype(F32) * zeta).astype(BF16), v) * bd_ref[...]

    @pl.when(p == 0)
    def _forward_states():
        @pl.when(n == 0)
        def _():
            rf_ref[...] = jnp.zeros_like(rf_ref)

        updates = [chunk_state(*chunk_kv(j), zf_ref[...]) for j in range(G)]
        rf = rf_ref[...]
        for j in range(G):
            stash_ref[n * G + j] = rf.astype(BF16)
            rf = gf_ref[...] * rf + updates[j]
        rf_ref[...] = rf

    @pl.when(p == 1)
    def _outputs():
        @pl.when(n == 0)
        def _():
            rb_ref[...] = jnp.zeros_like(rb_ref)

        first_chunk = (nsteps - 1 - n) * G
        lane_head = lax.broadcasted_iota(jnp.int32, (C, W), 1) // HEAD_DIM

        def head_rows(a):
            return jnp.concatenate([jnp.where(lane_head == hh, a, jnp.zeros_like(a)) for hh in range(heads)], axis=0)

        qs = [x_ref[0, j * C:(j + 1) * C, 0:W] for j in range(G)]
        weights = [(_dot_nt(qs[j], head_rows(chunk_kv(j)[0])) * dec_ref[...]).astype(BF16) for j in range(G)]
        updates = [chunk_state(*chunk_kv(j), zb_ref[...]) for j in range(G)]
        rb = rb_ref[...]
        later = [None] * G
        for j in reversed(range(G)):
            later[j] = rb.astype(BF16)
            rb = gb_ref[...] * rb + updates[j]
        rb_ref[...] = rb
        outs = []
        for j in range(G):
            qf = qs[j].astype(F32)
            o = _dot(weights[j], head_rows(chunk_kv(j)[1]))
            o = o + _dot((qf * xif_ref[...]).astype(BF16), stash_ref[first_chunk + j])
            outs.append(o + _dot((qf * xib_ref[...]).astype(BF16), later[j]))
        o = jnp.concatenate(outs, axis=0)
        avg = avg_ref[...]
        d = o - _dot_split(o, avg)
        var = _dot_split(d * d, avg)
        g = x_ref[0, :, 3 * W:4 * W].astype(F32)
        o_ref[0] = (d * lax.rsqrt(var + NORM_EPS) * gn_ref[...] * (g * jax.nn.sigmoid(g))).astype(BF16)


def _ret_tables(decay_logit):
    C = RET_CHUNK
    heads = BRANCH_WIDTH // HEAD_DIM
    lg = jax.nn.log_sigmoid(decay_logit.astype(F32))
    pos = np.arange(C, dtype=np.float32)
    diff = pos[:, None] - pos[None, :]
    fwd = jnp.exp(jnp.where(diff >= 0, diff[None] * lg[0][:, None, None], -jnp.inf))
    bwd = jnp.exp(jnp.where(diff < 0, -diff[None] * lg[1][:, None, None], -jnp.inf))
    dec = (fwd + bwd).transpose(1, 0, 2).reshape(C, heads * C)
    lane_lg = jnp.repeat(lg, HEAD_DIM, axis=1)
    xif = jnp.exp((pos + 1.0)[:, None] * lane_lg[0][None, :])
    xib = jnp.exp((C - pos)[:, None] * lane_lg[1][None, :])
    zf = jnp.exp((C - 1.0 - pos)[:, None] * lane_lg[0][None, :])
    zb = jnp.exp(pos[:, None] * lane_lg[1][None, :])
    gf = jnp.exp(C * lane_lg[0])[None, :]
    gb = jnp.exp(C * lane_lg[1])[None, :]
    return dec, xif, xib, zf, zb, gf, gb


def _head_block_constants():
    hid = np.arange(BRANCH_WIDTH) // HEAD_DIM
    same = (hid[:, None] == hid[None, :])
    return jnp.asarray(same.astype(np.float32)), jnp.asarray(same.astype(np.float32) / HEAD_DIM, dtype=BF16)


def _ret_call(ret_in, tables, gn_gain):
    B, T, _ = ret_in.shape
    rows = RET_CHUNK * RET_CHUNKS_PER_STEP
    N = T // RET_CHUNK
    S = T // rows
    bd, avg = _head_block_constants()
    step_rows = lambda b, p, n: (b, jnp.where(p == 0, n, S - 1 - n), 0)
    out_rows = lambda b, p, n: (b, jnp.where(p == 0, S - 1, S - 1 - n), 0)
    consts = list(tables) + [bd, avg, gn_gain.reshape(1, BRANCH_WIDTH).astype(F32)]
    return pl.pallas_call(
        _ret_kernel,
        grid=(B, 2, S),
        in_specs=[pl.BlockSpec((1, rows, 4 * BRANCH_WIDTH), step_rows)] + [_resident(t.shape) for t in consts],
        out_specs=pl.BlockSpec((1, rows, BRANCH_WIDTH), out_rows),
        out_shape=jax.ShapeDtypeStruct((B, T, BRANCH_WIDTH), BF16),
        scratch_shapes=[
            pltpu.VMEM((BRANCH_WIDTH, BRANCH_WIDTH), F32),
            pltpu.VMEM((BRANCH_WIDTH, BRANCH_WIDTH), F32),
            pltpu.VMEM((N, BRANCH_WIDTH, BRANCH_WIDTH), BF16),
        ],
        compiler_params=_params(("arbitrary", "arbitrary", "arbitrary")),
        name="ret",
    )(ret_in, *consts)


def _conv_fill_window(u_ref, win_ref, i, nt, tt):
    T = u_ref.shape[1]
    t0 = pl.multiple_of(i * tt, tt)
    left_start = pl.multiple_of(jnp.maximum(t0 - CONV_HALO, 0), CONV_HALO)
    right_start = pl.multiple_of(jnp.minimum(t0 + tt, T - CONV_HALO), CONV_HALO)
    left = u_ref[0, pl.ds(left_start, CONV_HALO), :].astype(F32) * (i > 0).astype(F32)
    right = u_ref[0, pl.ds(right_start, CONV_HALO), :].astype(F32) * (i < nt - 1).astype(F32)
    win = jnp.concatenate([left, u_ref[0, pl.ds(t0, tt), :].astype(F32), right], axis=0)
    span = win_ref.shape[1]
    for s in range(SUBLANES):
        win_ref[s] = win[s:s + span]


def _conv_rows(win_ref, w_ref, b_ref, lg_ref, lb_ref, r):
    off = CONV_HALO - CONV_WIDTH // 2
    acc = jnp.zeros((CONV_ROWS // SUBLANES, SUBLANES, BRANCH_WIDTH), F32)
    for tap in range(CONV_WIDTH):
        s, a = (tap + off) % SUBLANES, (tap + off) // SUBLANES * SUBLANES
        rows = win_ref[s, r + a:r + a + CONV_ROWS, :].reshape(acc.shape)
        acc = acc + rows * w_ref[tap][None]
    acc = acc.reshape(CONV_ROWS, BRANCH_WIDTH) + b_ref[...]
    mu = jnp.mean(acc, axis=-1, keepdims=True)
    d = acc - mu
    var = jnp.mean(d * d, axis=-1, keepdims=True)
    y = d * lax.rsqrt(var + NORM_EPS) * lg_ref[...] + lb_ref[...]
    return (y * jax.nn.sigmoid(y)).astype(BF16)


def _na_layout(rows):
    R = NA_QROWS
    KR = R + NA_WIN_ROWS
    nblk = rows // R
    qc = np.arange(GRID_W)
    ws = np.clip(qc - NA_WIN_COLS // 2, 0, GRID_W - NA_WIN_COLS)
    col_ok = (qc[None, :] >= ws[:, None]) & (qc[None, :] < ws[:, None] + NA_WIN_COLS)
    dc = qc[None, :] - qc[:, None] + NA_WIN_COLS - 1
    n_dr = 2 * NA_WIN_ROWS - 1
    variants, var_of_block, kr0s = [], [], []
    for j in range(nblk):
        r0 = j * R
        kr0 = int(np.clip(r0 - NA_WIN_ROWS // 2, 0, rows - KR))
        r = r0 + np.arange(R)
        start = np.clip(r - NA_WIN_ROWS // 2, 0, rows - NA_WIN_ROWS)
        kr = kr0 + np.arange(KR)
        row_ok = (kr[None, :] >= start[:, None]) & (kr[None, :] < start[:, None] + NA_WIN_ROWS)
        tile = np.where(row_ok, kr[None, :] - r[:, None] + NA_WIN_ROWS - 1, n_dr)
        key = tile.tobytes()
        if key not in [v[0] for v in variants]:
            variants.append((key, tile))
        var_of_block.append([v[0] for v in variants].index(key))
        kr0s.append(kr0)
    tiles = np.stack([v[1] for v in variants])
    onehot = (dc[None] == np.arange(2 * NA_WIN_COLS - 1)[:, None, None]) & col_ok[None]
    return tiles, onehot, col_ok, np.asarray(var_of_block, np.int32), np.asarray(kr0s, np.int32)


def _na_bias_table(rpb, rows):
    tiles, onehot, col_ok, _, _ = _na_layout(rows)
    H, n_dr, n_dc = rpb.shape
    oh = jnp.asarray(onehot.reshape(n_dc, GRID_W * GRID_W), F32)
    tz = jnp.dot(rpb.astype(F32).reshape(H * n_dr, n_dc), oh, precision=lax.Precision.HIGHEST)
    tz = jnp.where(col_ok[None, None], tz.reshape(H, n_dr, GRID_W, GRID_W), NEG_INF)
    tz = jnp.concatenate([tz, jnp.full((H, 1, GRID_W, GRID_W), NEG_INF, F32)], axis=1)
    nv, R, KR = tiles.shape
    per_variant = []
    for vv in range(nv):
        slabs = [jnp.concatenate([tz[:, int(tiles[vv, rl, kl])] for kl in range(KR)], axis=-1) for rl in range(R)]
        per_variant.append(jnp.concatenate(slabs, axis=1))
    return jnp.stack(per_variant)


def _na_bias_tables(rpb, row_counts):
    by_layout, out = {}, {}
    for rows in row_counts:
        key = _na_layout(rows)[0].tobytes()
        if key not in by_layout:
            by_layout[key] = _na_bias_table(rpb, rows)
        out[rows] = by_layout[key]
    return out


def _na_kernel(var_ref, kr0_ref, q_ref, k_ref, v_ref, *rest):
    tab_refs, o_ref = rest[:-1], rest[-1]
    step = pl.program_id(1)
    nq = q_ref.shape[1] // NA_BLOCKS_PER_STEP
    nk = tab_refs[0].shape[3]
    low = lax.broadcasted_iota(jnp.int32, (nq, LANES), 1) < HEAD_DIM
    units = [(sb, pair) for sb in range(len(tab_refs)) for pair in range(BRANCH_WIDTH // LANES)]
    kstart = [pl.multiple_of(kr0_ref[step * NA_BLOCKS_PER_STEP + sb] * GRID_W, GRID_W) for sb in range(len(tab_refs))]
    scores = []
    for sb, pair in units:
        cols = slice(pair * LANES, (pair + 1) * LANES)
        q2 = q_ref[0, sb * nq:(sb + 1) * nq, cols]
        zero = jnp.zeros_like(q2)
        qs = jnp.concatenate([jnp.where(low, q2, zero), jnp.where(low, zero, q2)], axis=0)
        scores.append(_dot_nt(qs, k_ref[0, pl.ds(kstart[sb], nk), cols]) + tab_refs[sb][0, pair])
    weights = []
    for s in scores:
        e = jnp.exp(s - jnp.max(s, axis=-1, keepdims=True))
        weights.append((e.astype(BF16), 1.0 / jnp.sum(e, axis=-1, keepdims=True)))
    for (sb, pair), (e, inv) in zip(units, weights):
        cols = slice(pair * LANES, (pair + 1) * LANES)
        o2 = _dot(e, v_ref[0, pl.ds(kstart[sb], nk), cols]) * inv
        o_ref[0, sb * nq:(sb + 1) * nq, cols] = jnp.where(low, o2[:nq], o2[nq:]).astype(BF16)


def _na_call(q, k, v, tab):
    B, T, _ = q.shape
    rows = T // GRID_W
    _, _, _, var_of_block, kr0s = _na_layout(rows)
    nq = NA_QROWS * GRID_W
    per_step = NA_BLOCKS_PER_STEP
    nblk = rows // NA_QROWS
    assert nblk % per_step == 0
    nv, heads, _, nk = tab.shape
    tab = tab.reshape(nv, heads // 2, 2 * nq, nk)
    tab_spec = lambda sb: pl.BlockSpec((1, heads // 2, 2 * nq, nk),
                                       lambda b, j, var, kr0: (var[j * per_step + sb], 0, 0, 0))
    grid_spec = pltpu.PrefetchScalarGridSpec(
        num_scalar_prefetch=2,
        grid=(B, nblk // per_step),
        in_specs=[
            pl.BlockSpec((1, per_step * nq, BRANCH_WIDTH), lambda b, j, var, kr0: (b, j, 0)),
            pl.BlockSpec((1, T, BRANCH_WIDTH), lambda b, j, var, kr0: (b, 0, 0)),
            pl.BlockSpec((1, T, BRANCH_WIDTH), lambda b, j, var, kr0: (b, 0, 0)),
        ] + [tab_spec(sb) for sb in range(per_step)],
        out_specs=pl.BlockSpec((1, per_step * nq, BRANCH_WIDTH), lambda b, j, var, kr0: (b, j, 0)),
    )
    return pl.pallas_call(
        _na_kernel,
        grid_spec=grid_spec,
        out_shape=jax.ShapeDtypeStruct((B, T, BRANCH_WIDTH), BF16),
        compiler_params=_params(("arbitrary", "arbitrary")),
        name="na",
    )(jnp.asarray(var_of_block), jnp.asarray(kr0s), q, k, v, *([tab] * per_step))


def _t5_bucket(rel):
    half = T5_BUCKETS // 2
    exact = half // 2
    n = np.abs(rel)
    large = exact + (np.log(np.maximum(n, 1) / exact) / math.log(T5_MAX_DIST / exact) * (half - exact)).astype(np.int64)
    large = np.minimum(large, half - 1)
    return (rel > 0).astype(np.int64) * half + np.where(n < exact, n, large)


def _swa_bias_table(t5_bias):
    kpos = np.arange(3 * SWA_BLOCK) - SWA_BLOCK
    rel = kpos[None, :] - np.arange(SWA_BLOCK)[:, None]
    onehot = (_t5_bucket(rel)[None] == np.arange(T5_BUCKETS)[:, None, None])
    oh = jnp.asarray(onehot.reshape(T5_BUCKETS, -1), F32)
    tab = jnp.dot(t5_bias.astype(F32).T, oh, precision=lax.Precision.HIGHEST).reshape((-1,) + rel.shape)
    tab = jnp.where((np.abs(rel) <= SWA_WINDOW)[None], tab, NEG_INF)
    before = (kpos < 0)[None, None, :]
    after = (kpos >= SWA_BLOCK)[None, None, :]
    tabs = jnp.stack([jnp.where(before, NEG_INF, tab), tab, jnp.where(after, NEG_INF, tab)])
    hq = tab.shape[0]
    return tabs.reshape(3, hq // 2, 2 * SWA_BLOCK, 3 * SWA_BLOCK)


def _swa_kernel(sink_ref, q_ref, k_ref, v_ref, tab_ref, o_ref):
    n = pl.program_id(1)
    tq = q_ref.shape[1]
    T = k_ref.shape[1]
    nb = T // SWA_BLOCK
    per_step = tq // SWA_BLOCK
    low = lax.broadcasted_iota(jnp.int32, (SWA_BLOCK, LANES), 1) < HEAD_DIM
    first_rows = lax.broadcasted_iota(jnp.int32, (2 * SWA_BLOCK, 1), 0) < SWA_BLOCK
    units = [(i, hk) for i in range(per_step) for hk in range(2)]

    def band(ref, blk):
        prev_start = pl.multiple_of(jnp.maximum(blk - 1, 0) * SWA_BLOCK, SWA_BLOCK)
        cur_start = pl.multiple_of(blk * SWA_BLOCK, SWA_BLOCK)
        next_start = pl.multiple_of(jnp.minimum(blk + 1, nb - 1) * SWA_BLOCK, SWA_BLOCK)
        return jnp.concatenate([ref[0, pl.ds(prev_start, SWA_BLOCK), :],
                                ref[0, pl.ds(cur_start, SWA_BLOCK), :],
                                ref[0, pl.ds(next_start, SWA_BLOCK), :]], axis=0)

    scores = []
    for i, hk in units:
        blk = n * per_step + i
        variant = jnp.where(blk == 0, 0, jnp.where(blk == nb - 1, 2, 1))
        q2 = q_ref[0, i * SWA_BLOCK:(i + 1) * SWA_BLOCK, hk * LANES:(hk + 1) * LANES]
        q2r = pltpu.roll(q2, HEAD_DIM, 1)
        zero = jnp.zeros_like(q2)
        if hk == 0:
            qs = jnp.concatenate([jnp.where(low, q2, zero), jnp.where(low, q2r, zero)], axis=0)
        else:
            qs = jnp.concatenate([jnp.where(low, zero, q2r), jnp.where(low, zero, q2)], axis=0)
        scores.append(_dot_nt(qs, band(k_ref, blk)) + tab_ref[variant, hk])
    weights = []
    for (i, hk), s in zip(units, scores):
        sink = jnp.where(first_rows, sink_ref[2 * hk], sink_ref[2 * hk + 1])
        m = jnp.maximum(jnp.max(s, axis=-1, keepdims=True), sink)
        e = jnp.exp(s - m)
        denom = jnp.sum(e, axis=-1, keepdims=True) + jnp.exp(sink - m)
        weights.append((e.astype(BF16), 1.0 / denom))
    for (i, hk), (e, inv) in zip(units, weights):
        o2 = _dot(e, band(v_ref, n * per_step + i)) * inv
        top, bot = o2[:SWA_BLOCK], o2[SWA_BLOCK:]
        if hk == 0:
            out = jnp.where(low, top, pltpu.roll(bot, HEAD_DIM, 1))
        else:
            out = jnp.where(low, pltpu.roll(top, HEAD_DIM, 1), bot)
        o_ref[0, i * SWA_BLOCK:(i + 1) * SWA_BLOCK, hk * LANES:(hk + 1) * LANES] = out.astype(BF16)


def _swa_call(q, k, v, tab, sink):
    B, T, _ = q.shape
    tq = TQ_SWA
    assert T // SWA_BLOCK >= 2
    return pl.pallas_call(
        _swa_kernel,
        grid=(B, T // tq),
        in_specs=[
            pl.BlockSpec(memory_space=pltpu.SMEM),
            pl.BlockSpec((1, tq, BRANCH_WIDTH), lambda b, i: (b, i, 0)),
            pl.BlockSpec((1, T, LANES), lambda b, i: (b, 0, 0)),
            pl.BlockSpec((1, T, LANES), lambda b, i: (b, 0, 0)),
            _resident(tab.shape),
        ],
        out_specs=pl.BlockSpec((1, tq, BRANCH_WIDTH), lambda b, i: (b, i, 0)),
        out_shape=jax.ShapeDtypeStruct((B, T, BRANCH_WIDTH), BF16),
        compiler_params=_params(("arbitrary", "arbitrary")),
        name="swa",
    )(sink.astype(F32), q, k, v, tab)


def _merge_kernel(x_ref, mod_ref, gain_ref, ret_ref, u_ref, na_ref, swa_ref, cw_ref, cb_ref, clg_ref, clb_ref,
                  wm_ref, wb_ref, wo_ref, o_ref, win_ref, conv_ref):
    tm = x_ref.shape[1]
    x = x_ref[0]
    h = _modulated_rmsnorm(x, gain_ref[...], mod_ref[0, 1:2, :], mod_ref[0, 0:1, :]).astype(BF16)
    _conv_fill_window(u_ref, win_ref, pl.program_id(1), pl.num_programs(1), tm)
    others = ((0, ret_ref), (2, na_ref), (3, swa_ref))
    halves = (slice(0, D_MODEL // 2), slice(D_MODEL // 2, D_MODEL))
    slots = [(i, br, cols) for cols in halves for i, br in others]
    groups = list(range(0, tm, CONV_ROWS))
    share = -(-len(groups) // len(slots))
    merged = [None, None]
    for n, (i, br, cols) in enumerate(slots):
        gate = _dot(h, wm_ref[i, :, cols])
        proj = _dot(br[0], wb_ref[i, :, cols])
        for r in groups[n * share:(n + 1) * share]:
            conv_ref[r:r + CONV_ROWS, :] = _conv_rows(win_ref, cw_ref, cb_ref, clg_ref, clb_ref, r)
        term = jax.nn.sigmoid(gate) * proj
        side = halves.index(cols)
        merged[side] = term if merged[side] is None else merged[side] + term
    conv = conv_ref[...]
    for side, cols in enumerate(halves):
        merged[side] = merged[side] + jax.nn.sigmoid(_dot(h, wm_ref[1, :, cols])) * _dot(conv, wb_ref[1, :, cols])
    merged = jnp.concatenate(merged, axis=1).astype(BF16)
    o_ref[0] = x + mod_ref[0, 2:3, :] * _dot(merged, wo_ref[...])


def _merge_call(x, mod, gain, ret, u, na, swa, conv_params, w_merge, w_branch, w_out):
    B, T, _ = x.shape
    tm = TM_MERGE
    tok = lambda w: pl.BlockSpec((1, tm, w), lambda b, i: (b, i, 0))
    whole = pl.BlockSpec(memory_space=pltpu.VMEM)
    dw_kernel, dw_bias, ln_gain, ln_bias = conv_params
    row = lambda a: a.reshape(1, BRANCH_WIDTH).astype(F32)
    taps = jnp.broadcast_to(dw_kernel.astype(F32)[:, None, :], (CONV_WIDTH, SUBLANES, BRANCH_WIDTH))
    return pl.pallas_call(
        _merge_kernel,
        grid=(B, T // tm),
        in_specs=[tok(D_MODEL), pl.BlockSpec((1, 6, D_MODEL), lambda b, i: (b, 0, 0)), _resident((1, D_MODEL)),
                  tok(BRANCH_WIDTH), pl.BlockSpec((1, T, BRANCH_WIDTH), lambda b, i: (b, 0, 0)),
                  tok(BRANCH_WIDTH), tok(BRANCH_WIDTH),
                  _resident((CONV_WIDTH, SUBLANES, BRANCH_WIDTH)),
                  _resident((1, BRANCH_WIDTH)), _resident((1, BRANCH_WIDTH)), _resident((1, BRANCH_WIDTH)),
                  whole, whole, whole],
        out_specs=tok(D_MODEL),
        out_shape=jax.ShapeDtypeStruct((B, T, D_MODEL), F32),
        scratch_shapes=[pltpu.VMEM((SUBLANES, tm + 2 * CONV_HALO - SUBLANES, BRANCH_WIDTH), F32),
                        pltpu.VMEM((tm, BRANCH_WIDTH), BF16)],
        compiler_params=_params(("arbitrary", "arbitrary")),
        name="merge",
    )(x, mod, gain, ret, u, na, swa, taps, row(dw_bias), row(ln_gain), row(ln_bias), w_merge, w_branch, w_out)


def _ffn_kernel(x_ref, mod_ref, gain_ref, fin_ref, w1_ref, w2_ref, o_ref, *, final_norm):
    x = x_ref[0]
    h = _modulated_rmsnorm(x, gain_ref[...], mod_ref[0, 4:5, :], mod_ref[0, 3:4, :]).astype(BF16)
    ff = None
    for c in range(0, D_FF, FFN_CHUNK):
        a = jnp.maximum(_dot(h, w1_ref[:, c:c + FFN_CHUNK]), 0.0)
        part = _dot((a * a).astype(BF16), w2_ref[c:c + FFN_CHUNK, :])
        ff = part if ff is None else ff + part
    y = x + mod_ref[0, 5:6, :] * ff
    if final_norm:
        y = y * lax.rsqrt(jnp.mean(y * y, axis=-1, keepdims=True) + NORM_EPS) * fin_ref[...]
    o_ref[0] = y


def _ffn_call(x, mod, gain, final_gain, w1, w2, final_norm):
    B, T, _ = x.shape
    tm = TM_FFN
    tok = pl.BlockSpec((1, tm, D_MODEL), lambda b, i: (b, i, 0))
    whole = pl.BlockSpec(memory_space=pltpu.VMEM)
    return pl.pallas_call(
        functools.partial(_ffn_kernel, final_norm=final_norm),
        grid=(B, T // tm),
        in_specs=[tok, pl.BlockSpec((1, 6, D_MODEL), lambda b, i: (b, 0, 0)), _resident((1, D_MODEL)),
                  _resident((1, D_MODEL)), whole, whole],
        out_specs=tok,
        out_shape=jax.ShapeDtypeStruct((B, T, D_MODEL), F32),
        compiler_params=_params(("arbitrary", "arbitrary")),
        name="ffn",
    )(x, mod, gain, final_gain, w1, w2)


def _rotary_tables(T):
    half = HEAD_DIM // 2
    inv = (ROPE_BASE ** (-np.arange(half, dtype=np.float32) / half)).astype(np.float32)
    ang = (np.arange(T, dtype=np.float32)[:, None] * inv[None, :]).astype(np.float32).astype(np.float64)
    lane = np.arange(LANES) % HEAD_DIM
    cos = np.cos(ang)[:, lane % half]
    sin = np.sin(ang)[:, lane % half] * np.where(lane < half, -1.0, 1.0)[None, :]
    return jnp.asarray(cos, F32), jnp.asarray(sin, F32)


def _trunk(x, mods, layer_params, shared):
    B, T, _ = x.shape
    cos_t, sin_t = _rotary_tables(T)
    rows = T // GRID_W
    for l, lp in enumerate(layer_params):
        mod = mods[l]
        ret_in, u, naq, nak, nav, sq, sk, sv = _proj_call(x, mod, lp["gain1"], lp["w_in"], cos_t, sin_t)
        ret = _ret_call(ret_in, lp["ret_tables"], lp["ret_gn_gain"])
        na = _na_call(naq, nak, nav, lp["na_tables"][rows])
        swa = _swa_call(sq, sk, sv, shared["swa_table"], lp["swa_sink"])
        conv_params = (lp["conv_dw_kernel"], lp["conv_dw_bias"], lp["conv_ln_gain"], lp["conv_ln_bias"])
        x = _merge_call(x, mod, lp["gain1"], ret, u, na, swa, conv_params, lp["w_merge"], lp["w_branch"], lp["w_out"])
        x = _ffn_call(x, mod, lp["gain2"], shared["final_gain"], lp["w_ff1"], lp["w_ff2"], l == len(layer_params) - 1)
    return x


def kernel(x_prompt, x_sample, c_prompt, c_sample, w_ada, b_ada, norm_gain, w_in, ret_decay_logit, ret_gn_gain,
           conv_dw_kernel, conv_dw_bias, conv_ln_gain, conv_ln_bias, na_rpb, swa_sink, t5_bias,
           w_branch, w_merge, w_out, w_ff1, w_ff2, final_gain):
    nbp = c_prompt.shape[0]
    c_all = jnp.concatenate([c_prompt, c_sample], axis=0)
    mods = _ada_call(c_all, w_ada, b_ada.astype(F32))
    mods = mods.reshape(DEPTH, c_all.shape[0], 6, D_MODEL)
    row_counts = sorted({x_prompt.shape[1] // GRID_W, x_sample.shape[1] // GRID_W})
    layer_params = []
    for l in range(DEPTH):
        layer_params.append(dict(
            gain1=norm_gain[l, 0].reshape(1, D_MODEL).astype(F32),
            gain2=norm_gain[l, 1].reshape(1, D_MODEL).astype(F32),
            w_in=w_in[l].astype(BF16),
            ret_tables=_ret_tables(ret_decay_logit[l]),
            ret_gn_gain=ret_gn_gain[l],
            conv_dw_kernel=conv_dw_kernel[l], conv_dw_bias=conv_dw_bias[l],
            conv_ln_gain=conv_ln_gain[l], conv_ln_bias=conv_ln_bias[l],
            na_tables=_na_bias_tables(na_rpb[l], row_counts),
            swa_sink=swa_sink[l],
            w_merge=w_merge[l].astype(BF16), w_branch=w_branch[l].astype(BF16), w_out=w_out[l].astype(BF16),
            w_ff1=w_ff1[l].astype(BF16), w_ff2=w_ff2[l].astype(BF16),
        ))
    shared = dict(swa_table=_swa_bias_table(t5_bias), final_gain=final_gain.reshape(1, D_MODEL).astype(F32))
    y_prompt = _trunk(x_prompt, [m[:nbp] for m in mods], layer_params, shared)
    y_sample = _trunk(x_sample, [m[nbp:] for m in mods], layer_params, shared)
    return (y_prompt, y_sample)
```

```python
import functools
import math

import jax
import jax.numpy as jnp
import numpy as np
from jax import lax
from jax.experimental import pallas as pl
from jax.experimental.pallas import tpu as pltpu

F32 = jnp.float32
BF16 = jnp.bfloat16

D_MODEL = 1024
DEPTH = 2
HEAD_DIM = 64
BRANCH_WIDTH = 256
N_BRANCHES = 4
RET_CHUNK = 128
RET_CHUNKS_PER_STEP = 16
ROPE_BASE = 10000.0
CONV_WIDTH = 31
CONV_HALO = 16
GRID_W = 64
NA_WIN_ROWS = 8
NA_WIN_COLS = 16
NA_QROWS = 2
NA_BLOCKS_PER_STEP = 8
SWA_WINDOW = 128
SWA_BLOCK = 128
T5_BUCKETS = 32
T5_MAX_DIST = 128
D_FF = 4 * D_MODEL
NORM_EPS = 1e-6
NEG_INF = -1e30
SWA_KV_WIDTH = 128
RET_COL0 = 0
CONV_COL0 = RET_COL0 + 4 * BRANCH_WIDTH
NA_COL0 = CONV_COL0 + 2 * BRANCH_WIDTH
SWA_COL0 = NA_COL0 + 3 * BRANCH_WIDTH
IN_COLS = SWA_COL0 + BRANCH_WIDTH + 2 * SWA_KV_WIDTH
LANES = 128
SUBLANES = 8

TM_PROJ = 1024
PROJ_ROW_SPLITS = 4
TM_MERGE = 512
MERGE_COL_SPLITS = 4
TM_FFN = 1024
FFN_CHUNK = 1024
CONV_ROWS = 32
TQ_SWA = 1024
VMEM_LIMIT = 56 * 1024 * 1024


def _params(sem, flags=None):
    return pltpu.CompilerParams(dimension_semantics=sem, vmem_limit_bytes=VMEM_LIMIT, flags=flags)


def _resident(shape):
    nd = len(shape)
    return pl.BlockSpec(shape, lambda *_: (0,) * nd)


def _modulated_rmsnorm(x, gain, scale, shift):
    y = x * lax.rsqrt(jnp.mean(x * x, axis=-1, keepdims=True) + NORM_EPS)
    return (y * gain) * (1.0 + scale) + shift


def _dot(a, b):
    return jnp.dot(a, b, preferred_element_type=F32)


def _dot_nt(a, b):
    return lax.dot_general(a, b, (((1,), (1,)), ((), ())), preferred_element_type=F32)


def _dot_tn(a, b):
    return lax.dot_general(a, b, (((0,), (0,)), ((), ())), preferred_element_type=F32)


def _zero_tied_to(v):
    bits = lax.bitcast_convert_type(v[0:1, :].astype(F32), jnp.uint32)
    zero = lax.shift_right_logical(lax.shift_right_logical(bits, jnp.uint32(16)), jnp.uint32(16))
    return zero.astype(F32).astype(BF16)


def _dot_split(x, w):
    hi = x.astype(BF16)
    lo = (x - hi.astype(F32)).astype(BF16)
    return _dot(hi, w) + _dot(lo, w)


def _ada_kernel(c_ref, w_ref, b_ref, o_ref):
    c = c_ref[...]
    a = (c * jax.nn.sigmoid(c)).astype(BF16)
    o_ref[0] = _dot(a, w_ref[0].astype(BF16)) + b_ref[0]


def _ada_call(c_all, w_ada, b_ada):
    nb = c_all.shape[0]
    return pl.pallas_call(
        _ada_kernel,
        grid=(DEPTH, 6),
        in_specs=[
            pl.BlockSpec((nb, D_MODEL), lambda l, j: (0, 0)),
            pl.BlockSpec((1, D_MODEL, D_MODEL), lambda l, j: (l, 0, j)),
            pl.BlockSpec((1, 1, D_MODEL), lambda l, j: (l, 0, j)),
        ],
        out_specs=pl.BlockSpec((1, nb, D_MODEL), lambda l, j: (l, 0, j)),
        out_shape=jax.ShapeDtypeStruct((DEPTH, nb, 6 * D_MODEL), F32),
        compiler_params=_params(("arbitrary", "arbitrary")),
        name="ada",
    )(c_all, w_ada, b_ada.reshape(DEPTH, 1, 6 * D_MODEL))


def _proj_kernel(x_ref, mod_ref, gain_ref, w_ref, cos_ref, sin_ref,
                 ret_ref, u_ref, naq_ref, nak_ref, nav_ref, sq_ref, sk_ref, sv_ref):
    tm = x_ref.shape[1]
    W = BRANCH_WIDTH
    half = tm // PROJ_ROW_SPLITS
    lane = lax.broadcasted_iota(jnp.int32, (half, LANES), 1)
    first_half = (lane & (HEAD_DIM // 2)) == 0
    qscale = HEAD_DIM ** -0.5
    blocks = [slice(n * half, (n + 1) * half) for n in range(PROJ_ROW_SPLITS)]
    hs = [_modulated_rmsnorm(x_ref[0, rows, :], gain_ref[...], mod_ref[0, 1:2, :], mod_ref[0, 0:1, :]).astype(BF16)
          for rows in blocks]
    for rows, h in zip(blocks, hs):
        cos = cos_ref[rows, :]
        sin = sin_ref[rows, :]

        def rot(y):
            sw = jnp.where(first_half, pltpu.roll(y, LANES - HEAD_DIM // 2, 1), pltpu.roll(y, HEAD_DIM // 2, 1))
            return y * cos + sw * sin

        y = _dot(h, w_ref[:, RET_COL0:RET_COL0 + 4 * W])
        for t in range(W // LANES):
            ret_ref[0, rows, t * LANES:(t + 1) * LANES] = (rot(y[:, t * LANES:(t + 1) * LANES]) * qscale).astype(BF16)
        for t in range(W // LANES, 2 * W // LANES):
            ret_ref[0, rows, t * LANES:(t + 1) * LANES] = rot(y[:, t * LANES:(t + 1) * LANES]).astype(BF16)
        ret_ref[0, rows, 2 * W:4 * W] = y[:, 2 * W:4 * W].astype(BF16)
        y = _dot(h, w_ref[:, CONV_COL0:CONV_COL0 + 2 * W])
        u_ref[0, rows, :] = (y[:, 0:W] * jax.nn.sigmoid(y[:, W:2 * W])).astype(BF16)
        y = _dot(h, w_ref[:, NA_COL0:NA_COL0 + 3 * W])
        naq_ref[0, rows, :] = (y[:, 0:W] * qscale).astype(BF16)
        nak_ref[0, rows, :] = y[:, W:2 * W].astype(BF16)
        nav_ref[0, rows, :] = y[:, 2 * W:3 * W].astype(BF16)
        y = _dot(h, w_ref[:, SWA_COL0:IN_COLS])
        sq_ref[0, rows, :] = (y[:, 0:W] * qscale).astype(BF16)
        sk_ref[0, rows, :] = y[:, W:W + SWA_KV_WIDTH].astype(BF16)
        sv_ref[0, rows, :] = y[:, W + SWA_KV_WIDTH:W + 2 * SWA_KV_WIDTH].astype(BF16)


def _proj_call(x, mod, gain, w_in, cos_t, sin_t):
    B, T, _ = x.shape
    tm = TM_PROJ
    tok = lambda w: pl.BlockSpec((1, tm, w), lambda b, i: (b, i, 0))
    W = BRANCH_WIDTH
    widths = (4 * W, W, W, W, W, W, SWA_KV_WIDTH, SWA_KV_WIDTH)
    return pl.pallas_call(
        _proj_kernel,
        grid=(B, T // tm),
        in_specs=[
            tok(D_MODEL),
            pl.BlockSpec((1, 6, D_MODEL), lambda b, i: (b, 0, 0)),
            _resident((1, D_MODEL)),
            _resident((D_MODEL, IN_COLS)),
            pl.BlockSpec((tm, LANES), lambda b, i: (i, 0)),
            pl.BlockSpec((tm, LANES), lambda b, i: (i, 0)),
        ],
        out_specs=[tok(w) for w in widths],
        out_shape=[jax.ShapeDtypeStruct((B, T, w), BF16) for w in widths],
        compiler_params=_params(("arbitrary", "arbitrary")),
        name="proj",
    )(x, mod, gain, w_in, cos_t, sin_t)


def _ret_kernel(x_ref, dec_ref, xif_ref, xib_ref, zf_ref, zb_ref, gf_ref, gb_ref, bd_ref, avg_ref, gn_ref,
                o_ref, rf_ref, rb_ref, stash_ref):
    p = pl.program_id(1)
    n = pl.program_id(2)
    nsteps = pl.num_programs(2)
    C = RET_CHUNK
    W = BRANCH_WIDTH
    G = x_ref.shape[1] // C
    heads = W // HEAD_DIM

    def chunk_kv(j):
        rows = slice(j * C, (j + 1) * C)
        return x_ref[0, rows, W:2 * W], x_ref[0, rows, 2 * W:3 * W]

    def chunk_state(k, v, zeta):
        return _dot_tn((k.astype(F32) * zeta).astype(BF16), v) * bd_ref[...]

    @pl.when(p == 0)
    def _forward_states():
        @pl.when(n == 0)
        def _():
            rf_ref[...] = jnp.zeros_like(rf_ref)

        updates = [chunk_state(*chunk_kv(j), zf_ref[...]) for j in range(G)]
        rf = rf_ref[...]
        for j in range(G):
            stash_ref[n * G + j] = rf.astype(BF16)
            rf = gf_ref[...] * rf + updates[j]
        rf_ref[...] = rf

    @pl.when(p == 1)
    def _outputs():
        @pl.when(n == 0)
        def _():
            rb_ref[...] = jnp.zeros_like(rb_ref)

        first_chunk = (nsteps - 1 - n) * G
        lane_head = lax.broadcasted_iota(jnp.int32, (C, W), 1) // HEAD_DIM

        def head_rows(a):
            return jnp.concatenate([jnp.where(lane_head == hh, a, jnp.zeros_like(a)) for hh in range(heads)], axis=0)

        qs = [x_ref[0, j * C:(j + 1) * C, 0:W] for j in range(G)]
        weights = [(_dot_nt(qs[j], head_rows(chunk_kv(j)[0])) * dec_ref[...]).astype(BF16) for j in range(G)]
        updates = [chunk_state(*chunk_kv(j), zb_ref[...]) for j in range(G)]
        rb = rb_ref[...]
        later = [None] * G
        for j in reversed(range(G)):
            later[j] = rb.astype(BF16)
            rb = gb_ref[...] * rb + updates[j]
        rb_ref[...] = rb
        outs = []
        for j in range(G):
            qf = qs[j].astype(F32)
            o = _dot(weights[j], head_rows(chunk_kv(j)[1]))
            o = o + _dot((qf * xif_ref[...]).astype(BF16), stash_ref[first_chunk + j])
            outs.append(o + _dot((qf * xib_ref[...]).astype(BF16), later[j]))
        o = jnp.concatenate(outs, axis=0)
        avg = avg_ref[...]
        d = o - _dot_split(o, avg)
        var = _dot_split(d * d, avg)
        g = x_ref[0, :, 3 * W:4 * W].astype(F32)
        o_ref[0] = (d * lax.rsqrt(var + NORM_EPS) * gn_ref[...] * (g * jax.nn.sigmoid(g))).astype(BF16)


def _ret_tables(decay_logit):
    C = RET_CHUNK
    heads = BRANCH_WIDTH // HEAD_DIM
    lg = jax.nn.log_sigmoid(decay_logit.astype(F32))
    pos = np.arange(C, dtype=np.float32)
    diff = pos[:, None] - pos[None, :]
    fwd = jnp.exp(jnp.where(diff >= 0, diff[None] * lg[0][:, None, None], -jnp.inf))
    bwd = jnp.exp(jnp.where(diff < 0, -diff[None] * lg[1][:, None, None], -jnp.inf))
    dec = (fwd + bwd).transpose(1, 0, 2).reshape(C, heads * C)
    lane_lg = jnp.repeat(lg, HEAD_DIM, axis=1)
    xif = jnp.exp((pos + 1.0)[:, None] * lane_lg[0][None, :])
    xib = jnp.exp((C - pos)[:, None] * lane_lg[1][None, :])
    zf = jnp.exp((C - 1.0 - pos)[:, None] * lane_lg[0][None, :])
    zb = jnp.exp(pos[:, None] * lane_lg[1][None, :])
    gf = jnp.exp(C * lane_lg[0])[None, :]
    gb = jnp.exp(C * lane_lg[1])[None, :]
    return dec, xif, xib, zf, zb, gf, gb


def _head_block_constants():
    hid = np.arange(BRANCH_WIDTH) // HEAD_DIM
    same = (hid[:, None] == hid[None, :])
    return jnp.asarray(same.astype(np.float32)), jnp.asarray(same.astype(np.float32) / HEAD_DIM, dtype=BF16)


def _ret_call(ret_in, tables, gn_gain):
    B, T, _ = ret_in.shape
    rows = RET_CHUNK * RET_CHUNKS_PER_STEP
    N = T // RET_CHUNK
    S = T // rows
    bd, avg = _head_block_constants()
    step_rows = lambda b, p, n: (b, jnp.where(p == 0, n, S - 1 - n), 0)
    out_rows = lambda b, p, n: (b, jnp.where(p == 0, S - 1, S - 1 - n), 0)
    consts = list(tables) + [bd, avg, gn_gain.reshape(1, BRANCH_WIDTH).astype(F32)]
    return pl.pallas_call(
        _ret_kernel,
        grid=(B, 2, S),
        in_specs=[pl.BlockSpec((1, rows, 4 * BRANCH_WIDTH), step_rows)] + [_resident(t.shape) for t in consts],
        out_specs=pl.BlockSpec((1, rows, BRANCH_WIDTH), out_rows),
        out_shape=jax.ShapeDtypeStruct((B, T, BRANCH_WIDTH), BF16),
        scratch_shapes=[
            pltpu.VMEM((BRANCH_WIDTH, BRANCH_WIDTH), F32),
            pltpu.VMEM((BRANCH_WIDTH, BRANCH_WIDTH), F32),
            pltpu.VMEM((N, BRANCH_WIDTH, BRANCH_WIDTH), BF16),
        ],
        compiler_params=_params(("arbitrary", "arbitrary", "arbitrary")),
        name="ret",
    )(ret_in, *consts)


def _conv_fill_window(u_ref, win_ref, i, nt, tt):
    T = u_ref.shape[1]
    t0 = pl.multiple_of(i * tt, tt)
    left_start = pl.multiple_of(jnp.maximum(t0 - CONV_HALO, 0), CONV_HALO)
    right_start = pl.multiple_of(jnp.minimum(t0 + tt, T - CONV_HALO), CONV_HALO)
    left = u_ref[0, pl.ds(left_start, CONV_HALO), :].astype(F32) * (i > 0).astype(F32)
    right = u_ref[0, pl.ds(right_start, CONV_HALO), :].astype(F32) * (i < nt - 1).astype(F32)
    win = jnp.concatenate([left, u_ref[0, pl.ds(t0, tt), :].astype(F32), right], axis=0)
    span = win_ref.shape[1]
    for s in range(SUBLANES):
        win_ref[s] = win[s:s + span]


def _conv_rows(win_ref, w_ref, b_ref, lg_ref, lb_ref, r):
    off = CONV_HALO - CONV_WIDTH // 2
    acc = jnp.zeros((CONV_ROWS // SUBLANES, SUBLANES, BRANCH_WIDTH), F32)
    for tap in range(CONV_WIDTH):
        s, a = (tap + off) % SUBLANES, (tap + off) // SUBLANES * SUBLANES
        rows = win_ref[s, r + a:r + a + CONV_ROWS, :].reshape(acc.shape)
        acc = acc + rows * w_ref[tap][None]
    acc = acc.reshape(CONV_ROWS, BRANCH_WIDTH) + b_ref[...]
    mu = jnp.mean(acc, axis=-1, keepdims=True)
    d = acc - mu
    var = jnp.mean(d * d, axis=-1, keepdims=True)
    y = d * lax.rsqrt(var + NORM_EPS) * lg_ref[...] + lb_ref[...]
    return (y * jax.nn.sigmoid(y)).astype(BF16)


def _na_layout(rows):
    R = NA_QROWS
    KR = R + NA_WIN_ROWS
    nblk = rows // R
    qc = np.arange(GRID_W)
    ws = np.clip(qc - NA_WIN_COLS // 2, 0, GRID_W - NA_WIN_COLS)
    col_ok = (qc[None, :] >= ws[:, None]) & (qc[None, :] < ws[:, None] + NA_WIN_COLS)
    dc = qc[None, :] - qc[:, None] + NA_WIN_COLS - 1
    n_dr = 2 * NA_WIN_ROWS - 1
    variants, var_of_block, kr0s = [], [], []
    for j in range(nblk):
        r0 = j * R
        kr0 = int(np.clip(r0 - NA_WIN_ROWS // 2, 0, rows - KR))
        r = r0 + np.arange(R)
        start = np.clip(r - NA_WIN_ROWS // 2, 0, rows - NA_WIN_ROWS)
        kr = kr0 + np.arange(KR)
        row_ok = (kr[None, :] >= start[:, None]) & (kr[None, :] < start[:, None] + NA_WIN_ROWS)
        tile = np.where(row_ok, kr[None, :] - r[:, None] + NA_WIN_ROWS - 1, n_dr)
        key = tile.tobytes()
        if key not in [v[0] for v in variants]:
            variants.append((key, tile))
        var_of_block.append([v[0] for v in variants].index(key))
        kr0s.append(kr0)
    tiles = np.stack([v[1] for v in variants])
    onehot = (dc[None] == np.arange(2 * NA_WIN_COLS - 1)[:, None, None]) & col_ok[None]
    return tiles, onehot, col_ok, np.asarray(var_of_block, np.int32), np.asarray(kr0s, np.int32)


def _na_bias_table(rpb, rows):
    tiles, onehot, col_ok, _, _ = _na_layout(rows)
    H, n_dr, n_dc = rpb.shape
    oh = jnp.asarray(onehot.reshape(n_dc, GRID_W * GRID_W), F32)
    tz = jnp.dot(rpb.astype(F32).reshape(H * n_dr, n_dc), oh, precision=lax.Precision.HIGHEST)
    tz = jnp.where(col_ok[None, None], tz.reshape(H, n_dr, GRID_W, GRID_W), NEG_INF)
    tz = jnp.concatenate([tz, jnp.full((H, 1, GRID_W, GRID_W), NEG_INF, F32)], axis=1)
    nv, R, KR = tiles.shape
    per_variant = []
    for vv in range(nv):
        slabs = [jnp.concatenate([tz[:, int(tiles[vv, rl, kl])] for kl in range(KR)], axis=-1) for rl in range(R)]
        per_variant.append(jnp.concatenate(slabs, axis=1))
    return jnp.stack(per_variant)


def _na_bias_tables(rpb, row_counts):
    by_layout, out = {}, {}
    for rows in row_counts:
        key = _na_layout(rows)[0].tobytes()
        if key not in by_layout:
            by_layout[key] = _na_bias_table(rpb, rows)
        out[rows] = by_layout[key]
    return out


def _na_kernel(var_ref, kr0_ref, q_ref, k_ref, v_ref, *rest):
    tab_refs, o_ref = rest[:-1], rest[-1]
    step = pl.program_id(1)
    nq = q_ref.shape[1] // NA_BLOCKS_PER_STEP
    nk = tab_refs[0].shape[3]
    low = lax.broadcasted_iota(jnp.int32, (nq, LANES), 1) < HEAD_DIM
    units = [(sb, pair) for sb in range(len(tab_refs)) for pair in range(BRANCH_WIDTH // LANES)]
    kstart = [pl.multiple_of(kr0_ref[step * NA_BLOCKS_PER_STEP + sb] * GRID_W, GRID_W) for sb in range(len(tab_refs))]
    scores = []
    for sb, pair in units:
        cols = slice(pair * LANES, (pair + 1) * LANES)
        q2 = q_ref[0, sb * nq:(sb + 1) * nq, cols]
        zero = jnp.zeros_like(q2)
        qs = jnp.concatenate([jnp.where(low, q2, zero), jnp.where(low, zero, q2)], axis=0)
        scores.append(_dot_nt(qs, k_ref[0, pl.ds(kstart[sb], nk), cols]) + tab_refs[sb][0, pair])
    weights = []
    for s in scores:
        e = jnp.exp(s - jnp.max(s, axis=-1, keepdims=True))
        weights.append((e.astype(BF16), 1.0 / jnp.sum(e, axis=-1, keepdims=True)))
    for (sb, pair), (e, inv) in zip(units, weights):
        cols = slice(pair * LANES, (pair + 1) * LANES)
        o2 = _dot(e, v_ref[0, pl.ds(kstart[sb], nk), cols]) * inv
        o_ref[0, sb * nq:(sb + 1) * nq, cols] = jnp.where(low, o2[:nq], o2[nq:]).astype(BF16)


def _na_call(q, k, v, tab):
    B, T, _ = q.shape
    rows = T // GRID_W
    _, _, _, var_of_block, kr0s = _na_layout(rows)
    nq = NA_QROWS * GRID_W
    per_step = NA_BLOCKS_PER_STEP
    nblk = rows // NA_QROWS
    assert nblk % per_step == 0
    nv, heads, _, nk = tab.shape
    tab = tab.reshape(nv, heads // 2, 2 * nq, nk)
    tab_spec = lambda sb: pl.BlockSpec((1, heads // 2, 2 * nq, nk),
                                       lambda b, j, var, kr0: (var[j * per_step + sb], 0, 0, 0))
    grid_spec = pltpu.PrefetchScalarGridSpec(
        num_scalar_prefetch=2,
        grid=(B, nblk // per_step),
        in_specs=[
            pl.BlockSpec((1, per_step * nq, BRANCH_WIDTH), lambda b, j, var, kr0: (b, j, 0)),
            pl.BlockSpec((1, T, BRANCH_WIDTH), lambda b, j, var, kr0: (b, 0, 0)),
            pl.BlockSpec((1, T, BRANCH_WIDTH), lambda b, j, var, kr0: (b, 0, 0)),
        ] + [tab_spec(sb) for sb in range(per_step)],
        out_specs=pl.BlockSpec((1, per_step * nq, BRANCH_WIDTH), lambda b, j, var, kr0: (b, j, 0)),
    )
    return pl.pallas_call(
        _na_kernel,
        grid_spec=grid_spec,
        out_shape=jax.ShapeDtypeStruct((B, T, BRANCH_WIDTH), BF16),
        compiler_params=_params(("arbitrary", "arbitrary")),
        name="na",
    )(jnp.asarray(var_of_block), jnp.asarray(kr0s), q, k, v, *([tab] * per_step))


def _t5_bucket(rel):
    half = T5_BUCKETS // 2
    exact = half // 2
    n = np.abs(rel)
    large = exact + (np.log(np.maximum(n, 1) / exact) / math.log(T5_MAX_DIST / exact) * (half - exact)).astype(np.int64)
    large = np.minimum(large, half - 1)
    return (rel > 0).astype(np.int64) * half + np.where(n < exact, n, large)


def _swa_bias_table(t5_bias):
    kpos = np.arange(3 * SWA_BLOCK) - SWA_BLOCK
    rel = kpos[None, :] - np.arange(SWA_BLOCK)[:, None]
    onehot = (_t5_bucket(rel)[None] == np.arange(T5_BUCKETS)[:, None, None])
    oh = jnp.asarray(onehot.reshape(T5_BUCKETS, -1), F32)
    tab = jnp.dot(t5_bias.astype(F32).T, oh, precision=lax.Precision.HIGHEST).reshape((-1,) + rel.shape)
    tab = jnp.where((np.abs(rel) <= SWA_WINDOW)[None], tab, NEG_INF)
    before = (kpos < 0)[None, None, :]
    after = (kpos >= SWA_BLOCK)[None, None, :]
    tabs = jnp.stack([jnp.where(before, NEG_INF, tab), tab, jnp.where(after, NEG_INF, tab)])
    hq = tab.shape[0]
    return tabs.reshape(3, hq // 2, 2 * SWA_BLOCK, 3 * SWA_BLOCK)


def _swa_kernel(sink_ref, q_ref, k_ref, v_ref, tab_ref, o_ref):
    n = pl.program_id(1)
    tq = q_ref.shape[1]
    T = k_ref.shape[1]
    nb = T // SWA_BLOCK
    per_step = tq // SWA_BLOCK
    low = lax.broadcasted_iota(jnp.int32, (SWA_BLOCK, LANES), 1) < HEAD_DIM
    first_rows = lax.broadcasted_iota(jnp.int32, (2 * SWA_BLOCK, 1), 0) < SWA_BLOCK
    units = [(i, hk) for i in range(per_step) for hk in range(2)]

    def band(ref, blk):
        prev_start = pl.multiple_of(jnp.maximum(blk - 1, 0) * SWA_BLOCK, SWA_BLOCK)
        cur_start = pl.multiple_of(blk * SWA_BLOCK, SWA_BLOCK)
        next_start = pl.multiple_of(jnp.minimum(blk + 1, nb - 1) * SWA_BLOCK, SWA_BLOCK)
        return jnp.concatenate([ref[0, pl.ds(prev_start, SWA_BLOCK), :],
                                ref[0, pl.ds(cur_start, SWA_BLOCK), :],
                                ref[0, pl.ds(next_start, SWA_BLOCK), :]], axis=0)

    scores = []
    for i, hk in units:
        blk = n * per_step + i
        variant = jnp.where(blk == 0, 0, jnp.where(blk == nb - 1, 2, 1))
        q2 = q_ref[0, i * SWA_BLOCK:(i + 1) * SWA_BLOCK, hk * LANES:(hk + 1) * LANES]
        q2r = pltpu.roll(q2, HEAD_DIM, 1)
        zero = jnp.zeros_like(q2)
        if hk == 0:
            qs = jnp.concatenate([jnp.where(low, q2, zero), jnp.where(low, q2r, zero)], axis=0)
        else:
            qs = jnp.concatenate([jnp.where(low, zero, q2r), jnp.where(low, zero, q2)], axis=0)
        scores.append(_dot_nt(qs, band(k_ref, blk)) + tab_ref[variant, hk])
    weights = []
    for (i, hk), s in zip(units, scores):
        sink = jnp.where(first_rows, sink_ref[2 * hk], sink_ref[2 * hk + 1])
        m = jnp.maximum(jnp.max(s, axis=-1, keepdims=True), sink)
        e = jnp.exp(s - m)
        denom = jnp.sum(e, axis=-1, keepdims=True) + jnp.exp(sink - m)
        weights.append((e.astype(BF16), 1.0 / denom))
    for (i, hk), (e, inv) in zip(units, weights):
        o2 = _dot(e, band(v_ref, n * per_step + i)) * inv
        top, bot = o2[:SWA_BLOCK], o2[SWA_BLOCK:]
        if hk == 0:
            out = jnp.where(low, top, pltpu.roll(bot, HEAD_DIM, 1))
        else:
            out = jnp.where(low, pltpu.roll(top, HEAD_DIM, 1), bot)
        o_ref[0, i * SWA_BLOCK:(i + 1) * SWA_BLOCK, hk * LANES:(hk + 1) * LANES] = out.astype(BF16)


def _swa_call(q, k, v, tab, sink):
    B, T, _ = q.shape
    tq = TQ_SWA
    assert T // SWA_BLOCK >= 2
    return pl.pallas_call(
        _swa_kernel,
        grid=(B, T // tq),
        in_specs=[
            pl.BlockSpec(memory_space=pltpu.SMEM),
            pl.BlockSpec((1, tq, BRANCH_WIDTH), lambda b, i: (b, i, 0)),
            pl.BlockSpec((1, T, LANES), lambda b, i: (b, 0, 0)),
            pl.BlockSpec((1, T, LANES), lambda b, i: (b, 0, 0)),
            _resident(tab.shape),
        ],
        out_specs=pl.BlockSpec((1, tq, BRANCH_WIDTH), lambda b, i: (b, i, 0)),
        out_shape=jax.ShapeDtypeStruct((B, T, BRANCH_WIDTH), BF16),
        compiler_params=_params(("arbitrary", "arbitrary")),
        name="swa",
    )(sink.astype(F32), q, k, v, tab)


def _merge_kernel(x_ref, mod_ref, gain_ref, ret_ref, u_ref, na_ref, swa_ref, cw_ref, cb_ref, clg_ref, clb_ref,
                  wm_ref, wb_ref, wo_ref, o_ref, win_ref, conv_ref):
    tm = x_ref.shape[1]
    x = x_ref[0]
    h = _modulated_rmsnorm(x, gain_ref[...], mod_ref[0, 1:2, :], mod_ref[0, 0:1, :]).astype(BF16)
    _conv_fill_window(u_ref, win_ref, pl.program_id(1), pl.num_programs(1), tm)
    branch_refs = {0: ret_ref, 2: na_ref, 3: swa_ref}
    width = D_MODEL // MERGE_COL_SPLITS
    col_blocks = [slice(c * width, (c + 1) * width) for c in range(MERGE_COL_SPLITS)]
    slots = [(i, c) for c in range(MERGE_COL_SPLITS) for i in (0, 2, 3, 1)]
    groups = list(range(0, tm, CONV_ROWS))
    after_slot = [[] for _ in slots]
    for g, r in enumerate(groups):
        after_slot[1 + g * (len(slots) - 1) // len(groups)].append(r)
    merged = [None] * MERGE_COL_SPLITS
    conv_gate = [None] * MERGE_COL_SPLITS
    tie = None
    for n, (i, c) in enumerate(slots):
        gate = jax.nn.sigmoid(_dot(h, wm_ref[i, :, col_blocks[c]]))
        if i in branch_refs:
            br = branch_refs[i][0]
            term = gate * _dot(br if tie is None else br + tie, wb_ref[i, :, col_blocks[c]])
            merged[c] = term if merged[c] is None else merged[c] + term
        else:
            conv_gate[c] = gate
        for r in after_slot[n]:
            rows = _conv_rows(win_ref, cw_ref, cb_ref, clg_ref, clb_ref, r)
            conv_ref[r:r + CONV_ROWS, :] = rows
            tie = _zero_tied_to(rows)
    merged = jnp.concatenate(merged, axis=1) + jnp.concatenate(conv_gate, axis=1) * _dot(conv_ref[...], wb_ref[1])
    o_ref[0] = x + mod_ref[0, 2:3, :] * _dot(merged.astype(BF16), wo_ref[...])


def _merge_call(x, mod, gain, ret, u, na, swa, conv_params, w_merge, w_branch, w_out):
    B, T, _ = x.shape
    tm = TM_MERGE
    tok = lambda w: pl.BlockSpec((1, tm, w), lambda b, i: (b, i, 0))
    whole = pl.BlockSpec(memory_space=pltpu.VMEM)
    dw_kernel, dw_bias, ln_gain, ln_bias = conv_params
    row = lambda a: a.reshape(1, BRANCH_WIDTH).astype(F32)
    taps = jnp.broadcast_to(dw_kernel.astype(F32)[:, None, :], (CONV_WIDTH, SUBLANES, BRANCH_WIDTH))
    return pl.pallas_call(
        _merge_kernel,
        grid=(B, T // tm),
        in_specs=[tok(D_MODEL), pl.BlockSpec((1, 6, D_MODEL), lambda b, i: (b, 0, 0)), _resident((1, D_MODEL)),
                  tok(BRANCH_WIDTH), pl.BlockSpec((1, T, BRANCH_WIDTH), lambda b, i: (b, 0, 0)),
                  tok(BRANCH_WIDTH), tok(BRANCH_WIDTH),
                  _resident((CONV_WIDTH, SUBLANES, BRANCH_WIDTH)),
                  _resident((1, BRANCH_WIDTH)), _resident((1, BRANCH_WIDTH)), _resident((1, BRANCH_WIDTH)),
                  whole, whole, whole],
        out_specs=tok(D_MODEL),
        out_shape=jax.ShapeDtypeStruct((B, T, D_MODEL), F32),
        scratch_shapes=[pltpu.VMEM((SUBLANES, tm + 2 * CONV_HALO - SUBLANES, BRANCH_WIDTH), F32),
                        pltpu.VMEM((tm, BRANCH_WIDTH), BF16)],
        compiler_params=_params(("arbitrary", "arbitrary")),
        name="merge",
    )(x, mod, gain, ret, u, na, swa, taps, row(dw_bias), row(ln_gain), row(ln_bias), w_merge, w_branch, w_out)


def _ffn_kernel(x_ref, mod_ref, gain_ref, fin_ref, w1_ref, w2_ref, o_ref, *, final_norm):
    x = x_ref[0]
    h = _modulated_rmsnorm(x, gain_ref[...], mod_ref[0, 4:5, :], mod_ref[0, 3:4, :]).astype(BF16)
    ff = None
    for c in range(0, D_FF, FFN_CHUNK):
        a = jnp.maximum(_dot(h, w1_ref[:, c:c + FFN_CHUNK]), 0.0)
        part = _dot((a * a).astype(BF16), w2_ref[c:c + FFN_CHUNK, :])
        ff = part if ff is None else ff + part
    y = x + mod_ref[0, 5:6, :] * ff
    if final_norm:
        y = y * lax.rsqrt(jnp.mean(y * y, axis=-1, keepdims=True) + NORM_EPS) * fin_ref[...]
    o_ref[0] = y


def _ffn_call(x, mod, gain, final_gain, w1, w2, final_norm):
    B, T, _ = x.shape
    tm = TM_FFN
    tok = pl.BlockSpec((1, tm, D_MODEL), lambda b, i: (b, i, 0))
    whole = pl.BlockSpec(memory_space=pltpu.VMEM)
    return pl.pallas_call(
        functools.partial(_ffn_kernel, final_norm=final_norm),
        grid=(B, T // tm),
        in_specs=[tok, pl.BlockSpec((1, 6, D_MODEL), lambda b, i: (b, 0, 0)), _resident((1, D_MODEL)),
                  _resident((1, D_MODEL)), whole, whole],
        out_specs=tok,
        out_shape=jax.ShapeDtypeStruct((B, T, D_MODEL), F32),
        compiler_params=_params(("arbitrary", "arbitrary")),
        name="ffn",
    )(x, mod, gain, final_gain, w1, w2)


def _rotary_tables(T):
    half = HEAD_DIM // 2
    inv = (ROPE_BASE ** (-np.arange(half, dtype=np.float32) / half)).astype(np.float32)
    ang = (np.arange(T, dtype=np.float32)[:, None] * inv[None, :]).astype(np.float32).astype(np.float64)
    lane = np.arange(LANES) % HEAD_DIM
    cos = np.cos(ang)[:, lane % half]
    sin = np.sin(ang)[:, lane % half] * np.where(lane < half, -1.0, 1.0)[None, :]
    return jnp.asarray(cos, F32), jnp.asarray(sin, F32)


def _trunk(x, mods, layer_params, shared):
    B, T, _ = x.shape
    cos_t, sin_t = _rotary_tables(T)
    rows = T // GRID_W
    for l, lp in enumerate(layer_params):
        mod = mods[l]
        ret_in, u, naq, nak, nav, sq, sk, sv = _proj_call(x, mod, lp["gain1"], lp["w_in"], cos_t, sin_t)
        ret = _ret_call(ret_in, lp["ret_tables"], lp["ret_gn_gain"])
        na = _na_call(naq, nak, nav, lp["na_tables"][rows])
        swa = _swa_call(sq, sk, sv, shared["swa_table"], lp["swa_sink"])
        conv_params = (lp["conv_dw_kernel"], lp["conv_dw_bias"], lp["conv_ln_gain"], lp["conv_ln_bias"])
        x = _merge_call(x, mod, lp["gain1"], ret, u, na, swa, conv_params, lp["w_merge"], lp["w_branch"], lp["w_out"])
        x = _ffn_call(x, mod, lp["gain2"], shared["final_gain"], lp["w_ff1"], lp["w_ff2"], l == len(layer_params) - 1)
    return x


def kernel(x_prompt, x_sample, c_prompt, c_sample, w_ada, b_ada, norm_gain, w_in, ret_decay_logit, ret_gn_gain,
           conv_dw_kernel, conv_dw_bias, conv_ln_gain, conv_ln_bias, na_rpb, swa_sink, t5_bias,
           w_branch, w_merge, w_out, w_ff1, w_ff2, final_gain):
    nbp = c_prompt.shape[0]
    c_all = jnp.concatenate([c_prompt, c_sample], axis=0)
    mods = _ada_call(c_all, w_ada, b_ada.astype(F32))
    mods = mods.reshape(DEPTH, c_all.shape[0], 6, D_MODEL)
    row_counts = sorted({x_prompt.shape[1] // GRID_W, x_sample.shape[1] // GRID_W})
    layer_params = []
    for l in range(DEPTH):
        layer_params.append(dict(
            gain1=norm_gain[l, 0].reshape(1, D_MODEL).astype(F32),
            gain2=norm_gain[l, 1].reshape(1, D_MODEL).astype(F32),
            w_in=w_in[l].astype(BF16),
            ret_tables=_ret_tables(ret_decay_logit[l]),
            ret_gn_gain=ret_gn_gain[l],
            conv_dw_kernel=conv_dw_kernel[l], conv_dw_bias=conv_dw_bias[l],
            conv_ln_gain=conv_ln_gain[l], conv_ln_bias=conv_ln_bias[l],
            na_tables=_na_bias_tables(na_rpb[l], row_counts),
            swa_sink=swa_sink[l],
            w_merge=w_merge[l].astype(BF16), w_branch=w_branch[l].astype(BF16), w_out=w_out[l].astype(BF16),
            w_ff1=w_ff1[l].astype(BF16), w_ff2=w_ff2[l].astype(BF16),
        ))
    shared = dict(swa_table=_swa_bias_table(t5_bias), final_gain=final_gain.reshape(1, D_MODEL).astype(F32))
    y_prompt = _trunk(x_prompt, [m[:nbp] for m in mods], layer_params, shared)
    y_sample = _trunk(x_sample, [m[nbp:] for m in mods], layer_params, shared)
    return (y_prompt, y_sample)
```

```python
import functools
import math

import jax
import jax.numpy as jnp
import numpy as np
from jax import lax
from jax.experimental import pallas as pl
from jax.experimental.pallas import tpu as pltpu

F32 = jnp.float32
BF16 = jnp.bfloat16

D_MODEL = 1024
DEPTH = 2
HEAD_DIM = 64
BRANCH_WIDTH = 256
N_BRANCHES = 4
RET_CHUNK = 128
RET_CHUNKS_PER_STEP = 16
ROPE_BASE = 10000.0
CONV_WIDTH = 31
CONV_HALO = 16
GRID_W = 64
NA_WIN_ROWS = 8
NA_WIN_COLS = 16
NA_QROWS = 2
NA_BLOCKS_PER_STEP = 8
SWA_WINDOW = 128
SWA_BLOCK = 128
T5_BUCKETS = 32
T5_MAX_DIST = 128
D_FF = 4 * D_MODEL
NORM_EPS = 1e-6
NEG_INF = -1e30
LOG2E = 1.4426950408889634
SWA_KV_WIDTH = 128
RET_COL0 = 0
CONV_COL0 = RET_COL0 + 4 * BRANCH_WIDTH
NA_COL0 = CONV_COL0 + 2 * BRANCH_WIDTH
SWA_COL0 = NA_COL0 + 3 * BRANCH_WIDTH
IN_COLS = SWA_COL0 + BRANCH_WIDTH + 2 * SWA_KV_WIDTH
LANES = 128
SUBLANES = 8

TM_PROJ = 1024
PROJ_ROW_SPLITS = 4
TM_MERGE = 512
MERGE_COL_SPLITS = 4
TM_FFN = 1024
FFN_CHUNK = 1024
CONV_ROWS = 32
TQ_SWA = 1024
VMEM_LIMIT = 56 * 1024 * 1024


def _params(sem, flags=None):
    return pltpu.CompilerParams(dimension_semantics=sem, vmem_limit_bytes=VMEM_LIMIT, flags=flags)


def _resident(shape):
    nd = len(shape)
    return pl.BlockSpec(shape, lambda *_: (0,) * nd)


def _modulated_rmsnorm(x, gain, scale, shift):
    y = x * lax.rsqrt(jnp.mean(x * x, axis=-1, keepdims=True) + NORM_EPS)
    return (y * gain) * (1.0 + scale) + shift


def _dot(a, b):
    return jnp.dot(a, b, preferred_element_type=F32)


def _dot_nt(a, b):
    return lax.dot_general(a, b, (((1,), (1,)), ((), ())), preferred_element_type=F32)


def _dot_tn(a, b):
    return lax.dot_general(a, b, (((0,), (0,)), ((), ())), preferred_element_type=F32)


def _zero_tied_to(v):
    bits = lax.bitcast_convert_type(v[0:1, :].astype(F32), jnp.uint32)
    zero = lax.shift_right_logical(lax.shift_right_logical(bits, jnp.uint32(16)), jnp.uint32(16))
    return zero.astype(F32).astype(BF16)


def _dot_split(x, w):
    hi = x.astype(BF16)
    lo = (x - hi.astype(F32)).astype(BF16)
    return _dot(hi, w) + _dot(lo, w)


def _ada_kernel(c_ref, w_ref, b_ref, o_ref):
    c = c_ref[...]
    a = (c * jax.nn.sigmoid(c)).astype(BF16)
    o_ref[0] = _dot(a, w_ref[0].astype(BF16)) + b_ref[0]


def _ada_call(c_all, w_ada, b_ada):
    nb = c_all.shape[0]
    return pl.pallas_call(
        _ada_kernel,
        grid=(DEPTH, 6),
        in_specs=[
            pl.BlockSpec((nb, D_MODEL), lambda l, j: (0, 0)),
            pl.BlockSpec((1, D_MODEL, D_MODEL), lambda l, j: (l, 0, j)),
            pl.BlockSpec((1, 1, D_MODEL), lambda l, j: (l, 0, j)),
        ],
        out_specs=pl.BlockSpec((1, nb, D_MODEL), lambda l, j: (l, 0, j)),
        out_shape=jax.ShapeDtypeStruct((DEPTH, nb, 6 * D_MODEL), F32),
        compiler_params=_params(("arbitrary", "arbitrary")),
        name="ada",
    )(c_all, w_ada, b_ada.reshape(DEPTH, 1, 6 * D_MODEL))


def _proj_kernel(x_ref, mod_ref, gain_ref, w_ref, cos_ref, sin_ref,
                 ret_ref, u_ref, naq_ref, nak_ref, nav_ref, sq_ref, sk_ref, sv_ref):
    tm = x_ref.shape[1]
    W = BRANCH_WIDTH
    half = tm // PROJ_ROW_SPLITS
    lane = lax.broadcasted_iota(jnp.int32, (half, LANES), 1)
    first_half = (lane & (HEAD_DIM // 2)) == 0
    qscale = HEAD_DIM ** -0.5
    blocks = [slice(n * half, (n + 1) * half) for n in range(PROJ_ROW_SPLITS)]
    hs = [_modulated_rmsnorm(x_ref[0, rows, :], gain_ref[...], mod_ref[0, 1:2, :], mod_ref[0, 0:1, :]).astype(BF16)
          for rows in blocks]
    for rows, h in zip(blocks, hs):
        cos = cos_ref[rows, :]
        sin = sin_ref[rows, :]

        def rot(y):
            sw = jnp.where(first_half, pltpu.roll(y, LANES - HEAD_DIM // 2, 1), pltpu.roll(y, HEAD_DIM // 2, 1))
            return y * cos + sw * sin

        y = _dot(h, w_ref[:, RET_COL0:RET_COL0 + 4 * W])
        for t in range(W // LANES):
            ret_ref[0, rows, t * LANES:(t + 1) * LANES] = (rot(y[:, t * LANES:(t + 1) * LANES]) * qscale).astype(BF16)
        for t in range(W // LANES, 2 * W // LANES):
            ret_ref[0, rows, t * LANES:(t + 1) * LANES] = rot(y[:, t * LANES:(t + 1) * LANES]).astype(BF16)
        ret_ref[0, rows, 2 * W:4 * W] = y[:, 2 * W:4 * W].astype(BF16)
        y = _dot(h, w_ref[:, CONV_COL0:CONV_COL0 + 2 * W])
        u_ref[0, rows, :] = (y[:, 0:W] * jax.nn.sigmoid(y[:, W:2 * W])).astype(BF16)
        y = _dot(h, w_ref[:, NA_COL0:NA_COL0 + 3 * W])
        naq_ref[0, rows, :] = (y[:, 0:W] * (qscale * LOG2E)).astype(BF16)
        nak_ref[0, rows, :] = y[:, W:2 * W].astype(BF16)
        nav_ref[0, rows, :] = y[:, 2 * W:3 * W].astype(BF16)
        y = _dot(h, w_ref[:, SWA_COL0:IN_COLS])
        sq_ref[0, rows, :] = (y[:, 0:W] * (qscale * LOG2E)).astype(BF16)
        sk_ref[0, rows, :] = y[:, W:W + SWA_KV_WIDTH].astype(BF16)
        sv_ref[0, rows, :] = y[:, W + SWA_KV_WIDTH:W + 2 * SWA_KV_WIDTH].astype(BF16)


def _proj_call(x, mod, gain, w_in, cos_t, sin_t):
    B, T, _ = x.shape
    tm = TM_PROJ
    tok = lambda w: pl.BlockSpec((1, tm, w), lambda b, i: (b, i, 0))
    W = BRANCH_WIDTH
    widths = (4 * W, W, W, W, W, W, SWA_KV_WIDTH, SWA_KV_WIDTH)
    return pl.pallas_call(
        _proj_kernel,
        grid=(B, T // tm),
        in_specs=[
            tok(D_MODEL),
            pl.BlockSpec((1, 6, D_MODEL), lambda b, i: (b, 0, 0)),
            _resident((1, D_MODEL)),
            _resident((D_MODEL, IN_COLS)),
            pl.BlockSpec((tm, LANES), lambda b, i: (i, 0)),
            pl.BlockSpec((tm, LANES), lambda b, i: (i, 0)),
        ],
        out_specs=[tok(w) for w in widths],
        out_shape=[jax.ShapeDtypeStruct((B, T, w), BF16) for w in widths],
        compiler_params=_params(("arbitrary", "arbitrary")),
        name="proj",
    )(x, mod, gain, w_in, cos_t, sin_t)


def _ret_kernel(x_ref, dec_ref, xif_ref, xib_ref, zf_ref, zb_ref, gf_ref, gb_ref, bd_ref, avg_ref, gn_ref,
                o_ref, rf_ref, rb_ref, stash_ref):
    p = pl.program_id(1)
    n = pl.program_id(2)
    nsteps = pl.num_programs(2)
    C = RET_CHUNK
    W = BRANCH_WIDTH
    G = x_ref.shape[1] // C
    heads = W // HEAD_DIM

    def chunk_kv(j):
        rows = slice(j * C, (j + 1) * C)
        return x_ref[0, rows, W:2 * W], x_ref[0, rows, 2 * W:3 * W]

    def chunk_state(k, v, zeta):
        return _dot_tn((k.astype(F32) * zeta).astype(BF16), v) * bd_ref[...]

    @pl.when(p == 0)
    def _forward_states():
        @pl.when(n == 0)
        def _():
            rf_ref[...] = jnp.zeros_like(rf_ref)

        updates = [chunk_state(*chunk_kv(j), zf_ref[...]) for j in range(G)]
        rf = rf_ref[...]
        for j in range(G):
            stash_ref[n * G + j] = rf.astype(BF16)
            rf = gf_ref[...] * rf + updates[j]
        rf_ref[...] = rf

    @pl.when(p == 1)
    def _outputs():
        @pl.when(n == 0)
        def _():
            rb_ref[...] = jnp.zeros_like(rb_ref)

        first_chunk = (nsteps - 1 - n) * G
        lane_head = lax.broadcasted_iota(jnp.int32, (C, W), 1) // HEAD_DIM

        def head_rows(a):
            return jnp.concatenate([jnp.where(lane_head == hh, a, jnp.zeros_like(a)) for hh in range(heads)], axis=0)

        qs = [x_ref[0, j * C:(j + 1) * C, 0:W] for j in range(G)]
        weights = [(_dot_nt(qs[j], head_rows(chunk_kv(j)[0])) * dec_ref[...]).astype(BF16) for j in range(G)]
        updates = [chunk_state(*chunk_kv(j), zb_ref[...]) for j in range(G)]
        rb = rb_ref[...]
        later = [None] * G
        for j in reversed(range(G)):
            later[j] = rb.astype(BF16)
            rb = gb_ref[...] * rb + updates[j]
        rb_ref[...] = rb
        outs = []
        for j in range(G):
            qf = qs[j].astype(F32)
            o = _dot(weights[j], head_rows(chunk_kv(j)[1]))
            o = o + _dot((qf * xif_ref[...]).astype(BF16), stash_ref[first_chunk + j])
            outs.append(o + _dot((qf * xib_ref[...]).astype(BF16), later[j]))
        o = jnp.concatenate(outs, axis=0)
        avg = avg_ref[...]
        d = o - _dot_split(o, avg)
        var = _dot_split(d * d, avg)
        g = x_ref[0, :, 3 * W:4 * W].astype(F32)
        o_ref[0] = (d * lax.rsqrt(var + NORM_EPS) * gn_ref[...] * (g * jax.nn.sigmoid(g))).astype(BF16)


def _ret_tables(decay_logit):
    C = RET_CHUNK
    heads = BRANCH_WIDTH // HEAD_DIM
    lg = jax.nn.log_sigmoid(decay_logit.astype(F32))
    pos = np.arange(C, dtype=np.float32)
    diff = pos[:, None] - pos[None, :]
    fwd = jnp.exp(jnp.where(diff >= 0, diff[None] * lg[0][:, None, None], -jnp.inf))
    bwd = jnp.exp(jnp.where(diff < 0, -diff[None] * lg[1][:, None, None], -jnp.inf))
    dec = (fwd + bwd).transpose(1, 0, 2).reshape(C, heads * C)
    lane_lg = jnp.repeat(lg, HEAD_DIM, axis=1)
    xif = jnp.exp((pos + 1.0)[:, None] * lane_lg[0][None, :])
    xib = jnp.exp((C - pos)[:, None] * lane_lg[1][None, :])
    zf = jnp.exp((C - 1.0 - pos)[:, None] * lane_lg[0][None, :])
    zb = jnp.exp(pos[:, None] * lane_lg[1][None, :])
    gf = jnp.exp(C * lane_lg[0])[None, :]
    gb = jnp.exp(C * lane_lg[1])[None, :]
    return dec, xif, xib, zf, zb, gf, gb


def _head_block_constants():
    hid = np.arange(BRANCH_WIDTH) // HEAD_DIM
    same = (hid[:, None] == hid[None, :])
    return jnp.asarray(same.astype(np.float32)), jnp.asarray(same.astype(np.float32) / HEAD_DIM, dtype=BF16)


def _ret_call(ret_in, tables, gn_gain):
    B, T, _ = ret_in.shape
    rows = RET_CHUNK * RET_CHUNKS_PER_STEP
    N = T // RET_CHUNK
    S = T // rows
    bd, avg = _head_block_constants()
    step_rows = lambda b, p, n: (b, jnp.where(p == 0, n, S - 1 - n), 0)
    out_rows = lambda b, p, n: (b, jnp.where(p == 0, S - 1, S - 1 - n), 0)
    consts = list(tables) + [bd, avg, gn_gain.reshape(1, BRANCH_WIDTH).astype(F32)]
    return pl.pallas_call(
        _ret_kernel,
        grid=(B, 2, S),
        in_specs=[pl.BlockSpec((1, rows, 4 * BRANCH_WIDTH), step_rows)] + [_resident(t.shape) for t in consts],
        out_specs=pl.BlockSpec((1, rows, BRANCH_WIDTH), out_rows),
        out_shape=jax.ShapeDtypeStruct((B, T, BRANCH_WIDTH), BF16),
        scratch_shapes=[
            pltpu.VMEM((BRANCH_WIDTH, BRANCH_WIDTH), F32),
            pltpu.VMEM((BRANCH_WIDTH, BRANCH_WIDTH), F32),
            pltpu.VMEM((N, BRANCH_WIDTH, BRANCH_WIDTH), BF16),
        ],
        compiler_params=_params(("arbitrary", "arbitrary", "arbitrary")),
        name="ret",
    )(ret_in, *consts)


def _conv_fill_window(u_ref, win_ref, i, nt, tt):
    T = u_ref.shape[1]
    t0 = pl.multiple_of(i * tt, tt)
    left_start = pl.multiple_of(jnp.maximum(t0 - CONV_HALO, 0), CONV_HALO)
    right_start = pl.multiple_of(jnp.minimum(t0 + tt, T - CONV_HALO), CONV_HALO)
    left = u_ref[0, pl.ds(left_start, CONV_HALO), :].astype(F32) * (i > 0).astype(F32)
    right = u_ref[0, pl.ds(right_start, CONV_HALO), :].astype(F32) * (i < nt - 1).astype(F32)
    win = jnp.concatenate([left, u_ref[0, pl.ds(t0, tt), :].astype(F32), right], axis=0)
    span = win_ref.shape[1]
    for s in range(SUBLANES):
        win_ref[s] = win[s:s + span]


def _conv_rows(win_ref, w_ref, b_ref, lg_ref, lb_ref, r):
    off = CONV_HALO - CONV_WIDTH // 2
    acc = jnp.zeros((CONV_ROWS // SUBLANES, SUBLANES, BRANCH_WIDTH), F32)
    for tap in range(CONV_WIDTH):
        s, a = (tap + off) % SUBLANES, (tap + off) // SUBLANES * SUBLANES
        rows = win_ref[s, r + a:r + a + CONV_ROWS, :].reshape(acc.shape)
        acc = acc + rows * w_ref[tap][None]
    acc = acc.reshape(CONV_ROWS, BRANCH_WIDTH) + b_ref[...]
    mu = jnp.mean(acc, axis=-1, keepdims=True)
    d = acc - mu
    var = jnp.mean(d * d, axis=-1, keepdims=True)
    y = d * lax.rsqrt(var + NORM_EPS) * lg_ref[...] + lb_ref[...]
    return (y * jax.nn.sigmoid(y)).astype(BF16)


def _na_layout(rows):
    R = NA_QROWS
    KR = R + NA_WIN_ROWS
    nblk = rows // R
    qc = np.arange(GRID_W)
    ws = np.clip(qc - NA_WIN_COLS // 2, 0, GRID_W - NA_WIN_COLS)
    col_ok = (qc[None, :] >= ws[:, None]) & (qc[None, :] < ws[:, None] + NA_WIN_COLS)
    dc = qc[None, :] - qc[:, None] + NA_WIN_COLS - 1
    n_dr = 2 * NA_WIN_ROWS - 1
    variants, var_of_block, kr0s = [], [], []
    for j in range(nblk):
        r0 = j * R
        kr0 = int(np.clip(r0 - NA_WIN_ROWS // 2, 0, rows - KR))
        r = r0 + np.arange(R)
        start = np.clip(r - NA_WIN_ROWS // 2, 0, rows - NA_WIN_ROWS)
        kr = kr0 + np.arange(KR)
        row_ok = (kr[None, :] >= start[:, None]) & (kr[None, :] < start[:, None] + NA_WIN_ROWS)
        tile = np.where(row_ok, kr[None, :] - r[:, None] + NA_WIN_ROWS - 1, n_dr)
        key = tile.tobytes()
        if key not in [v[0] for v in variants]:
            variants.append((key, tile))
        var_of_block.append([v[0] for v in variants].index(key))
        kr0s.append(kr0)
    tiles = np.stack([v[1] for v in variants])
    onehot = (dc[None] == np.arange(2 * NA_WIN_COLS - 1)[:, None, None]) & col_ok[None]
    return tiles, onehot, col_ok, np.asarray(var_of_block, np.int32), np.asarray(kr0s, np.int32)


def _na_bias_table(rpb, rows):
    tiles, onehot, col_ok, _, _ = _na_layout(rows)
    H, n_dr, n_dc = rpb.shape
    oh = jnp.asarray(onehot.reshape(n_dc, GRID_W * GRID_W), F32)
    tz = jnp.dot(rpb.astype(F32).reshape(H * n_dr, n_dc), oh, precision=lax.Precision.HIGHEST)
    tz = jnp.where(col_ok[None, None], tz.reshape(H, n_dr, GRID_W, GRID_W) * LOG2E, NEG_INF)
    tz = jnp.concatenate([tz, jnp.full((H, 1, GRID_W, GRID_W), NEG_INF, F32)], axis=1)
    nv, R, KR = tiles.shape
    per_variant = []
    for vv in range(nv):
        slabs = [jnp.concatenate([tz[:, int(tiles[vv, rl, kl])] for kl in range(KR)], axis=-1) for rl in range(R)]
        per_variant.append(jnp.concatenate(slabs, axis=1))
    return jnp.stack(per_variant)


def _na_bias_tables(rpb, row_counts):
    by_layout, out = {}, {}
    for rows in row_counts:
        key = _na_layout(rows)[0].tobytes()
        if key not in by_layout:
            by_layout[key] = _na_bias_table(rpb, rows)
        out[rows] = by_layout[key]
    return out


def _na_kernel(var_ref, kr0_ref, q_ref, k_ref, v_ref, *rest):
    tab_refs, o_ref = rest[:-1], rest[-1]
    step = pl.program_id(1)
    nq = q_ref.shape[1] // NA_BLOCKS_PER_STEP
    nk = tab_refs[0].shape[3]
    low = lax.broadcasted_iota(jnp.int32, (nq, LANES), 1) < HEAD_DIM
    units = [(sb, pair) for sb in range(len(tab_refs)) for pair in range(BRANCH_WIDTH // LANES)]
    kstart = [pl.multiple_of(kr0_ref[step * NA_BLOCKS_PER_STEP + sb] * GRID_W, GRID_W) for sb in range(len(tab_refs))]
    scores = []
    for sb, pair in units:
        cols = slice(pair * LANES, (pair + 1) * LANES)
        q2 = q_ref[0, sb * nq:(sb + 1) * nq, cols]
        zero = jnp.zeros_like(q2)
        qs = jnp.concatenate([jnp.where(low, q2, zero), jnp.where(low, zero, q2)], axis=0)
        scores.append(_dot_nt(qs, k_ref[0, pl.ds(kstart[sb], nk), cols]) + tab_refs[sb][0, pair])
    weights = [jnp.exp2(s - jnp.max(s, axis=-1, keepdims=True)).astype(BF16) for s in scores]
    ones = jnp.ones((nk, LANES), BF16)
    for (sb, pair), e in zip(units, weights):
        cols = slice(pair * LANES, (pair + 1) * LANES)
        o2 = _dot(e, jnp.concatenate([v_ref[0, pl.ds(kstart[sb], nk), cols], ones], axis=1))
        o2 = o2[:, :LANES] * (1.0 / o2[:, LANES:])
        o_ref[0, sb * nq:(sb + 1) * nq, cols] = jnp.where(low, o2[:nq], o2[nq:]).astype(BF16)


def _na_call(q, k, v, tab):
    B, T, _ = q.shape
    rows = T // GRID_W
    _, _, _, var_of_block, kr0s = _na_layout(rows)
    nq = NA_QROWS * GRID_W
    per_step = NA_BLOCKS_PER_STEP
    nblk = rows // NA_QROWS
    assert nblk % per_step == 0
    nv, heads, _, nk = tab.shape
    tab = tab.reshape(nv, heads // 2, 2 * nq, nk)
    tab_spec = lambda sb: pl.BlockSpec((1, heads // 2, 2 * nq, nk),
                                       lambda b, j, var, kr0: (var[j * per_step + sb], 0, 0, 0))
    grid_spec = pltpu.PrefetchScalarGridSpec(
        num_scalar_prefetch=2,
        grid=(B, nblk // per_step),
        in_specs=[
            pl.BlockSpec((1, per_step * nq, BRANCH_WIDTH), lambda b, j, var, kr0: (b, j, 0)),
            pl.BlockSpec((1, T, BRANCH_WIDTH), lambda b, j, var, kr0: (b, 0, 0)),
            pl.BlockSpec((1, T, BRANCH_WIDTH), lambda b, j, var, kr0: (b, 0, 0)),
        ] + [tab_spec(sb) for sb in range(per_step)],
        out_specs=pl.BlockSpec((1, per_step * nq, BRANCH_WIDTH), lambda b, j, var, kr0: (b, j, 0)),
    )
    return pl.pallas_call(
        _na_kernel,
        grid_spec=grid_spec,
        out_shape=jax.ShapeDtypeStruct((B, T, BRANCH_WIDTH), BF16),
        compiler_params=_params(("arbitrary", "arbitrary")),
        name="na",
    )(jnp.asarray(var_of_block), jnp.asarray(kr0s), q, k, v, *([tab] * per_step))


def _t5_bucket(rel):
    half = T5_BUCKETS // 2
    exact = half // 2
    n = np.abs(rel)
    large = exact + (np.log(np.maximum(n, 1) / exact) / math.log(T5_MAX_DIST / exact) * (half - exact)).astype(np.int64)
    large = np.minimum(large, half - 1)
    return (rel > 0).astype(np.int64) * half + np.where(n < exact, n, large)


def _swa_bias_table(t5_bias):
    kpos = np.arange(3 * SWA_BLOCK) - SWA_BLOCK
    rel = kpos[None, :] - np.arange(SWA_BLOCK)[:, None]
    onehot = (_t5_bucket(rel)[None] == np.arange(T5_BUCKETS)[:, None, None])
    oh = jnp.asarray(onehot.reshape(T5_BUCKETS, -1), F32)
    tab = jnp.dot(t5_bias.astype(F32).T, oh, precision=lax.Precision.HIGHEST).reshape((-1,) + rel.shape)
    tab = jnp.where((np.abs(rel) <= SWA_WINDOW)[None], tab * LOG2E, NEG_INF)
    before = (kpos < 0)[None, None, :]
    after = (kpos >= SWA_BLOCK)[None, None, :]
    tabs = jnp.stack([jnp.where(before, NEG_INF, tab), tab, jnp.where(after, NEG_INF, tab)])
    hq = tab.shape[0]
    return tabs.reshape(3, hq // 2, 2 * SWA_BLOCK, 3 * SWA_BLOCK)


def _swa_kernel(sink_ref, q_ref, k_ref, v_ref, tab_ref, o_ref):
    n = pl.program_id(1)
    tq = q_ref.shape[1]
    T = k_ref.shape[1]
    nb = T // SWA_BLOCK
    per_step = tq // SWA_BLOCK
    low = lax.broadcasted_iota(jnp.int32, (SWA_BLOCK, LANES), 1) < HEAD_DIM
    first_rows = lax.broadcasted_iota(jnp.int32, (2 * SWA_BLOCK, 1), 0) < SWA_BLOCK
    units = [(i, hk) for i in range(per_step) for hk in range(2)]

    def band(ref, blk):
        prev_start = pl.multiple_of(jnp.maximum(blk - 1, 0) * SWA_BLOCK, SWA_BLOCK)
        cur_start = pl.multiple_of(blk * SWA_BLOCK, SWA_BLOCK)
        next_start = pl.multiple_of(jnp.minimum(blk + 1, nb - 1) * SWA_BLOCK, SWA_BLOCK)
        return jnp.concatenate([ref[0, pl.ds(prev_start, SWA_BLOCK), :],
                                ref[0, pl.ds(cur_start, SWA_BLOCK), :],
                                ref[0, pl.ds(next_start, SWA_BLOCK), :]], axis=0)

    scores = []
    for i, hk in units:
        blk = n * per_step + i
        variant = jnp.where(blk == 0, 0, jnp.where(blk == nb - 1, 2, 1))
        q2 = q_ref[0, i * SWA_BLOCK:(i + 1) * SWA_BLOCK, hk * LANES:(hk + 1) * LANES]
        q2r = pltpu.roll(q2, HEAD_DIM, 1)
        zero = jnp.zeros_like(q2)
        if hk == 0:
            qs = jnp.concatenate([jnp.where(low, q2, zero), jnp.where(low, q2r, zero)], axis=0)
        else:
            qs = jnp.concatenate([jnp.where(low, zero, q2r), jnp.where(low, zero, q2)], axis=0)
        scores.append(_dot_nt(qs, band(k_ref, blk)) + tab_ref[variant, hk])
    weights = []
    for (i, hk), s in zip(units, scores):
        sink = jnp.where(first_rows, sink_ref[2 * hk], sink_ref[2 * hk + 1]) * LOG2E
        m = jnp.maximum(jnp.max(s, axis=-1, keepdims=True), sink)
        weights.append((jnp.exp2(s - m).astype(BF16), jnp.exp2(sink - m)))
    ones = jnp.ones((3 * SWA_BLOCK, LANES), BF16)
    for (i, hk), (e, sink_weight) in zip(units, weights):
        o2 = _dot(e, jnp.concatenate([band(v_ref, n * per_step + i), ones], axis=1))
        o2 = o2[:, :LANES] * (1.0 / (o2[:, LANES:] + sink_weight))
        top, bot = o2[:SWA_BLOCK], o2[SWA_BLOCK:]
        if hk == 0:
            out = jnp.where(low, top, pltpu.roll(bot, HEAD_DIM, 1))
        else:
            out = jnp.where(low, pltpu.roll(top, HEAD_DIM, 1), bot)
        o_ref[0, i * SWA_BLOCK:(i + 1) * SWA_BLOCK, hk * LANES:(hk + 1) * LANES] = out.astype(BF16)


def _swa_call(q, k, v, tab, sink):
    B, T, _ = q.shape
    tq = TQ_SWA
    assert T // SWA_BLOCK >= 2
    return pl.pallas_call(
        _swa_kernel,
        grid=(B, T // tq),
        in_specs=[
            pl.BlockSpec(memory_space=pltpu.SMEM),
            pl.BlockSpec((1, tq, BRANCH_WIDTH), lambda b, i: (b, i, 0)),
            pl.BlockSpec((1, T, LANES), lambda b, i: (b, 0, 0)),
            pl.BlockSpec((1, T, LANES), lambda b, i: (b, 0, 0)),
            _resident(tab.shape),
        ],
        out_specs=pl.BlockSpec((1, tq, BRANCH_WIDTH), lambda b, i: (b, i, 0)),
        out_shape=jax.ShapeDtypeStruct((B, T, BRANCH_WIDTH), BF16),
        compiler_params=_params(("arbitrary", "arbitrary")),
        name="swa",
    )(sink.astype(F32), q, k, v, tab)


def _merge_kernel(x_ref, mod_ref, gain_ref, ret_ref, u_ref, na_ref, swa_ref, cw_ref, cb_ref, clg_ref, clb_ref,
                  wm_ref, wb_ref, wo_ref, o_ref, win_ref, conv_ref):
    tm = x_ref.shape[1]
    x = x_ref[0]
    h = _modulated_rmsnorm(x, gain_ref[...], mod_ref[0, 1:2, :], mod_ref[0, 0:1, :]).astype(BF16)
    _conv_fill_window(u_ref, win_ref, pl.program_id(1), pl.num_programs(1), tm)
    branch_refs = {0: ret_ref, 2: na_ref, 3: swa_ref}
    width = D_MODEL // MERGE_COL_SPLITS
    col_blocks = [slice(c * width, (c + 1) * width) for c in range(MERGE_COL_SPLITS)]
    slots = [(i, c) for c in range(MERGE_COL_SPLITS) for i in (0, 2, 3, 1)]
    groups = list(range(0, tm, CONV_ROWS))
    after_slot = [[] for _ in slots]
    for g, r in enumerate(groups):
        after_slot[1 + g * (len(slots) - 1) // len(groups)].append(r)
    merged = [None] * MERGE_COL_SPLITS
    conv_gate = [None] * MERGE_COL_SPLITS
    tie = None
    for n, (i, c) in enumerate(slots):
        gate = jax.nn.sigmoid(_dot(h, wm_ref[i, :, col_blocks[c]]))
        if i in branch_refs:
            br = branch_refs[i][0]
            term = gate * _dot(br if tie is None else br + tie, wb_ref[i, :, col_blocks[c]])
            merged[c] = term if merged[c] is None else merged[c] + term
        else:
            conv_gate[c] = gate
        for r in after_slot[n]:
            rows = _conv_rows(win_ref, cw_ref, cb_ref, clg_ref, clb_ref, r)
            conv_ref[r:r + CONV_ROWS, :] = rows
            tie = _zero_tied_to(rows)
    merged = jnp.concatenate(merged, axis=1) + jnp.concatenate(conv_gate, axis=1) * _dot(conv_ref[...], wb_ref[1])
    o_ref[0] = x + mod_ref[0, 2:3, :] * _dot(merged.astype(BF16), wo_ref[...])


def _merge_call(x, mod, gain, ret, u, na, swa, conv_params, w_merge, w_branch, w_out):
    B, T, _ = x.shape
    tm = TM_MERGE
    tok = lambda w: pl.BlockSpec((1, tm, w), lambda b, i: (b, i, 0))
    whole = pl.BlockSpec(memory_space=pltpu.VMEM)
    dw_kernel, dw_bias, ln_gain, ln_bias = conv_params
    row = lambda a: a.reshape(1, BRANCH_WIDTH).astype(F32)
    taps = jnp.broadcast_to(dw_kernel.astype(F32)[:, None, :], (CONV_WIDTH, SUBLANES, BRANCH_WIDTH))
    return pl.pallas_call(
        _merge_kernel,
        grid=(B, T // tm),
        in_specs=[tok(D_MODEL), pl.BlockSpec((1, 6, D_MODEL), lambda b, i: (b, 0, 0)), _resident((1, D_MODEL)),
                  tok(BRANCH_WIDTH), pl.BlockSpec((1, T, BRANCH_WIDTH), lambda b, i: (b, 0, 0)),
                  tok(BRANCH_WIDTH), tok(BRANCH_WIDTH),
                  _resident((CONV_WIDTH, SUBLANES, BRANCH_WIDTH)),
                  _resident((1, BRANCH_WIDTH)), _resident((1, BRANCH_WIDTH)), _resident((1, BRANCH_WIDTH)),
                  whole, whole, whole],
        out_specs=tok(D_MODEL),
        out_shape=jax.ShapeDtypeStruct((B, T, D_MODEL), F32),
        scratch_shapes=[pltpu.VMEM((SUBLANES, tm + 2 * CONV_HALO - SUBLANES, BRANCH_WIDTH), F32),
                        pltpu.VMEM((tm, BRANCH_WIDTH), BF16)],
        compiler_params=_params(("arbitrary", "arbitrary")),
        name="merge",
    )(x, mod, gain, ret, u, na, swa, taps, row(dw_bias), row(ln_gain), row(ln_bias), w_merge, w_branch, w_out)


def _ffn_kernel(x_ref, mod_ref, gain_ref, fin_ref, w1_ref, w2_ref, o_ref, *, final_norm):
    x = x_ref[0]
    h = _modulated_rmsnorm(x, gain_ref[...], mod_ref[0, 4:5, :], mod_ref[0, 3:4, :]).astype(BF16)
    ff = None
    for c in range(0, D_FF, FFN_CHUNK):
        a = jnp.maximum(_dot(h, w1_ref[:, c:c + FFN_CHUNK]), 0.0)
        part = _dot((a * a).astype(BF16), w2_ref[c:c + FFN_CHUNK, :])
        ff = part if ff is None else ff + part
    y = x + mod_ref[0, 5:6, :] * ff
    if final_norm:
        y = y * lax.rsqrt(jnp.mean(y * y, axis=-1, keepdims=True) + NORM_EPS) * fin_ref[...]
    o_ref[0] = y


def _ffn_call(x, mod, gain, final_gain, w1, w2, final_norm):
    B, T, _ = x.shape
    tm = TM_FFN
    tok = pl.BlockSpec((1, tm, D_MODEL), lambda b, i: (b, i, 0))
    whole = pl.BlockSpec(memory_space=pltpu.VMEM)
    return pl.pallas_call(
        functools.partial(_ffn_kernel, final_norm=final_norm),
        grid=(B, T // tm),
        in_specs=[tok, pl.BlockSpec((1, 6, D_MODEL), lambda b, i: (b, 0, 0)), _resident((1, D_MODEL)),
                  _resident((1, D_MODEL)), whole, whole],
        out_specs=tok,
        out_shape=jax.ShapeDtypeStruct((B, T, D_MODEL), F32),
        compiler_params=_params(("arbitrary", "arbitrary")),
        name="ffn",
    )(x, mod, gain, final_gain, w1, w2)


def _rotary_tables(T):
    half = HEAD_DIM // 2
    inv = (ROPE_BASE ** (-np.arange(half, dtype=np.float32) / half)).astype(np.float32)
    ang = (np.arange(T, dtype=np.float32)[:, None] * inv[None, :]).astype(np.float32).astype(np.float64)
    lane = np.arange(LANES) % HEAD_DIM
    cos = np.cos(ang)[:, lane % half]
    sin = np.sin(ang)[:, lane % half] * np.where(lane < half, -1.0, 1.0)[None, :]
    return jnp.asarray(cos, F32), jnp.asarray(sin, F32)


def _trunk(x, mods, layer_params, shared):
    B, T, _ = x.shape
    cos_t, sin_t = _rotary_tables(T)
    rows = T // GRID_W
    for l, lp in enumerate(layer_params):
        mod = mods[l]
        ret_in, u, naq, nak, nav, sq, sk, sv = _proj_call(x, mod, lp["gain1"], lp["w_in"], cos_t, sin_t)
        ret = _ret_call(ret_in, lp["ret_tables"], lp["ret_gn_gain"])
        na = _na_call(naq, nak, nav, lp["na_tables"][rows])
        swa = _swa_call(sq, sk, sv, shared["swa_table"], lp["swa_sink"])
        conv_params = (lp["conv_dw_kernel"], lp["conv_dw_bias"], lp["conv_ln_gain"], lp["conv_ln_bias"])
        x = _merge_call(x, mod, lp["gain1"], ret, u, na, swa, conv_params, lp["w_merge"], lp["w_branch"], lp["w_out"])
        x = _ffn_call(x, mod, lp["gain2"], shared["final_gain"], lp["w_ff1"], lp["w_ff2"], l == len(layer_params) - 1)
    return x


def kernel(x_prompt, x_sample, c_prompt, c_sample, w_ada, b_ada, norm_gain, w_in, ret_decay_logit, ret_gn_gain,
           conv_dw_kernel, conv_dw_bias, conv_ln_gain, conv_ln_bias, na_rpb, swa_sink, t5_bias,
           w_branch, w_merge, w_out, w_ff1, w_ff2, final_gain):
    nbp = c_prompt.shape[0]
    c_all = jnp.concatenate([c_prompt, c_sample], axis=0)
    mods = _ada_call(c_all, w_ada, b_ada.astype(F32))
    mods = mods.reshape(DEPTH, c_all.shape[0], 6, D_MODEL)
    row_counts = sorted({x_prompt.shape[1] // GRID_W, x_sample.shape[1] // GRID_W})
    layer_params = []
    for l in range(DEPTH):
        layer_params.append(dict(
            gain1=norm_gain[l, 0].reshape(1, D_MODEL).astype(F32),
            gain2=norm_gain[l, 1].reshape(1, D_MODEL).astype(F32),
            w_in=w_in[l].astype(BF16),
            ret_tables=_ret_tables(ret_decay_logit[l]),
            ret_gn_gain=ret_gn_gain[l],
            conv_dw_kernel=conv_dw_kernel[l], conv_dw_bias=conv_dw_bias[l],
            conv_ln_gain=conv_ln_gain[l], conv_ln_bias=conv_ln_bias[l],
            na_tables=_na_bias_tables(na_rpb[l], row_counts),
            swa_sink=swa_sink[l],
            w_merge=w_merge[l].astype(BF16), w_branch=w_branch[l].astype(BF16), w_out=w_out[l].astype(BF16),
            w_ff1=w_ff1[l].astype(BF16), w_ff2=w_ff2[l].astype(BF16),
        ))
    shared = dict(swa_table=_swa_bias_table(t5_bias), final_gain=final_gain.reshape(1, D_MODEL).astype(F32))
    y_prompt = _trunk(x_prompt, [m[:nbp] for m in mods], layer_params, shared)
    y_sample = _trunk(x_sample, [m[nbp:] for m in mods], layer_params, shared)
    return (y_prompt, y_sample)
```

```python
import functools
import math

import jax
import jax.numpy as jnp
import numpy as np
from jax import lax
from jax.experimental import pallas as pl
from jax.experimental.pallas import tpu as pltpu

F32 = jnp.float32
BF16 = jnp.bfloat16

D_MODEL = 1024
DEPTH = 2
HEAD_DIM = 64
BRANCH_WIDTH = 256
N_BRANCHES = 4
RET_CHUNK = 128
RET_CHUNKS_PER_STEP = 16
ROPE_BASE = 10000.0
CONV_WIDTH = 31
CONV_HALO = 16
GRID_W = 64
NA_WIN_ROWS = 8
NA_WIN_COLS = 16
NA_QROWS = 2
NA_BLOCKS_PER_STEP = 16
SWA_WINDOW = 128
SWA_BLOCK = 128
T5_BUCKETS = 32
T5_MAX_DIST = 128
D_FF = 4 * D_MODEL
NORM_EPS = 1e-6
NEG_INF = -1e30
LOG2E = 1.4426950408889634
SWA_KV_WIDTH = 128
RET_COL0 = 0
CONV_COL0 = RET_COL0 + 4 * BRANCH_WIDTH
NA_COL0 = CONV_COL0 + 2 * BRANCH_WIDTH
SWA_COL0 = NA_COL0 + 3 * BRANCH_WIDTH
IN_COLS = SWA_COL0 + BRANCH_WIDTH + 2 * SWA_KV_WIDTH
LANES = 128
SUBLANES = 8

TM_PROJ = 1024
PROJ_ROW_SPLITS = 4
TM_MERGE = 512
MERGE_COL_SPLITS = 4
TM_FFN = 1024
FFN_CHUNK = 1024
CONV_ROWS = 32
TQ_SWA = 2048
VMEM_LIMIT = 56 * 1024 * 1024


def _params(sem, flags=None):
    return pltpu.CompilerParams(dimension_semantics=sem, vmem_limit_bytes=VMEM_LIMIT, flags=flags)


def _resident(shape):
    nd = len(shape)
    return pl.BlockSpec(shape, lambda *_: (0,) * nd)


def _modulated_rmsnorm(x, gain, scale, shift):
    y = x * lax.rsqrt(jnp.mean(x * x, axis=-1, keepdims=True) + NORM_EPS)
    return (y * gain) * (1.0 + scale) + shift


def _dot(a, b):
    return jnp.dot(a, b, preferred_element_type=F32)


def _dot_nt(a, b):
    return lax.dot_general(a, b, (((1,), (1,)), ((), ())), preferred_element_type=F32)


def _dot_tn(a, b):
    return lax.dot_general(a, b, (((0,), (0,)), ((), ())), preferred_element_type=F32)


def _zero_tied_to(v):
    bits = lax.bitcast_convert_type(v[0:1, :].astype(F32), jnp.uint32)
    zero = lax.shift_right_logical(lax.shift_right_logical(bits, jnp.uint32(16)), jnp.uint32(16))
    return zero.astype(F32).astype(BF16)


def _dot_split(x, w):
    hi = x.astype(BF16)
    lo = (x - hi.astype(F32)).astype(BF16)
    return _dot(hi, w) + _dot(lo, w)


def _ada_kernel(c_ref, w_ref, b_ref, o_ref):
    c = c_ref[...]
    a = (c * jax.nn.sigmoid(c)).astype(BF16)
    o_ref[0] = _dot(a, w_ref[0].astype(BF16)) + b_ref[0]


def _ada_call(c_all, w_ada, b_ada):
    nb = c_all.shape[0]
    return pl.pallas_call(
        _ada_kernel,
        grid=(DEPTH, 6),
        in_specs=[
            pl.BlockSpec((nb, D_MODEL), lambda l, j: (0, 0)),
            pl.BlockSpec((1, D_MODEL, D_MODEL), lambda l, j: (l, 0, j)),
            pl.BlockSpec((1, 1, D_MODEL), lambda l, j: (l, 0, j)),
        ],
        out_specs=pl.BlockSpec((1, nb, D_MODEL), lambda l, j: (l, 0, j)),
        out_shape=jax.ShapeDtypeStruct((DEPTH, nb, 6 * D_MODEL), F32),
        compiler_params=_params(("arbitrary", "arbitrary")),
        name="ada",
    )(c_all, w_ada, b_ada.reshape(DEPTH, 1, 6 * D_MODEL))


def _proj_kernel(x_ref, mod_ref, gain_ref, w_ref, cos_ref, sin_ref,
                 ret_ref, u_ref, naq_ref, nak_ref, nav_ref, sq_ref, sk_ref, sv_ref):
    tm = x_ref.shape[1]
    W = BRANCH_WIDTH
    half = tm // PROJ_ROW_SPLITS
    lane = lax.broadcasted_iota(jnp.int32, (half, LANES), 1)
    first_half = (lane & (HEAD_DIM // 2)) == 0
    qscale = HEAD_DIM ** -0.5
    blocks = [slice(n * half, (n + 1) * half) for n in range(PROJ_ROW_SPLITS)]
    hs = [_modulated_rmsnorm(x_ref[0, rows, :], gain_ref[...], mod_ref[0, 1:2, :], mod_ref[0, 0:1, :]).astype(BF16)
          for rows in blocks]
    for rows, h in zip(blocks, hs):
        cos = cos_ref[rows, :]
        sin = sin_ref[rows, :]

        def rot(y):
            sw = jnp.where(first_half, pltpu.roll(y, LANES - HEAD_DIM // 2, 1), pltpu.roll(y, HEAD_DIM // 2, 1))
            return y * cos + sw * sin

        y = _dot(h, w_ref[:, RET_COL0:RET_COL0 + 4 * W])
        for t in range(W // LANES):
            ret_ref[0, rows, t * LANES:(t + 1) * LANES] = (rot(y[:, t * LANES:(t + 1) * LANES]) * qscale).astype(BF16)
        for t in range(W // LANES, 2 * W // LANES):
            ret_ref[0, rows, t * LANES:(t + 1) * LANES] = rot(y[:, t * LANES:(t + 1) * LANES]).astype(BF16)
        ret_ref[0, rows, 2 * W:4 * W] = y[:, 2 * W:4 * W].astype(BF16)
        y = _dot(h, w_ref[:, CONV_COL0:CONV_COL0 + 2 * W])
        u_ref[0, rows, :] = (y[:, 0:W] * jax.nn.sigmoid(y[:, W:2 * W])).astype(BF16)
        y = _dot(h, w_ref[:, NA_COL0:NA_COL0 + 3 * W])
        naq_ref[0, rows, :] = (y[:, 0:W] * (qscale * LOG2E)).astype(BF16)
        nak_ref[0, rows, :] = y[:, W:2 * W].astype(BF16)
        nav_ref[0, rows, :] = y[:, 2 * W:3 * W].astype(BF16)
        y = _dot(h, w_ref[:, SWA_COL0:IN_COLS])
        sq_ref[0, rows, :] = (y[:, 0:W] * (qscale * LOG2E)).astype(BF16)
        sk_ref[0, rows, :] = y[:, W:W + SWA_KV_WIDTH].astype(BF16)
        sv_ref[0, rows, :] = y[:, W + SWA_KV_WIDTH:W + 2 * SWA_KV_WIDTH].astype(BF16)


def _proj_call(x, mod, gain, w_in, cos_t, sin_t):
    B, T, _ = x.shape
    tm = TM_PROJ
    tok = lambda w: pl.BlockSpec((1, tm, w), lambda b, i: (b, i, 0))
    W = BRANCH_WIDTH
    widths = (4 * W, W, W, W, W, W, SWA_KV_WIDTH, SWA_KV_WIDTH)
    return pl.pallas_call(
        _proj_kernel,
        grid=(B, T // tm),
        in_specs=[
            tok(D_MODEL),
            pl.BlockSpec((1, 6, D_MODEL), lambda b, i: (b, 0, 0)),
            _resident((1, D_MODEL)),
            _resident((D_MODEL, IN_COLS)),
            pl.BlockSpec((tm, LANES), lambda b, i: (i, 0)),
            pl.BlockSpec((tm, LANES), lambda b, i: (i, 0)),
        ],
        out_specs=[tok(w) for w in widths],
        out_shape=[jax.ShapeDtypeStruct((B, T, w), BF16) for w in widths],
        compiler_params=_params(("arbitrary", "arbitrary")),
        name="proj",
    )(x, mod, gain, w_in, cos_t, sin_t)


def _ret_kernel(x_ref, dec_ref, xif_ref, xib_ref, zf_ref, zb_ref, gf_ref, gb_ref, bd_ref, avg_ref, gn_ref,
                o_ref, rf_ref, rb_ref, stash_ref):
    p = pl.program_id(1)
    n = pl.program_id(2)
    nsteps = pl.num_programs(2)
    C = RET_CHUNK
    W = BRANCH_WIDTH
    G = x_ref.shape[1] // C
    heads = W // HEAD_DIM

    def chunk_kv(j):
        rows = slice(j * C, (j + 1) * C)
        return x_ref[0, rows, W:2 * W], x_ref[0, rows, 2 * W:3 * W]

    def chunk_state(k, v, zeta):
        return _dot_tn((k.astype(F32) * zeta).astype(BF16), v) * bd_ref[...]

    @pl.when(p == 0)
    def _forward_states():
        @pl.when(n == 0)
        def _():
            rf_ref[...] = jnp.zeros_like(rf_ref)

        updates = [chunk_state(*chunk_kv(j), zf_ref[...]) for j in range(G)]
        rf = rf_ref[...]
        for j in range(G):
            stash_ref[n * G + j] = rf.astype(BF16)
            rf = gf_ref[...] * rf + updates[j]
        rf_ref[...] = rf

    @pl.when(p == 1)
    def _outputs():
        @pl.when(n == 0)
        def _():
            rb_ref[...] = jnp.zeros_like(rb_ref)

        first_chunk = (nsteps - 1 - n) * G
        lane_head = lax.broadcasted_iota(jnp.int32, (C, W), 1) // HEAD_DIM

        def head_rows(a):
            return jnp.concatenate([jnp.where(lane_head == hh, a, jnp.zeros_like(a)) for hh in range(heads)], axis=0)

        qs = [x_ref[0, j * C:(j + 1) * C, 0:W] for j in range(G)]
        weights = [(_dot_nt(qs[j], head_rows(chunk_kv(j)[0])) * dec_ref[...]).astype(BF16) for j in range(G)]
        updates = [chunk_state(*chunk_kv(j), zb_ref[...]) for j in range(G)]
        rb = rb_ref[...]
        later = [None] * G
        for j in reversed(range(G)):
            later[j] = rb.astype(BF16)
            rb = gb_ref[...] * rb + updates[j]
        rb_ref[...] = rb
        outs = []
        for j in range(G):
            qf = qs[j].astype(F32)
            o = _dot(weights[j], head_rows(chunk_kv(j)[1]))
            o = o + _dot((qf * xif_ref[...]).astype(BF16), stash_ref[first_chunk + j])
            outs.append(o + _dot((qf * xib_ref[...]).astype(BF16), later[j]))
        o = jnp.concatenate(outs, axis=0)
        avg = avg_ref[...]
        d = o - _dot_split(o, avg)
        var = _dot_split(d * d, avg)
        g = x_ref[0, :, 3 * W:4 * W].astype(F32)
        o_ref[0] = (d * lax.rsqrt(var + NORM_EPS) * gn_ref[...] * (g * jax.nn.sigmoid(g))).astype(BF16)


def _ret_tables(decay_logit):
    C = RET_CHUNK
    heads = BRANCH_WIDTH // HEAD_DIM
    lg = jax.nn.log_sigmoid(decay_logit.astype(F32))
    pos = np.arange(C, dtype=np.float32)
    diff = pos[:, None] - pos[None, :]
    fwd = jnp.exp(jnp.where(diff >= 0, diff[None] * lg[0][:, None, None], -jnp.inf))
    bwd = jnp.exp(jnp.where(diff < 0, -diff[None] * lg[1][:, None, None], -jnp.inf))
    dec = (fwd + bwd).transpose(1, 0, 2).reshape(C, heads * C)
    lane_lg = jnp.repeat(lg, HEAD_DIM, axis=1)
    xif = jnp.exp((pos + 1.0)[:, None] * lane_lg[0][None, :])
    xib = jnp.exp((C - pos)[:, None] * lane_lg[1][None, :])
    zf = jnp.exp((C - 1.0 - pos)[:, None] * lane_lg[0][None, :])
    zb = jnp.exp(pos[:, None] * lane_lg[1][None, :])
    gf = jnp.exp(C * lane_lg[0])[None, :]
    gb = jnp.exp(C * lane_lg[1])[None, :]
    return dec, xif, xib, zf, zb, gf, gb


def _head_block_constants():
    hid = np.arange(BRANCH_WIDTH) // HEAD_DIM
    same = (hid[:, None] == hid[None, :])
    return jnp.asarray(same.astype(np.float32)), jnp.asarray(same.astype(np.float32) / HEAD_DIM, dtype=BF16)


def _ret_call(ret_in, tables, gn_gain):
    B, T, _ = ret_in.shape
    rows = RET_CHUNK * RET_CHUNKS_PER_STEP
    N = T // RET_CHUNK
    S = T // rows
    bd, avg = _head_block_constants()
    step_rows = lambda b, p, n: (b, jnp.where(p == 0, n, S - 1 - n), 0)
    out_rows = lambda b, p, n: (b, jnp.where(p == 0, S - 1, S - 1 - n), 0)
    consts = list(tables) + [bd, avg, gn_gain.reshape(1, BRANCH_WIDTH).astype(F32)]
    return pl.pallas_call(
        _ret_kernel,
        grid=(B, 2, S),
        in_specs=[pl.BlockSpec((1, rows, 4 * BRANCH_WIDTH), step_rows)] + [_resident(t.shape) for t in consts],
        out_specs=pl.BlockSpec((1, rows, BRANCH_WIDTH), out_rows),
        out_shape=jax.ShapeDtypeStruct((B, T, BRANCH_WIDTH), BF16),
        scratch_shapes=[
            pltpu.VMEM((BRANCH_WIDTH, BRANCH_WIDTH), F32),
            pltpu.VMEM((BRANCH_WIDTH, BRANCH_WIDTH), F32),
            pltpu.VMEM((N, BRANCH_WIDTH, BRANCH_WIDTH), BF16),
        ],
        compiler_params=_params(("arbitrary", "arbitrary", "arbitrary")),
        name="ret",
    )(ret_in, *consts)


def _conv_fill_window(u_ref, win_ref, i, nt, tt):
    T = u_ref.shape[1]
    t0 = pl.multiple_of(i * tt, tt)
    left_start = pl.multiple_of(jnp.maximum(t0 - CONV_HALO, 0), CONV_HALO)
    right_start = pl.multiple_of(jnp.minimum(t0 + tt, T - CONV_HALO), CONV_HALO)
    left = u_ref[0, pl.ds(left_start, CONV_HALO), :].astype(F32) * (i > 0).astype(F32)
    right = u_ref[0, pl.ds(right_start, CONV_HALO), :].astype(F32) * (i < nt - 1).astype(F32)
    win = jnp.concatenate([left, u_ref[0, pl.ds(t0, tt), :].astype(F32), right], axis=0)
    span = win_ref.shape[1]
    for s in range(SUBLANES):
        win_ref[s] = win[s:s + span]


def _conv_rows(win_ref, w_ref, b_ref, lg_ref, lb_ref, r):
    off = CONV_HALO - CONV_WIDTH // 2
    acc = jnp.zeros((CONV_ROWS // SUBLANES, SUBLANES, BRANCH_WIDTH), F32)
    for tap in range(CONV_WIDTH):
        s, a = (tap + off) % SUBLANES, (tap + off) // SUBLANES * SUBLANES
        rows = win_ref[s, r + a:r + a + CONV_ROWS, :].reshape(acc.shape)
        acc = acc + rows * w_ref[tap][None]
    acc = acc.reshape(CONV_ROWS, BRANCH_WIDTH) + b_ref[...]
    mu = jnp.mean(acc, axis=-1, keepdims=True)
    d = acc - mu
    var = jnp.mean(d * d, axis=-1, keepdims=True)
    y = d * lax.rsqrt(var + NORM_EPS) * lg_ref[...] + lb_ref[...]
    return (y * jax.nn.sigmoid(y)).astype(BF16)


def _na_layout(rows):
    R = NA_QROWS
    KR = R + NA_WIN_ROWS
    nblk = rows // R
    qc = np.arange(GRID_W)
    ws = np.clip(qc - NA_WIN_COLS // 2, 0, GRID_W - NA_WIN_COLS)
    col_ok = (qc[None, :] >= ws[:, None]) & (qc[None, :] < ws[:, None] + NA_WIN_COLS)
    dc = qc[None, :] - qc[:, None] + NA_WIN_COLS - 1
    n_dr = 2 * NA_WIN_ROWS - 1
    variants, var_of_block, kr0s = [], [], []
    for j in range(nblk):
        r0 = j * R
        kr0 = int(np.clip(r0 - NA_WIN_ROWS // 2, 0, rows - KR))
        r = r0 + np.arange(R)
        start = np.clip(r - NA_WIN_ROWS // 2, 0, rows - NA_WIN_ROWS)
        kr = kr0 + np.arange(KR)
        row_ok = (kr[None, :] >= start[:, None]) & (kr[None, :] < start[:, None] + NA_WIN_ROWS)
        tile = np.where(row_ok, kr[None, :] - r[:, None] + NA_WIN_ROWS - 1, n_dr)
        key = tile.tobytes()
        if key not in [v[0] for v in variants]:
            variants.append((key, tile))
        var_of_block.append([v[0] for v in variants].index(key))
        kr0s.append(kr0)
    tiles = np.stack([v[1] for v in variants])
    onehot = (dc[None] == np.arange(2 * NA_WIN_COLS - 1)[:, None, None]) & col_ok[None]
    return tiles, onehot, col_ok, np.asarray(var_of_block, np.int32), np.asarray(kr0s, np.int32)


def _na_bias_table(rpb, rows):
    tiles, onehot, col_ok, _, _ = _na_layout(rows)
    H, n_dr, n_dc = rpb.shape
    oh = jnp.asarray(onehot.reshape(n_dc, GRID_W * GRID_W), F32)
    tz = jnp.dot(rpb.astype(F32).reshape(H * n_dr, n_dc), oh, precision=lax.Precision.HIGHEST)
    tz = jnp.where(col_ok[None, None], tz.reshape(H, n_dr, GRID_W, GRID_W) * LOG2E, NEG_INF)
    tz = jnp.concatenate([tz, jnp.full((H, 1, GRID_W, GRID_W), NEG_INF, F32)], axis=1)
    nv, R, KR = tiles.shape
    per_variant = []
    for vv in range(nv):
        slabs = [jnp.concatenate([tz[:, int(tiles[vv, rl, kl])] for kl in range(KR)], axis=-1) for rl in range(R)]
        per_variant.append(jnp.concatenate(slabs, axis=1))
    return jnp.stack(per_variant)


def _na_bias_tables(rpb, row_counts):
    by_layout, out = {}, {}
    for rows in row_counts:
        key = _na_layout(rows)[0].tobytes()
        if key not in by_layout:
            by_layout[key] = _na_bias_table(rpb, rows)
        out[rows] = by_layout[key]
    return out


def _na_kernel(var_ref, kr0_ref, q_ref, k_ref, v_ref, tab_ref, o_ref):
    step = pl.program_id(1)
    nq = q_ref.shape[1] // NA_BLOCKS_PER_STEP
    nk = tab_ref.shape[3]
    low = lax.broadcasted_iota(jnp.int32, (nq, LANES), 1) < HEAD_DIM
    units = [(sb, pair) for sb in range(NA_BLOCKS_PER_STEP) for pair in range(BRANCH_WIDTH // LANES)]
    blocks = [step * NA_BLOCKS_PER_STEP + sb for sb in range(NA_BLOCKS_PER_STEP)]
    kstart = [pl.multiple_of(kr0_ref[blk] * GRID_W, GRID_W) for blk in blocks]
    variant = [var_ref[blk] for blk in blocks]
    scores = []
    for sb, pair in units:
        cols = slice(pair * LANES, (pair + 1) * LANES)
        q2 = q_ref[0, sb * nq:(sb + 1) * nq, cols]
        zero = jnp.zeros_like(q2)
        qs = jnp.concatenate([jnp.where(low, q2, zero), jnp.where(low, zero, q2)], axis=0)
        scores.append(_dot_nt(qs, k_ref[0, pl.ds(kstart[sb], nk), cols]) + tab_ref[variant[sb], pair])
    weights = [jnp.exp2(s - jnp.max(s, axis=-1, keepdims=True)).astype(BF16) for s in scores]
    ones = jnp.ones((nk, LANES), BF16)
    for (sb, pair), e in zip(units, weights):
        cols = slice(pair * LANES, (pair + 1) * LANES)
        o2 = _dot(e, jnp.concatenate([v_ref[0, pl.ds(kstart[sb], nk), cols], ones], axis=1))
        o2 = o2[:, :LANES] * (1.0 / o2[:, LANES:])
        o_ref[0, sb * nq:(sb + 1) * nq, cols] = jnp.where(low, o2[:nq], o2[nq:]).astype(BF16)


def _na_call(q, k, v, tab):
    B, T, _ = q.shape
    rows = T // GRID_W
    _, _, _, var_of_block, kr0s = _na_layout(rows)
    nq = NA_QROWS * GRID_W
    per_step = NA_BLOCKS_PER_STEP
    nblk = rows // NA_QROWS
    assert nblk % per_step == 0
    nv, heads, _, nk = tab.shape
    tab = tab.reshape(nv, heads // 2, 2 * nq, nk)
    grid_spec = pltpu.PrefetchScalarGridSpec(
        num_scalar_prefetch=2,
        grid=(B, nblk // per_step),
        in_specs=[
            pl.BlockSpec((1, per_step * nq, BRANCH_WIDTH), lambda b, j, var, kr0: (b, j, 0)),
            pl.BlockSpec((1, T, BRANCH_WIDTH), lambda b, j, var, kr0: (b, 0, 0)),
            pl.BlockSpec((1, T, BRANCH_WIDTH), lambda b, j, var, kr0: (b, 0, 0)),
            pl.BlockSpec(memory_space=pltpu.VMEM),
        ],
        out_specs=pl.BlockSpec((1, per_step * nq, BRANCH_WIDTH), lambda b, j, var, kr0: (b, j, 0)),
    )
    return pl.pallas_call(
        _na_kernel,
        grid_spec=grid_spec,
        out_shape=jax.ShapeDtypeStruct((B, T, BRANCH_WIDTH), BF16),
        compiler_params=_params(("arbitrary", "arbitrary")),
        name="na",
    )(jnp.asarray(var_of_block), jnp.asarray(kr0s), q, k, v, tab)


def _t5_bucket(rel):
    half = T5_BUCKETS // 2
    exact = half // 2
    n = np.abs(rel)
    large = exact + (np.log(np.maximum(n, 1) / exact) / math.log(T5_MAX_DIST / exact) * (half - exact)).astype(np.int64)
    large = np.minimum(large, half - 1)
    return (rel > 0).astype(np.int64) * half + np.where(n < exact, n, large)


def _swa_bias_table(t5_bias):
    kpos = np.arange(3 * SWA_BLOCK) - SWA_BLOCK
    rel = kpos[None, :] - np.arange(SWA_BLOCK)[:, None]
    onehot = (_t5_bucket(rel)[None] == np.arange(T5_BUCKETS)[:, None, None])
    oh = jnp.asarray(onehot.reshape(T5_BUCKETS, -1), F32)
    tab = jnp.dot(t5_bias.astype(F32).T, oh, precision=lax.Precision.HIGHEST).reshape((-1,) + rel.shape)
    tab = jnp.where((np.abs(rel) <= SWA_WINDOW)[None], tab * LOG2E, NEG_INF)
    before = (kpos < 0)[None, None, :]
    after = (kpos >= SWA_BLOCK)[None, None, :]
    tabs = jnp.stack([jnp.where(before, NEG_INF, tab), tab, jnp.where(after, NEG_INF, tab)])
    hq = tab.shape[0]
    return tabs.reshape(3, hq // 2, 2 * SWA_BLOCK, 3 * SWA_BLOCK)


def _swa_kernel(sink_ref, q_ref, k_ref, v_ref, tab_ref, o_ref):
    n = pl.program_id(1)
    tq = q_ref.shape[1]
    T = k_ref.shape[1]
    nb = T // SWA_BLOCK
    per_step = tq // SWA_BLOCK
    low = lax.broadcasted_iota(jnp.int32, (SWA_BLOCK, LANES), 1) < HEAD_DIM
    first_rows = lax.broadcasted_iota(jnp.int32, (2 * SWA_BLOCK, 1), 0) < SWA_BLOCK
    units = [(i, hk) for i in range(per_step) for hk in range(2)]

    def band(ref, blk):
        prev_start = pl.multiple_of(jnp.maximum(blk - 1, 0) * SWA_BLOCK, SWA_BLOCK)
        cur_start = pl.multiple_of(blk * SWA_BLOCK, SWA_BLOCK)
        next_start = pl.multiple_of(jnp.minimum(blk + 1, nb - 1) * SWA_BLOCK, SWA_BLOCK)
        return jnp.concatenate([ref[0, pl.ds(prev_start, SWA_BLOCK), :],
                                ref[0, pl.ds(cur_start, SWA_BLOCK), :],
                                ref[0, pl.ds(next_start, SWA_BLOCK), :]], axis=0)

    scores = []
    for i, hk in units:
        blk = n * per_step + i
        variant = jnp.where(blk == 0, 0, jnp.where(blk == nb - 1, 2, 1))
        q2 = q_ref[0, i * SWA_BLOCK:(i + 1) * SWA_BLOCK, hk * LANES:(hk + 1) * LANES]
        q2r = pltpu.roll(q2, HEAD_DIM, 1)
        zero = jnp.zeros_like(q2)
        if hk == 0:
            qs = jnp.concatenate([jnp.where(low, q2, zero), jnp.where(low, q2r, zero)], axis=0)
        else:
            qs = jnp.concatenate([jnp.where(low, zero, q2r), jnp.where(low, zero, q2)], axis=0)
        scores.append(_dot_nt(qs, band(k_ref, blk)) + tab_ref[variant, hk])
    weights = []
    for (i, hk), s in zip(units, scores):
        sink = jnp.where(first_rows, sink_ref[2 * hk], sink_ref[2 * hk + 1]) * LOG2E
        m = jnp.maximum(jnp.max(s, axis=-1, keepdims=True), sink)
        weights.append((jnp.exp2(s - m).astype(BF16), jnp.exp2(sink - m)))
    ones = jnp.ones((3 * SWA_BLOCK, LANES), BF16)
    for (i, hk), (e, sink_weight) in zip(units, weights):
        o2 = _dot(e, jnp.concatenate([band(v_ref, n * per_step + i), ones], axis=1))
        o2 = o2[:, :LANES] * (1.0 / (o2[:, LANES:] + sink_weight))
        top, bot = o2[:SWA_BLOCK], o2[SWA_BLOCK:]
        if hk == 0:
            out = jnp.where(low, top, pltpu.roll(bot, HEAD_DIM, 1))
        else:
            out = jnp.where(low, pltpu.roll(top, HEAD_DIM, 1), bot)
        o_ref[0, i * SWA_BLOCK:(i + 1) * SWA_BLOCK, hk * LANES:(hk + 1) * LANES] = out.astype(BF16)


def _swa_call(q, k, v, tab, sink):
    B, T, _ = q.shape
    tq = TQ_SWA
    assert T // SWA_BLOCK >= 2
    return pl.pallas_call(
        _swa_kernel,
        grid=(B, T // tq),
        in_specs=[
            pl.BlockSpec(memory_space=pltpu.SMEM),
            pl.BlockSpec((1, tq, BRANCH_WIDTH), lambda b, i: (b, i, 0)),
            pl.BlockSpec((1, T, LANES), lambda b, i: (b, 0, 0)),
            pl.BlockSpec((1, T, LANES), lambda b, i: (b, 0, 0)),
            _resident(tab.shape),
        ],
        out_specs=pl.BlockSpec((1, tq, BRANCH_WIDTH), lambda b, i: (b, i, 0)),
        out_shape=jax.ShapeDtypeStruct((B, T, BRANCH_WIDTH), BF16),
        compiler_params=_params(("arbitrary", "arbitrary")),
        name="swa",
    )(sink.astype(F32), q, k, v, tab)


def _merge_kernel(x_ref, mod_ref, gain_ref, ret_ref, u_ref, na_ref, swa_ref, cw_ref, cb_ref, clg_ref, clb_ref,
                  wm_ref, wb_ref, wo_ref, o_ref, win_ref, conv_ref):
    tm = x_ref.shape[1]
    x = x_ref[0]
    h = _modulated_rmsnorm(x, gain_ref[...], mod_ref[0, 1:2, :], mod_ref[0, 0:1, :]).astype(BF16)
    _conv_fill_window(u_ref, win_ref, pl.program_id(1), pl.num_programs(1), tm)
    branch_refs = {0: ret_ref, 2: na_ref, 3: swa_ref}
    width = D_MODEL // MERGE_COL_SPLITS
    col_blocks = [slice(c * width, (c + 1) * width) for c in range(MERGE_COL_SPLITS)]
    slots = [(i, c) for c in range(MERGE_COL_SPLITS) for i in (0, 2, 3, 1)]
    groups = list(range(0, tm, CONV_ROWS))
    after_slot = [[] for _ in slots]
    for g, r in enumerate(groups):
        after_slot[1 + g * (len(slots) - 1) // len(groups)].append(r)
    merged = [None] * MERGE_COL_SPLITS
    conv_gate = [None] * MERGE_COL_SPLITS
    tie = None
    for n, (i, c) in enumerate(slots):
        gate = jax.nn.sigmoid(_dot(h, wm_ref[i, :, col_blocks[c]]))
        if i in branch_refs:
            br = branch_refs[i][0]
            term = gate * _dot(br if tie is None else br + tie, wb_ref[i, :, col_blocks[c]])
            merged[c] = term if merged[c] is None else merged[c] + term
        else:
            conv_gate[c] = gate
        for r in after_slot[n]:
            rows = _conv_rows(win_ref, cw_ref, cb_ref, clg_ref, clb_ref, r)
            conv_ref[r:r + CONV_ROWS, :] = rows
            tie = _zero_tied_to(rows)
    merged = jnp.concatenate(merged, axis=1) + jnp.concatenate(conv_gate, axis=1) * _dot(conv_ref[...], wb_ref[1])
    o_ref[0] = x + mod_ref[0, 2:3, :] * _dot(merged.astype(BF16), wo_ref[...])


def _merge_call(x, mod, gain, ret, u, na, swa, conv_params, w_merge, w_branch, w_out):
    B, T, _ = x.shape
    tm = TM_MERGE
    tok = lambda w: pl.BlockSpec((1, tm, w), lambda b, i: (b, i, 0))
    whole = pl.BlockSpec(memory_space=pltpu.VMEM)
    dw_kernel, dw_bias, ln_gain, ln_bias = conv_params
    row = lambda a: a.reshape(1, BRANCH_WIDTH).astype(F32)
    taps = jnp.broadcast_to(dw_kernel.astype(F32)[:, None, :], (CONV_WIDTH, SUBLANES, BRANCH_WIDTH))
    return pl.pallas_call(
        _merge_kernel,
        grid=(B, T // tm),
        in_specs=[tok(D_MODEL), pl.BlockSpec((1, 6, D_MODEL), lambda b, i: (b, 0, 0)), _resident((1, D_MODEL)),
                  tok(BRANCH_WIDTH), pl.BlockSpec((1, T, BRANCH_WIDTH), lambda b, i: (b, 0, 0)),
                  tok(BRANCH_WIDTH), tok(BRANCH_WIDTH),
                  _resident((CONV_WIDTH, SUBLANES, BRANCH_WIDTH)),
                  _resident((1, BRANCH_WIDTH)), _resident((1, BRANCH_WIDTH)), _resident((1, BRANCH_WIDTH)),
                  whole, whole, whole],
        out_specs=tok(D_MODEL),
        out_shape=jax.ShapeDtypeStruct((B, T, D_MODEL), F32),
        scratch_shapes=[pltpu.VMEM((SUBLANES, tm + 2 * CONV_HALO - SUBLANES, BRANCH_WIDTH), F32),
                        pltpu.VMEM((tm, BRANCH_WIDTH), BF16)],
        compiler_params=_params(("arbitrary", "arbitrary")),
        name="merge",
    )(x, mod, gain, ret, u, na, swa, taps, row(dw_bias), row(ln_gain), row(ln_bias), w_merge, w_branch, w_out)


def _ffn_kernel(x_ref, mod_ref, gain_ref, fin_ref, w1_ref, w2_ref, o_ref, *, final_norm):
    x = x_ref[0]
    h = _modulated_rmsnorm(x, gain_ref[...], mod_ref[0, 4:5, :], mod_ref[0, 3:4, :]).astype(BF16)
    ff = None
    for c in range(0, D_FF, FFN_CHUNK):
        a = jnp.maximum(_dot(h, w1_ref[:, c:c + FFN_CHUNK]), 0.0)
        part = _dot((a * a).astype(BF16), w2_ref[c:c + FFN_CHUNK, :])
        ff = part if ff is None else ff + part
    y = x + mod_ref[0, 5:6, :] * ff
    if final_norm:
        y = y * lax.rsqrt(jnp.mean(y * y, axis=-1, keepdims=True) + NORM_EPS) * fin_ref[...]
    o_ref[0] = y


def _ffn_call(x, mod, gain, final_gain, w1, w2, final_norm):
    B, T, _ = x.shape
    tm = TM_FFN
    tok = pl.BlockSpec((1, tm, D_MODEL), lambda b, i: (b, i, 0))
    whole = pl.BlockSpec(memory_space=pltpu.VMEM)
    return pl.pallas_call(
        functools.partial(_ffn_kernel, final_norm=final_norm),
        grid=(B, T // tm),
        in_specs=[tok, pl.BlockSpec((1, 6, D_MODEL), lambda b, i: (b, 0, 0)), _resident((1, D_MODEL)),
                  _resident((1, D_MODEL)), whole, whole],
        out_specs=tok,
        out_shape=jax.ShapeDtypeStruct((B, T, D_MODEL), F32),
        compiler_params=_params(("arbitrary", "arbitrary")),
        name="ffn",
    )(x, mod, gain, final_gain, w1, w2)


def _rotary_tables(T):
    half = HEAD_DIM // 2
    inv = (ROPE_BASE ** (-np.arange(half, dtype=np.float32) / half)).astype(np.float32)
    ang = (np.arange(T, dtype=np.float32)[:, None] * inv[None, :]).astype(np.float32).astype(np.float64)
    lane = np.arange(LANES) % HEAD_DIM
    cos = np.cos(ang)[:, lane % half]
    sin = np.sin(ang)[:, lane % half] * np.where(lane < half, -1.0, 1.0)[None, :]
    return jnp.asarray(cos, F32), jnp.asarray(sin, F32)


def _trunk(x, mods, layer_params, shared):
    B, T, _ = x.shape
    cos_t, sin_t = _rotary_tables(T)
    rows = T // GRID_W
    for l, lp in enumerate(layer_params):
        mod = mods[l]
        ret_in, u, naq, nak, nav, sq, sk, sv = _proj_call(x, mod, lp["gain1"], lp["w_in"], cos_t, sin_t)
        ret = _ret_call(ret_in, lp["ret_tables"], lp["ret_gn_gain"])
        na = _na_call(naq, nak, nav, lp["na_tables"][rows])
        swa = _swa_call(sq, sk, sv, shared["swa_table"], lp["swa_sink"])
        conv_params = (lp["conv_dw_kernel"], lp["conv_dw_bias"], lp["conv_ln_gain"], lp["conv_ln_bias"])
        x = _merge_call(x, mod, lp["gain1"], ret, u, na, swa, conv_params, lp["w_merge"], lp["w_branch"], lp["w_out"])
        x = _ffn_call(x, mod, lp["gain2"], shared["final_gain"], lp["w_ff1"], lp["w_ff2"], l == len(layer_params) - 1)
    return x


def kernel(x_prompt, x_sample, c_prompt, c_sample, w_ada, b_ada, norm_gain, w_in, ret_decay_logit, ret_gn_gain,
           conv_dw_kernel, conv_dw_bias, conv_ln_gain, conv_ln_bias, na_rpb, swa_sink, t5_bias,
           w_branch, w_merge, w_out, w_ff1, w_ff2, final_gain):
    nbp = c_prompt.shape[0]
    c_all = jnp.concatenate([c_prompt, c_sample], axis=0)
    mods = _ada_call(c_all, w_ada, b_ada.astype(F32))
    mods = mods.reshape(DEPTH, c_all.shape[0], 6, D_MODEL)
    row_counts = sorted({x_prompt.shape[1] // GRID_W, x_sample.shape[1] // GRID_W})
    layer_params = []
    for l in range(DEPTH):
        layer_params.append(dict(
            gain1=norm_gain[l, 0].reshape(1, D_MODEL).astype(F32),
            gain2=norm_gain[l, 1].reshape(1, D_MODEL).astype(F32),
            w_in=w_in[l].astype(BF16),
            ret_tables=_ret_tables(ret_decay_logit[l]),
            ret_gn_gain=ret_gn_gain[l],
            conv_dw_kernel=conv_dw_kernel[l], conv_dw_bias=conv_dw_bias[l],
            conv_ln_gain=conv_ln_gain[l], conv_ln_bias=conv_ln_bias[l],
            na_tables=_na_bias_tables(na_rpb[l], row_counts),
            swa_sink=swa_sink[l],
            w_merge=w_merge[l].astype(BF16), w_branch=w_branch[l].astype(BF16), w_out=w_out[l].astype(BF16),
            w_ff1=w_ff1[l].astype(BF16), w_ff2=w_ff2[l].astype(BF16),
        ))
    shared = dict(swa_table=_swa_bias_table(t5_bias), final_gain=final_gain.reshape(1, D_MODEL).astype(F32))
    y_prompt = _trunk(x_prompt, [m[:nbp] for m in mods], layer_params, shared)
    y_sample = _trunk(x_sample, [m[nbp:] for m in mods], layer_params, shared)
    return (y_prompt, y_sample)
```

```python
import functools
import math

import jax
import jax.numpy as jnp
import numpy as np
from jax import lax
from jax.experimental import pallas as pl
from jax.experimental.pallas import tpu as pltpu

F32 = jnp.float32
BF16 = jnp.bfloat16

D_MODEL = 1024
DEPTH = 2
HEAD_DIM = 64
BRANCH_WIDTH = 256
RET_CHUNK = 128
RET_CHUNKS_PER_STEP = 16
ROPE_BASE = 10000.0
CONV_WIDTH = 31
CONV_HALO = 16
GRID_W = 64
NA_WIN_ROWS = 8
NA_WIN_COLS = 16
NA_QROWS = 2
NA_BLOCKS_PER_STEP = 16
SWA_WINDOW = 128
SWA_BLOCK = 128
T5_BUCKETS = 32
T5_MAX_DIST = 128
D_FF = 4 * D_MODEL
NORM_EPS = 1e-6
NEG_INF = -1e30
LOG2E = 1.4426950408889634
SWA_KV_WIDTH = 128
RET_COL0 = 0
CONV_COL0 = RET_COL0 + 4 * BRANCH_WIDTH
NA_COL0 = CONV_COL0 + 2 * BRANCH_WIDTH
SWA_COL0 = NA_COL0 + 3 * BRANCH_WIDTH
IN_COLS = SWA_COL0 + BRANCH_WIDTH + 2 * SWA_KV_WIDTH
LANES = 128
SUBLANES = 8

TM_PROJ = 1024
PROJ_ROW_SPLITS = 4
TM_MERGE = 512
MERGE_COL_SPLITS = 4
TM_FFN = 1024
FFN_CHUNK = 1024
CONV_ROWS = 32
TQ_SWA = 2048
VMEM_LIMIT = 56 * 1024 * 1024


def _params(sem, flags=None):
    return pltpu.CompilerParams(dimension_semantics=sem, vmem_limit_bytes=VMEM_LIMIT, flags=flags)


def _resident(shape):
    nd = len(shape)
    return pl.BlockSpec(shape, lambda *_: (0,) * nd)


def _modulated_rmsnorm(x, gain, scale, shift):
    y = x * lax.rsqrt(jnp.mean(x * x, axis=-1, keepdims=True) + NORM_EPS)
    return (y * gain) * (1.0 + scale) + shift


def _dot(a, b):
    return jnp.dot(a, b, preferred_element_type=F32)


def _dot_nt(a, b):
    return lax.dot_general(a, b, (((1,), (1,)), ((), ())), preferred_element_type=F32)


def _dot_tn(a, b):
    return lax.dot_general(a, b, (((0,), (0,)), ((), ())), preferred_element_type=F32)


def _zero_tied_to(v):
    bits = lax.bitcast_convert_type(v[0:1, :].astype(F32), jnp.uint32)
    zero = lax.shift_right_logical(lax.shift_right_logical(bits, jnp.uint32(16)), jnp.uint32(16))
    return zero.astype(F32).astype(BF16)


def _dot_split(x, w):
    hi = x.astype(BF16)
    lo = (x - hi.astype(F32)).astype(BF16)
    return _dot(hi, w) + _dot(lo, w)


def _ada_kernel(c_ref, w_ref, b_ref, o_ref):
    c = c_ref[...]
    a = (c * jax.nn.sigmoid(c)).astype(BF16)
    o_ref[0] = _dot(a, w_ref[0].astype(BF16)) + b_ref[0]


def _ada_call(c_all, w_ada, b_ada):
    nb = c_all.shape[0]
    return pl.pallas_call(
        _ada_kernel,
        grid=(DEPTH, 6),
        in_specs=[
            pl.BlockSpec((nb, D_MODEL), lambda l, j: (0, 0)),
            pl.BlockSpec((1, D_MODEL, D_MODEL), lambda l, j: (l, 0, j)),
            pl.BlockSpec((1, 1, D_MODEL), lambda l, j: (l, 0, j)),
        ],
        out_specs=pl.BlockSpec((1, nb, D_MODEL), lambda l, j: (l, 0, j)),
        out_shape=jax.ShapeDtypeStruct((DEPTH, nb, 6 * D_MODEL), F32),
        compiler_params=_params(("arbitrary", "arbitrary")),
        name="ada",
    )(c_all, w_ada, b_ada.reshape(DEPTH, 1, 6 * D_MODEL))


def _proj_kernel(x_ref, mod_ref, gain_ref, w_ref, cos_ref, sin_ref,
                 ret_ref, u_ref, naq_ref, nak_ref, nav_ref, sq_ref, sk_ref, sv_ref):
    tm = x_ref.shape[1]
    W = BRANCH_WIDTH
    half = tm // PROJ_ROW_SPLITS
    lane = lax.broadcasted_iota(jnp.int32, (half, LANES), 1)
    first_half = (lane & (HEAD_DIM // 2)) == 0
    qscale = HEAD_DIM ** -0.5
    blocks = [slice(n * half, (n + 1) * half) for n in range(PROJ_ROW_SPLITS)]
    hs = [_modulated_rmsnorm(x_ref[0, rows, :], gain_ref[...], mod_ref[0, 1:2, :], mod_ref[0, 0:1, :]).astype(BF16)
          for rows in blocks]
    for rows, h in zip(blocks, hs):
        cos = cos_ref[rows, :]
        sin = sin_ref[rows, :]

        def rot(y):
            sw = jnp.where(first_half, pltpu.roll(y, LANES - HEAD_DIM // 2, 1), pltpu.roll(y, HEAD_DIM // 2, 1))
            return y * cos + sw * sin

        y = _dot(h, w_ref[:, RET_COL0:RET_COL0 + 4 * W])
        for t in range(W // LANES):
            ret_ref[0, rows, t * LANES:(t + 1) * LANES] = (rot(y[:, t * LANES:(t + 1) * LANES]) * qscale).astype(BF16)
        for t in range(W // LANES, 2 * W // LANES):
            ret_ref[0, rows, t * LANES:(t + 1) * LANES] = rot(y[:, t * LANES:(t + 1) * LANES]).astype(BF16)
        ret_ref[0, rows, 2 * W:4 * W] = y[:, 2 * W:4 * W].astype(BF16)
        y = _dot(h, w_ref[:, CONV_COL0:CONV_COL0 + 2 * W])
        u_ref[0, rows, :] = (y[:, 0:W] * jax.nn.sigmoid(y[:, W:2 * W])).astype(BF16)
        y = _dot(h, w_ref[:, NA_COL0:NA_COL0 + 3 * W])
        naq_ref[0, rows, :] = (y[:, 0:W] * (qscale * LOG2E)).astype(BF16)
        nak_ref[0, rows, :] = y[:, W:2 * W].astype(BF16)
        nav_ref[0, rows, :] = y[:, 2 * W:3 * W].astype(BF16)
        y = _dot(h, w_ref[:, SWA_COL0:IN_COLS])
        sq_ref[0, rows, :] = (y[:, 0:W] * (qscale * LOG2E)).astype(BF16)
        sk_ref[0, rows, :] = y[:, W:W + SWA_KV_WIDTH].astype(BF16)
        sv_ref[0, rows, :] = y[:, W + SWA_KV_WIDTH:W + 2 * SWA_KV_WIDTH].astype(BF16)


def _proj_call(x, mod, gain, w_in, cos_t, sin_t):
    B, T, _ = x.shape
    tm = TM_PROJ
    tok = lambda w: pl.BlockSpec((1, tm, w), lambda b, i: (b, i, 0))
    W = BRANCH_WIDTH
    widths = (4 * W, W, W, W, W, W, SWA_KV_WIDTH, SWA_KV_WIDTH)
    return pl.pallas_call(
        _proj_kernel,
        grid=(B, T // tm),
        in_specs=[
            tok(D_MODEL),
            pl.BlockSpec((1, 6, D_MODEL), lambda b, i: (b, 0, 0)),
            _resident((1, D_MODEL)),
            _resident((D_MODEL, IN_COLS)),
            pl.BlockSpec((tm, LANES), lambda b, i: (i, 0)),
            pl.BlockSpec((tm, LANES), lambda b, i: (i, 0)),
        ],
        out_specs=[tok(w) for w in widths],
        out_shape=[jax.ShapeDtypeStruct((B, T, w), BF16) for w in widths],
        compiler_params=_params(("arbitrary", "arbitrary")),
        name="proj",
    )(x, mod, gain, w_in, cos_t, sin_t)


def _ret_kernel(x_ref, dec_ref, xif_ref, xib_ref, zf_ref, zb_ref, gf_ref, gb_ref, bd_ref, avg_ref, gn_ref,
                o_ref, rf_ref, rb_ref, stash_ref):
    p = pl.program_id(1)
    n = pl.program_id(2)
    nsteps = pl.num_programs(2)
    C = RET_CHUNK
    W = BRANCH_WIDTH
    G = x_ref.shape[1] // C
    heads = W // HEAD_DIM

    def chunk_kv(j):
        rows = slice(j * C, (j + 1) * C)
        return x_ref[0, rows, W:2 * W], x_ref[0, rows, 2 * W:3 * W]

    def chunk_state(k, v, zeta):
        return _dot_tn((k.astype(F32) * zeta).astype(BF16), v) * bd_ref[...]

    @pl.when(p == 0)
    def _forward_states():
        @pl.when(n == 0)
        def _():
            rf_ref[...] = jnp.zeros_like(rf_ref)

        updates = [chunk_state(*chunk_kv(j), zf_ref[...]) for j in range(G)]
        rf = rf_ref[...]
        for j in range(G):
            stash_ref[n * G + j] = rf.astype(BF16)
            rf = gf_ref[...] * rf + updates[j]
        rf_ref[...] = rf

    @pl.when(p == 1)
    def _outputs():
        @pl.when(n == 0)
        def _():
            rb_ref[...] = jnp.zeros_like(rb_ref)

        first_chunk = (nsteps - 1 - n) * G
        lane_head = lax.broadcasted_iota(jnp.int32, (C, W), 1) // HEAD_DIM

        def head_rows(a):
            return jnp.concatenate([jnp.where(lane_head == hh, a, jnp.zeros_like(a)) for hh in range(heads)], axis=0)

        qs = [x_ref[0, j * C:(j + 1) * C, 0:W] for j in range(G)]
        weights = [(_dot_nt(qs[j], head_rows(chunk_kv(j)[0])) * dec_ref[...]).astype(BF16) for j in range(G)]
        updates = [chunk_state(*chunk_kv(j), zb_ref[...]) for j in range(G)]
        rb = rb_ref[...]
        later = [None] * G
        for j in reversed(range(G)):
            later[j] = rb.astype(BF16)
            rb = gb_ref[...] * rb + updates[j]
        rb_ref[...] = rb
        outs = []
        for j in range(G):
            qf = qs[j].astype(F32)
            o = _dot(weights[j], head_rows(chunk_kv(j)[1]))
            o = o + _dot((qf * xif_ref[...]).astype(BF16), stash_ref[first_chunk + j])
            outs.append(o + _dot((qf * xib_ref[...]).astype(BF16), later[j]))
        o = jnp.concatenate(outs, axis=0)
        avg = avg_ref[...]
        d = o - _dot_split(o, avg)
        var = _dot((d * d).astype(BF16), avg)
        g = x_ref[0, :, 3 * W:4 * W].astype(F32)
        o_ref[0] = (d * lax.rsqrt(var + NORM_EPS) * gn_ref[...] * (g * jax.nn.sigmoid(g))).astype(BF16)


def _ret_tables(decay_logit):
    C = RET_CHUNK
    heads = BRANCH_WIDTH // HEAD_DIM
    lg = jax.nn.log_sigmoid(decay_logit.astype(F32))
    pos = np.arange(C, dtype=np.float32)
    diff = pos[:, None] - pos[None, :]
    fwd = jnp.exp(jnp.where(diff >= 0, diff[None] * lg[0][:, None, None], -jnp.inf))
    bwd = jnp.exp(jnp.where(diff < 0, -diff[None] * lg[1][:, None, None], -jnp.inf))
    dec = (fwd + bwd).transpose(1, 0, 2).reshape(C, heads * C)
    lane_lg = jnp.repeat(lg, HEAD_DIM, axis=1)
    xif = jnp.exp((pos + 1.0)[:, None] * lane_lg[0][None, :])
    xib = jnp.exp((C - pos)[:, None] * lane_lg[1][None, :])
    zf = jnp.exp((C - 1.0 - pos)[:, None] * lane_lg[0][None, :])
    zb = jnp.exp(pos[:, None] * lane_lg[1][None, :])
    gf = jnp.exp(C * lane_lg[0])[None, :]
    gb = jnp.exp(C * lane_lg[1])[None, :]
    return dec, xif, xib, zf, zb, gf, gb


def _head_block_constants():
    hid = np.arange(BRANCH_WIDTH) // HEAD_DIM
    same = (hid[:, None] == hid[None, :])
    return jnp.asarray(same.astype(np.float32)), jnp.asarray(same.astype(np.float32) / HEAD_DIM, dtype=BF16)


def _ret_call(ret_in, tables, gn_gain):
    B, T, _ = ret_in.shape
    rows = RET_CHUNK * RET_CHUNKS_PER_STEP
    N = T // RET_CHUNK
    S = T // rows
    bd, avg = _head_block_constants()
    step_rows = lambda b, p, n: (b, jnp.where(p == 0, n, S - 1 - n), 0)
    out_rows = lambda b, p, n: (b, jnp.where(p == 0, S - 1, S - 1 - n), 0)
    consts = list(tables) + [bd, avg, gn_gain.reshape(1, BRANCH_WIDTH).astype(F32)]
    return pl.pallas_call(
        _ret_kernel,
        grid=(B, 2, S),
        in_specs=[pl.BlockSpec((1, rows, 4 * BRANCH_WIDTH), step_rows)] + [_resident(t.shape) for t in consts],
        out_specs=pl.BlockSpec((1, rows, BRANCH_WIDTH), out_rows),
        out_shape=jax.ShapeDtypeStruct((B, T, BRANCH_WIDTH), BF16),
        scratch_shapes=[
            pltpu.VMEM((BRANCH_WIDTH, BRANCH_WIDTH), F32),
            pltpu.VMEM((BRANCH_WIDTH, BRANCH_WIDTH), F32),
            pltpu.VMEM((N, BRANCH_WIDTH, BRANCH_WIDTH), BF16),
        ],
        compiler_params=_params(("arbitrary", "arbitrary", "arbitrary")),
        name="ret",
    )(ret_in, *consts)


def _conv_fill_window(u_ref, win_ref, i, nt, tt):
    T = u_ref.shape[1]
    t0 = pl.multiple_of(i * tt, tt)
    left_start = pl.multiple_of(jnp.maximum(t0 - CONV_HALO, 0), CONV_HALO)
    right_start = pl.multiple_of(jnp.minimum(t0 + tt, T - CONV_HALO), CONV_HALO)
    left = u_ref[0, pl.ds(left_start, CONV_HALO), :].astype(F32) * (i > 0).astype(F32)
    right = u_ref[0, pl.ds(right_start, CONV_HALO), :].astype(F32) * (i < nt - 1).astype(F32)
    win = jnp.concatenate([left, u_ref[0, pl.ds(t0, tt), :].astype(F32), right], axis=0)
    span = win_ref.shape[1]
    for s in range(SUBLANES):
        win_ref[s] = win[s:s + span]


def _conv_rows(win_ref, w_ref, b_ref, lg_ref, lb_ref, r):
    off = CONV_HALO - CONV_WIDTH // 2
    acc = jnp.zeros((CONV_ROWS // SUBLANES, SUBLANES, BRANCH_WIDTH), F32)
    for tap in range(CONV_WIDTH):
        s, a = (tap + off) % SUBLANES, (tap + off) // SUBLANES * SUBLANES
        rows = win_ref[s, r + a:r + a + CONV_ROWS, :].reshape(acc.shape)
        acc = acc + rows * w_ref[tap][None]
    acc = acc.reshape(CONV_ROWS, BRANCH_WIDTH) + b_ref[...]
    mu = jnp.mean(acc, axis=-1, keepdims=True)
    d = acc - mu
    var = jnp.mean(d * d, axis=-1, keepdims=True)
    y = d * lax.rsqrt(var + NORM_EPS) * lg_ref[...] + lb_ref[...]
    return (y * jax.nn.sigmoid(y)).astype(BF16)


def _na_layout(rows):
    R = NA_QROWS
    KR = R + NA_WIN_ROWS
    nblk = rows // R
    qc = np.arange(GRID_W)
    ws = np.clip(qc - NA_WIN_COLS // 2, 0, GRID_W - NA_WIN_COLS)
    col_ok = (qc[None, :] >= ws[:, None]) & (qc[None, :] < ws[:, None] + NA_WIN_COLS)
    dc = qc[None, :] - qc[:, None] + NA_WIN_COLS - 1
    n_dr = 2 * NA_WIN_ROWS - 1
    variants, var_of_block, kr0s = [], [], []
    for j in range(nblk):
        r0 = j * R
        kr0 = int(np.clip(r0 - NA_WIN_ROWS // 2, 0, rows - KR))
        r = r0 + np.arange(R)
        start = np.clip(r - NA_WIN_ROWS // 2, 0, rows - NA_WIN_ROWS)
        kr = kr0 + np.arange(KR)
        row_ok = (kr[None, :] >= start[:, None]) & (kr[None, :] < start[:, None] + NA_WIN_ROWS)
        tile = np.where(row_ok, kr[None, :] - r[:, None] + NA_WIN_ROWS - 1, n_dr)
        key = tile.tobytes()
        if key not in [v[0] for v in variants]:
            variants.append((key, tile))
        var_of_block.append([v[0] for v in variants].index(key))
        kr0s.append(kr0)
    tiles = np.stack([v[1] for v in variants])
    onehot = (dc[None] == np.arange(2 * NA_WIN_COLS - 1)[:, None, None]) & col_ok[None]
    return tiles, onehot, col_ok, np.asarray(var_of_block, np.int32), np.asarray(kr0s, np.int32)


def _na_bias_table(rpb, rows):
    tiles, onehot, col_ok, _, _ = _na_layout(rows)
    H, n_dr, n_dc = rpb.shape
    oh = jnp.asarray(onehot.reshape(n_dc, GRID_W * GRID_W), F32)
    tz = jnp.dot(rpb.astype(F32).reshape(H * n_dr, n_dc), oh, precision=lax.Precision.HIGHEST)
    tz = jnp.where(col_ok[None, None], tz.reshape(H, n_dr, GRID_W, GRID_W) * LOG2E, NEG_INF)
    tz = jnp.concatenate([tz, jnp.full((H, 1, GRID_W, GRID_W), NEG_INF, F32)], axis=1)
    nv, R, KR = tiles.shape
    per_variant = []
    for vv in range(nv):
        slabs = [jnp.concatenate([tz[:, int(tiles[vv, rl, kl])] for kl in range(KR)], axis=-1) for rl in range(R)]
        per_variant.append(jnp.concatenate(slabs, axis=1))
    return jnp.stack(per_variant)


def _na_bias_tables(rpb, row_counts):
    by_layout, out = {}, {}
    for rows in row_counts:
        key = _na_layout(rows)[0].tobytes()
        if key not in by_layout:
            by_layout[key] = _na_bias_table(rpb, rows)
        out[rows] = by_layout[key]
    return out


def _na_kernel(var_ref, kr0_ref, q_ref, k_ref, v_ref, tab_ref, o_ref):
    step = pl.program_id(1)
    nq = q_ref.shape[1] // NA_BLOCKS_PER_STEP
    nk = tab_ref.shape[3]
    low = lax.broadcasted_iota(jnp.int32, (nq, LANES), 1) < HEAD_DIM
    units = [(sb, pair) for sb in range(NA_BLOCKS_PER_STEP) for pair in range(BRANCH_WIDTH // LANES)]
    blocks = [step * NA_BLOCKS_PER_STEP + sb for sb in range(NA_BLOCKS_PER_STEP)]
    kstart = [pl.multiple_of(kr0_ref[blk] * GRID_W, GRID_W) for blk in blocks]
    variant = [var_ref[blk] for blk in blocks]
    scores = []
    for sb, pair in units:
        cols = slice(pair * LANES, (pair + 1) * LANES)
        q2 = q_ref[0, sb * nq:(sb + 1) * nq, cols]
        zero = jnp.zeros_like(q2)
        qs = jnp.concatenate([jnp.where(low, q2, zero), jnp.where(low, zero, q2)], axis=0)
        scores.append(_dot_nt(qs, k_ref[0, pl.ds(kstart[sb], nk), cols]) + tab_ref[variant[sb], pair])
    weights = [jnp.exp2(s - jnp.max(s, axis=-1, keepdims=True)).astype(BF16) for s in scores]
    ones = jnp.ones((nk, LANES), BF16)
    for (sb, pair), e in zip(units, weights):
        cols = slice(pair * LANES, (pair + 1) * LANES)
        o2 = _dot(e, jnp.concatenate([v_ref[0, pl.ds(kstart[sb], nk), cols], ones], axis=1))
        o2 = o2[:, :LANES] * (1.0 / o2[:, LANES:])
        o_ref[0, sb * nq:(sb + 1) * nq, cols] = jnp.where(low, o2[:nq], o2[nq:]).astype(BF16)


def _na_call(q, k, v, tab):
    B, T, _ = q.shape
    rows = T // GRID_W
    _, _, _, var_of_block, kr0s = _na_layout(rows)
    nq = NA_QROWS * GRID_W
    per_step = NA_BLOCKS_PER_STEP
    nblk = rows // NA_QROWS
    assert nblk % per_step == 0
    nv, heads, _, nk = tab.shape
    tab = tab.reshape(nv, heads // 2, 2 * nq, nk)
    grid_spec = pltpu.PrefetchScalarGridSpec(
        num_scalar_prefetch=2,
        grid=(B, nblk // per_step),
        in_specs=[
            pl.BlockSpec((1, per_step * nq, BRANCH_WIDTH), lambda b, j, var, kr0: (b, j, 0)),
            pl.BlockSpec((1, T, BRANCH_WIDTH), lambda b, j, var, kr0: (b, 0, 0)),
            pl.BlockSpec((1, T, BRANCH_WIDTH), lambda b, j, var, kr0: (b, 0, 0)),
            pl.BlockSpec(memory_space=pltpu.VMEM),
        ],
        out_specs=pl.BlockSpec((1, per_step * nq, BRANCH_WIDTH), lambda b, j, var, kr0: (b, j, 0)),
    )
    return pl.pallas_call(
        _na_kernel,
        grid_spec=grid_spec,
        out_shape=jax.ShapeDtypeStruct((B, T, BRANCH_WIDTH), BF16),
        compiler_params=_params(("arbitrary", "arbitrary")),
        name="na",
    )(jnp.asarray(var_of_block), jnp.asarray(kr0s), q, k, v, tab)


def _t5_bucket(rel):
    half = T5_BUCKETS // 2
    exact = half // 2
    n = np.abs(rel)
    large = exact + (np.log(np.maximum(n, 1) / exact) / math.log(T5_MAX_DIST / exact) * (half - exact)).astype(np.int64)
    large = np.minimum(large, half - 1)
    return (rel > 0).astype(np.int64) * half + np.where(n < exact, n, large)


def _swa_bias_table(t5_bias):
    kpos = np.arange(3 * SWA_BLOCK) - SWA_BLOCK
    rel = kpos[None, :] - np.arange(SWA_BLOCK)[:, None]
    onehot = (_t5_bucket(rel)[None] == np.arange(T5_BUCKETS)[:, None, None])
    oh = jnp.asarray(onehot.reshape(T5_BUCKETS, -1), F32)
    tab = jnp.dot(t5_bias.astype(F32).T, oh, precision=lax.Precision.HIGHEST).reshape((-1,) + rel.shape)
    tab = jnp.where((np.abs(rel) <= SWA_WINDOW)[None], tab * LOG2E, NEG_INF)
    before = (kpos < 0)[None, None, :]
    after = (kpos >= SWA_BLOCK)[None, None, :]
    tabs = jnp.stack([jnp.where(before, NEG_INF, tab), tab, jnp.where(after, NEG_INF, tab)])
    hq = tab.shape[0]
    return tabs.reshape(3, hq // 2, 2 * SWA_BLOCK, 3 * SWA_BLOCK)


def _swa_kernel(sink_ref, q_ref, k_ref, v_ref, tab_ref, o_ref):
    n = pl.program_id(1)
    tq = q_ref.shape[1]
    T = k_ref.shape[1]
    nb = T // SWA_BLOCK
    per_step = tq // SWA_BLOCK
    low = lax.broadcasted_iota(jnp.int32, (SWA_BLOCK, LANES), 1) < HEAD_DIM
    first_rows = lax.broadcasted_iota(jnp.int32, (2 * SWA_BLOCK, 1), 0) < SWA_BLOCK
    units = [(i, hk) for i in range(per_step) for hk in range(2)]

    def band(ref, blk):
        prev_start = pl.multiple_of(jnp.maximum(blk - 1, 0) * SWA_BLOCK, SWA_BLOCK)
        cur_start = pl.multiple_of(blk * SWA_BLOCK, SWA_BLOCK)
        next_start = pl.multiple_of(jnp.minimum(blk + 1, nb - 1) * SWA_BLOCK, SWA_BLOCK)
        return jnp.concatenate([ref[0, pl.ds(prev_start, SWA_BLOCK), :],
                                ref[0, pl.ds(cur_start, SWA_BLOCK), :],
                                ref[0, pl.ds(next_start, SWA_BLOCK), :]], axis=0)

    scores = []
    for i, hk in units:
        blk = n * per_step + i
        variant = jnp.where(blk == 0, 0, jnp.where(blk == nb - 1, 2, 1))
        q2 = q_ref[0, i * SWA_BLOCK:(i + 1) * SWA_BLOCK, hk * LANES:(hk + 1) * LANES]
        q2r = pltpu.roll(q2, HEAD_DIM, 1)
        zero = jnp.zeros_like(q2)
        if hk == 0:
            qs = jnp.concatenate([jnp.where(low, q2, zero), jnp.where(low, q2r, zero)], axis=0)
        else:
            qs = jnp.concatenate([jnp.where(low, zero, q2r), jnp.where(low, zero, q2)], axis=0)
        scores.append(_dot_nt(qs, band(k_ref, blk)) + tab_ref[variant, hk])
    weights = []
    for (i, hk), s in zip(units, scores):
        sink = jnp.where(first_rows, sink_ref[2 * hk], sink_ref[2 * hk + 1]) * LOG2E
        m = jnp.maximum(jnp.max(s, axis=-1, keepdims=True), sink)
        weights.append((jnp.exp2(s - m).astype(BF16), jnp.exp2(sink - m)))
    ones = jnp.ones((3 * SWA_BLOCK, LANES), BF16)
    for (i, hk), (e, sink_weight) in zip(units, weights):
        o2 = _dot(e, jnp.concatenate([band(v_ref, n * per_step + i), ones], axis=1))
        o2 = o2[:, :LANES] * (1.0 / (o2[:, LANES:] + sink_weight))
        top, bot = o2[:SWA_BLOCK], o2[SWA_BLOCK:]
        if hk == 0:
            out = jnp.where(low, top, pltpu.roll(bot, HEAD_DIM, 1))
        else:
            out = jnp.where(low, pltpu.roll(top, HEAD_DIM, 1), bot)
        o_ref[0, i * SWA_BLOCK:(i + 1) * SWA_BLOCK, hk * LANES:(hk + 1) * LANES] = out.astype(BF16)


def _swa_call(q, k, v, tab, sink):
    B, T, _ = q.shape
    tq = TQ_SWA
    assert T // SWA_BLOCK >= 2
    return pl.pallas_call(
        _swa_kernel,
        grid=(B, T // tq),
        in_specs=[
            pl.BlockSpec(memory_space=pltpu.SMEM),
            pl.BlockSpec((1, tq, BRANCH_WIDTH), lambda b, i: (b, i, 0)),
            pl.BlockSpec((1, T, LANES), lambda b, i: (b, 0, 0)),
            pl.BlockSpec((1, T, LANES), lambda b, i: (b, 0, 0)),
            _resident(tab.shape),
        ],
        out_specs=pl.BlockSpec((1, tq, BRANCH_WIDTH), lambda b, i: (b, i, 0)),
        out_shape=jax.ShapeDtypeStruct((B, T, BRANCH_WIDTH), BF16),
        compiler_params=_params(("arbitrary", "arbitrary")),
        name="swa",
    )(sink.astype(F32), q, k, v, tab)


def _merge_kernel(x_ref, mod_ref, gain_ref, ret_ref, u_ref, na_ref, swa_ref, cw_ref, cb_ref, clg_ref, clb_ref,
                  wm_ref, wb_ref, wo_ref, o_ref, win_ref, conv_ref):
    tm = x_ref.shape[1]
    x = x_ref[0]
    h = _modulated_rmsnorm(x, gain_ref[...], mod_ref[0, 1:2, :], mod_ref[0, 0:1, :]).astype(BF16)
    _conv_fill_window(u_ref, win_ref, pl.program_id(1), pl.num_programs(1), tm)
    branch_refs = {0: ret_ref, 2: na_ref, 3: swa_ref}
    width = D_MODEL // MERGE_COL_SPLITS
    col_blocks = [slice(c * width, (c + 1) * width) for c in range(MERGE_COL_SPLITS)]
    slots = [(i, c) for c in range(MERGE_COL_SPLITS) for i in (0, 2, 3, 1)]
    groups = list(range(0, tm, CONV_ROWS))
    after_slot = [[] for _ in slots]
    for g, r in enumerate(groups):
        after_slot[1 + g * (len(slots) - 1) // len(groups)].append(r)
    merged = [None] * MERGE_COL_SPLITS
    conv_gate = [None] * MERGE_COL_SPLITS
    tie = None
    for n, (i, c) in enumerate(slots):
        gate = jax.nn.sigmoid(_dot(h, wm_ref[i, :, col_blocks[c]]))
        if i in branch_refs:
            br = branch_refs[i][0]
            term = gate * _dot(br if tie is None else br + tie, wb_ref[i, :, col_blocks[c]])
            merged[c] = term if merged[c] is None else merged[c] + term
        else:
            conv_gate[c] = gate
        for r in after_slot[n]:
            rows = _conv_rows(win_ref, cw_ref, cb_ref, clg_ref, clb_ref, r)
            conv_ref[r:r + CONV_ROWS, :] = rows
            tie = _zero_tied_to(rows)
    merged = jnp.concatenate(merged, axis=1) + jnp.concatenate(conv_gate, axis=1) * _dot(conv_ref[...], wb_ref[1])
    o_ref[0] = x + mod_ref[0, 2:3, :] * _dot(merged.astype(BF16), wo_ref[...])


def _merge_call(x, mod, gain, ret, u, na, swa, conv_params, w_merge, w_branch, w_out):
    B, T, _ = x.shape
    tm = TM_MERGE
    tok = lambda w: pl.BlockSpec((1, tm, w), lambda b, i: (b, i, 0))
    whole = pl.BlockSpec(memory_space=pltpu.VMEM)
    dw_kernel, dw_bias, ln_gain, ln_bias = conv_params
    row = lambda a: a.reshape(1, BRANCH_WIDTH).astype(F32)
    taps = jnp.broadcast_to(dw_kernel.astype(F32)[:, None, :], (CONV_WIDTH, SUBLANES, BRANCH_WIDTH))
    return pl.pallas_call(
        _merge_kernel,
        grid=(B, T // tm),
        in_specs=[tok(D_MODEL), pl.BlockSpec((1, 6, D_MODEL), lambda b, i: (b, 0, 0)), _resident((1, D_MODEL)),
                  tok(BRANCH_WIDTH), pl.BlockSpec((1, T, BRANCH_WIDTH), lambda b, i: (b, 0, 0)),
                  tok(BRANCH_WIDTH), tok(BRANCH_WIDTH),
                  _resident((CONV_WIDTH, SUBLANES, BRANCH_WIDTH)),
                  _resident((1, BRANCH_WIDTH)), _resident((1, BRANCH_WIDTH)), _resident((1, BRANCH_WIDTH)),
                  whole, whole, whole],
        out_specs=tok(D_MODEL),
        out_shape=jax.ShapeDtypeStruct((B, T, D_MODEL), F32),
        scratch_shapes=[pltpu.VMEM((SUBLANES, tm + 2 * CONV_HALO - SUBLANES, BRANCH_WIDTH), F32),
                        pltpu.VMEM((tm, BRANCH_WIDTH), BF16)],
        compiler_params=_params(("arbitrary", "arbitrary")),
        name="merge",
    )(x, mod, gain, ret, u, na, swa, taps, row(dw_bias), row(ln_gain), row(ln_bias), w_merge, w_branch, w_out)


def _ffn_kernel(x_ref, mod_ref, gain_ref, fin_ref, w1_ref, w2_ref, o_ref, *, final_norm):
    x = x_ref[0]
    h = _modulated_rmsnorm(x, gain_ref[...], mod_ref[0, 4:5, :], mod_ref[0, 3:4, :]).astype(BF16)
    ff = None
    for c in range(0, D_FF, FFN_CHUNK):
        a = jnp.maximum(_dot(h, w1_ref[:, c:c + FFN_CHUNK]), 0.0)
        part = _dot((a * a).astype(BF16), w2_ref[c:c + FFN_CHUNK, :])
        ff = part if ff is None else ff + part
    y = x + mod_ref[0, 5:6, :] * ff
    if final_norm:
        y = y * lax.rsqrt(jnp.mean(y * y, axis=-1, keepdims=True) + NORM_EPS) * fin_ref[...]
    o_ref[0] = y


def _ffn_call(x, mod, gain, final_gain, w1, w2, final_norm):
    B, T, _ = x.shape
    tm = TM_FFN
    tok = pl.BlockSpec((1, tm, D_MODEL), lambda b, i: (b, i, 0))
    whole = pl.BlockSpec(memory_space=pltpu.VMEM)
    return pl.pallas_call(
        functools.partial(_ffn_kernel, final_norm=final_norm),
        grid=(B, T // tm),
        in_specs=[tok, pl.BlockSpec((1, 6, D_MODEL), lambda b, i: (b, 0, 0)), _resident((1, D_MODEL)),
                  _resident((1, D_MODEL)), whole, whole],
        out_specs=tok,
        out_shape=jax.ShapeDtypeStruct((B, T, D_MODEL), F32),
        compiler_params=_params(("arbitrary", "arbitrary")),
        name="ffn",
    )(x, mod, gain, final_gain, w1, w2)


def _rotary_tables(T):
    half = HEAD_DIM // 2
    inv = (ROPE_BASE ** (-np.arange(half, dtype=np.float32) / half)).astype(np.float32)
    ang = (np.arange(T, dtype=np.float32)[:, None] * inv[None, :]).astype(np.float32).astype(np.float64)
    lane = np.arange(LANES) % HEAD_DIM
    cos = np.cos(ang)[:, lane % half]
    sin = np.sin(ang)[:, lane % half] * np.where(lane < half, -1.0, 1.0)[None, :]
    return jnp.asarray(cos, F32), jnp.asarray(sin, F32)


def _trunk(x, mods, layer_params, shared):
    B, T, _ = x.shape
    cos_t, sin_t = _rotary_tables(T)
    rows = T // GRID_W
    for l, lp in enumerate(layer_params):
        mod = mods[l]
        ret_in, u, naq, nak, nav, sq, sk, sv = _proj_call(x, mod, lp["gain1"], lp["w_in"], cos_t, sin_t)
        ret = _ret_call(ret_in, lp["ret_tables"], lp["ret_gn_gain"])
        na = _na_call(naq, nak, nav, lp["na_tables"][rows])
        swa = _swa_call(sq, sk, sv, shared["swa_table"], lp["swa_sink"])
        conv_params = (lp["conv_dw_kernel"], lp["conv_dw_bias"], lp["conv_ln_gain"], lp["conv_ln_bias"])
        x = _merge_call(x, mod, lp["gain1"], ret, u, na, swa, conv_params, lp["w_merge"], lp["w_branch"], lp["w_out"])
        x = _ffn_call(x, mod, lp["gain2"], shared["final_gain"], lp["w_ff1"], lp["w_ff2"], l == len(layer_params) - 1)
    return x


def kernel(x_prompt, x_sample, c_prompt, c_sample, w_ada, b_ada, norm_gain, w_in, ret_decay_logit, ret_gn_gain,
           conv_dw_kernel, conv_dw_bias, conv_ln_gain, conv_ln_bias, na_rpb, swa_sink, t5_bias,
           w_branch, w_merge, w_out, w_ff1, w_ff2, final_gain):
    nbp = c_prompt.shape[0]
    c_all = jnp.concatenate([c_prompt, c_sample], axis=0)
    mods = _ada_call(c_all, w_ada, b_ada.astype(F32))
    mods = mods.reshape(DEPTH, c_all.shape[0], 6, D_MODEL)
    row_counts = sorted({x_prompt.shape[1] // GRID_W, x_sample.shape[1] // GRID_W})
    layer_params = []
    for l in range(DEPTH):
        layer_params.append(dict(
            gain1=norm_gain[l, 0].reshape(1, D_MODEL).astype(F32),
            gain2=norm_gain[l, 1].reshape(1, D_MODEL).astype(F32),
            w_in=w_in[l].astype(BF16),
            ret_tables=_ret_tables(ret_decay_logit[l]),
            ret_gn_gain=ret_gn_gain[l],
            conv_dw_kernel=conv_dw_kernel[l], conv_dw_bias=conv_dw_bias[l],
            conv_ln_gain=conv_ln_gain[l], conv_ln_bias=conv_ln_bias[l],
            na_tables=_na_bias_tables(na_rpb[l], row_counts),
            swa_sink=swa_sink[l],
            w_merge=w_merge[l].astype(BF16), w_branch=w_branch[l].astype(BF16), w_out=w_out[l].astype(BF16),
            w_ff1=w_ff1[l].astype(BF16), w_ff2=w_ff2[l].astype(BF16),
        ))
    shared = dict(swa_table=_swa_bias_table(t5_bias), final_gain=final_gain.reshape(1, D_MODEL).astype(F32))
    y_prompt = _trunk(x_prompt, [m[:nbp] for m in mods], layer_params, shared)
    y_sample = _trunk(x_sample, [m[nbp:] for m in mods], layer_params, shared)
    return (y_prompt, y_sample)
```

```python
import functools
import math

import jax
import jax.numpy as jnp
import numpy as np
from jax import lax
from jax.experimental import pallas as pl
from jax.experimental.pallas import tpu as pltpu

F32 = jnp.float32
BF16 = jnp.bfloat16

D_MODEL = 1024
DEPTH = 2
HEAD_DIM = 64
BRANCH_WIDTH = 256
RET_CHUNK = 128
RET_CHUNKS_PER_STEP = 16
ROPE_BASE = 10000.0
CONV_WIDTH = 31
CONV_HALO = 16
GRID_W = 64
NA_WIN_ROWS = 8
NA_WIN_COLS = 16
NA_QROWS = 2
NA_BLOCKS_PER_STEP = 16
SWA_WINDOW = 128
SWA_BLOCK = 128
T5_BUCKETS = 32
T5_MAX_DIST = 128
D_FF = 4 * D_MODEL
NORM_EPS = 1e-6
NEG_INF = -1e30
LOG2E = 1.4426950408889634
SWA_KV_WIDTH = 128
RET_COL0 = 0
CONV_COL0 = RET_COL0 + 4 * BRANCH_WIDTH
NA_COL0 = CONV_COL0 + 2 * BRANCH_WIDTH
SWA_COL0 = NA_COL0 + 3 * BRANCH_WIDTH
IN_COLS = SWA_COL0 + BRANCH_WIDTH + 2 * SWA_KV_WIDTH
LANES = 128
SUBLANES = 8

TM_PROJ = 1024
PROJ_ROW_SPLITS = 4
TM_MERGE = 512
MERGE_COL_SPLITS = 4
FFN_CHUNK = 1024
CONV_ROWS = 32
TQ_SWA = 2048
VMEM_LIMIT = 62 * 1024 * 1024


def _params(sem, flags=None):
    return pltpu.CompilerParams(dimension_semantics=sem, vmem_limit_bytes=VMEM_LIMIT, flags=flags)


def _resident(shape):
    nd = len(shape)
    return pl.BlockSpec(shape, lambda *_: (0,) * nd)


def _modulated_rmsnorm(x, gain, scale, shift):
    y = x * lax.rsqrt(jnp.mean(x * x, axis=-1, keepdims=True) + NORM_EPS)
    return (y * gain) * (1.0 + scale) + shift


def _dot(a, b):
    return jnp.dot(a, b, preferred_element_type=F32)


def _dot_nt(a, b):
    return lax.dot_general(a, b, (((1,), (1,)), ((), ())), preferred_element_type=F32)


def _dot_tn(a, b):
    return lax.dot_general(a, b, (((0,), (0,)), ((), ())), preferred_element_type=F32)


def _zero_tied_to(v):
    bits = lax.bitcast_convert_type(v[0:1, :].astype(F32), jnp.uint32)
    zero = lax.shift_right_logical(lax.shift_right_logical(bits, jnp.uint32(16)), jnp.uint32(16))
    return zero.astype(F32).astype(BF16)


def _dot_split(x, w):
    hi = x.astype(BF16)
    lo = (x - hi.astype(F32)).astype(BF16)
    return _dot(hi, w) + _dot(lo, w)


def _ada_kernel(c_ref, w_ref, b_ref, o_ref):
    c = c_ref[...]
    a = (c * jax.nn.sigmoid(c)).astype(BF16)
    o_ref[0] = _dot(a, w_ref[0].astype(BF16)) + b_ref[0]


def _ada_call(c_all, w_ada, b_ada):
    nb = c_all.shape[0]
    return pl.pallas_call(
        _ada_kernel,
        grid=(DEPTH, 6),
        in_specs=[
            pl.BlockSpec((nb, D_MODEL), lambda l, j: (0, 0)),
            pl.BlockSpec((1, D_MODEL, D_MODEL), lambda l, j: (l, 0, j)),
            pl.BlockSpec((1, 1, D_MODEL), lambda l, j: (l, 0, j)),
        ],
        out_specs=pl.BlockSpec((1, nb, D_MODEL), lambda l, j: (l, 0, j)),
        out_shape=jax.ShapeDtypeStruct((DEPTH, nb, 6 * D_MODEL), F32),
        compiler_params=_params(("arbitrary", "arbitrary")),
        name="ada",
    )(c_all, w_ada, b_ada.reshape(DEPTH, 1, 6 * D_MODEL))


def _proj_kernel(x_ref, mod_ref, gain_ref, w_ref, cos_ref, sin_ref,
                 ret_ref, u_ref, naq_ref, nak_ref, nav_ref, sq_ref, sk_ref, sv_ref):
    tm = x_ref.shape[1]
    W = BRANCH_WIDTH
    half = tm // PROJ_ROW_SPLITS
    lane = lax.broadcasted_iota(jnp.int32, (half, LANES), 1)
    first_half = (lane & (HEAD_DIM // 2)) == 0
    qscale = HEAD_DIM ** -0.5
    blocks = [slice(n * half, (n + 1) * half) for n in range(PROJ_ROW_SPLITS)]
    hs = [_modulated_rmsnorm(x_ref[0, rows, :], gain_ref[...], mod_ref[0, 1:2, :], mod_ref[0, 0:1, :]).astype(BF16)
          for rows in blocks]
    for rows, h in zip(blocks, hs):
        cos = cos_ref[rows, :]
        sin = sin_ref[rows, :]

        def rot(y):
            sw = jnp.where(first_half, pltpu.roll(y, LANES - HEAD_DIM // 2, 1), pltpu.roll(y, HEAD_DIM // 2, 1))
            return y * cos + sw * sin

        y = _dot(h, w_ref[:, RET_COL0:RET_COL0 + 4 * W])
        for t in range(W // LANES):
            ret_ref[0, rows, t * LANES:(t + 1) * LANES] = (rot(y[:, t * LANES:(t + 1) * LANES]) * qscale).astype(BF16)
        for t in range(W // LANES, 2 * W // LANES):
            ret_ref[0, rows, t * LANES:(t + 1) * LANES] = rot(y[:, t * LANES:(t + 1) * LANES]).astype(BF16)
        ret_ref[0, rows, 2 * W:4 * W] = y[:, 2 * W:4 * W].astype(BF16)
        y = _dot(h, w_ref[:, CONV_COL0:CONV_COL0 + 2 * W])
        u_ref[0, rows, :] = (y[:, 0:W] * jax.nn.sigmoid(y[:, W:2 * W])).astype(BF16)
        y = _dot(h, w_ref[:, NA_COL0:NA_COL0 + 3 * W])
        naq_ref[0, rows, :] = (y[:, 0:W] * (qscale * LOG2E)).astype(BF16)
        nak_ref[0, rows, :] = y[:, W:2 * W].astype(BF16)
        nav_ref[0, rows, :] = y[:, 2 * W:3 * W].astype(BF16)
        y = _dot(h, w_ref[:, SWA_COL0:IN_COLS])
        sq_ref[0, rows, :] = (y[:, 0:W] * (qscale * LOG2E)).astype(BF16)
        sk_ref[0, rows, :] = y[:, W:W + SWA_KV_WIDTH].astype(BF16)
        sv_ref[0, rows, :] = y[:, W + SWA_KV_WIDTH:W + 2 * SWA_KV_WIDTH].astype(BF16)


def _proj_call(x, mod, gain, w_in, cos_t, sin_t):
    B, T, _ = x.shape
    tm = TM_PROJ
    tok = lambda w: pl.BlockSpec((1, tm, w), lambda b, i: (b, i, 0))
    W = BRANCH_WIDTH
    widths = (4 * W, W, W, W, W, W, SWA_KV_WIDTH, SWA_KV_WIDTH)
    return pl.pallas_call(
        _proj_kernel,
        grid=(B, T // tm),
        in_specs=[
            tok(D_MODEL),
            pl.BlockSpec((1, 6, D_MODEL), lambda b, i: (b, 0, 0)),
            _resident((1, D_MODEL)),
            _resident((D_MODEL, IN_COLS)),
            pl.BlockSpec((tm, LANES), lambda b, i: (i, 0)),
            pl.BlockSpec((tm, LANES), lambda b, i: (i, 0)),
        ],
        out_specs=[tok(w) for w in widths],
        out_shape=[jax.ShapeDtypeStruct((B, T, w), BF16) for w in widths],
        compiler_params=_params(("arbitrary", "arbitrary")),
        name="proj",
    )(x, mod, gain, w_in, cos_t, sin_t)


def _ret_kernel(x_ref, dec_ref, xif_ref, xib_ref, zf_ref, zb_ref, gf_ref, gb_ref, bd_ref, avg_ref, gn_ref,
                o_ref, rf_ref, rb_ref, stash_ref):
    p = pl.program_id(1)
    n = pl.program_id(2)
    nsteps = pl.num_programs(2)
    C = RET_CHUNK
    W = BRANCH_WIDTH
    G = x_ref.shape[1] // C
    heads = W // HEAD_DIM

    def chunk_kv(j):
        rows = slice(j * C, (j + 1) * C)
        return x_ref[0, rows, W:2 * W], x_ref[0, rows, 2 * W:3 * W]

    def chunk_state(k, v, zeta):
        return _dot_tn((k.astype(F32) * zeta).astype(BF16), v) * bd_ref[...]

    @pl.when(p == 0)
    def _forward_states():
        @pl.when(n == 0)
        def _():
            rf_ref[...] = jnp.zeros_like(rf_ref)

        updates = [chunk_state(*chunk_kv(j), zf_ref[...]) for j in range(G)]
        rf = rf_ref[...]
        for j in range(G):
            stash_ref[n * G + j] = rf.astype(BF16)
            rf = gf_ref[...] * rf + updates[j]
        rf_ref[...] = rf

    @pl.when(p == 1)
    def _outputs():
        @pl.when(n == 0)
        def _():
            rb_ref[...] = jnp.zeros_like(rb_ref)

        first_chunk = (nsteps - 1 - n) * G
        lane_head = lax.broadcasted_iota(jnp.int32, (C, W), 1) // HEAD_DIM

        def head_rows(a):
            return jnp.concatenate([jnp.where(lane_head == hh, a, jnp.zeros_like(a)) for hh in range(heads)], axis=0)

        qs = [x_ref[0, j * C:(j + 1) * C, 0:W] for j in range(G)]
        weights = [(_dot_nt(qs[j], head_rows(chunk_kv(j)[0])) * dec_ref[...]).astype(BF16) for j in range(G)]
        updates = [chunk_state(*chunk_kv(j), zb_ref[...]) for j in range(G)]
        rb = rb_ref[...]
        later = [None] * G
        for j in reversed(range(G)):
            later[j] = rb.astype(BF16)
            rb = gb_ref[...] * rb + updates[j]
        rb_ref[...] = rb
        outs = []
        for j in range(G):
            qf = qs[j].astype(F32)
            o = _dot(weights[j], head_rows(chunk_kv(j)[1]))
            o = o + _dot((qf * xif_ref[...]).astype(BF16), stash_ref[first_chunk + j])
            outs.append(o + _dot((qf * xib_ref[...]).astype(BF16), later[j]))
        o = jnp.concatenate(outs, axis=0)
        avg = avg_ref[...]
        d = o - _dot_split(o, avg)
        var = _dot((d * d).astype(BF16), avg)
        g = x_ref[0, :, 3 * W:4 * W].astype(F32)
        o_ref[0] = (d * lax.rsqrt(var + NORM_EPS) * gn_ref[...] * (g * jax.nn.sigmoid(g))).astype(BF16)


def _ret_tables(decay_logit):
    C = RET_CHUNK
    heads = BRANCH_WIDTH // HEAD_DIM
    lg = jax.nn.log_sigmoid(decay_logit.astype(F32))
    pos = np.arange(C, dtype=np.float32)
    diff = pos[:, None] - pos[None, :]
    fwd = jnp.exp(jnp.where(diff >= 0, diff[None] * lg[0][:, None, None], -jnp.inf))
    bwd = jnp.exp(jnp.where(diff < 0, -diff[None] * lg[1][:, None, None], -jnp.inf))
    dec = (fwd + bwd).transpose(1, 0, 2).reshape(C, heads * C)
    lane_lg = jnp.repeat(lg, HEAD_DIM, axis=1)
    xif = jnp.exp((pos + 1.0)[:, None] * lane_lg[0][None, :])
    xib = jnp.exp((C - pos)[:, None] * lane_lg[1][None, :])
    zf = jnp.exp((C - 1.0 - pos)[:, None] * lane_lg[0][None, :])
    zb = jnp.exp(pos[:, None] * lane_lg[1][None, :])
    gf = jnp.exp(C * lane_lg[0])[None, :]
    gb = jnp.exp(C * lane_lg[1])[None, :]
    return dec, xif, xib, zf, zb, gf, gb


def _head_block_constants():
    hid = np.arange(BRANCH_WIDTH) // HEAD_DIM
    same = (hid[:, None] == hid[None, :])
    return jnp.asarray(same.astype(np.float32)), jnp.asarray(same.astype(np.float32) / HEAD_DIM, dtype=BF16)


def _ret_call(ret_in, tables, gn_gain):
    B, T, _ = ret_in.shape
    rows = RET_CHUNK * RET_CHUNKS_PER_STEP
    N = T // RET_CHUNK
    S = T // rows
    bd, avg = _head_block_constants()
    step_rows = lambda b, p, n: (b, jnp.where(p == 0, n, S - 1 - n), 0)
    out_rows = lambda b, p, n: (b, jnp.where(p == 0, S - 1, S - 1 - n), 0)
    consts = list(tables) + [bd, avg, gn_gain.reshape(1, BRANCH_WIDTH).astype(F32)]
    return pl.pallas_call(
        _ret_kernel,
        grid=(B, 2, S),
        in_specs=[pl.BlockSpec((1, rows, 4 * BRANCH_WIDTH), step_rows)] + [_resident(t.shape) for t in consts],
        out_specs=pl.BlockSpec((1, rows, BRANCH_WIDTH), out_rows),
        out_shape=jax.ShapeDtypeStruct((B, T, BRANCH_WIDTH), BF16),
        scratch_shapes=[
            pltpu.VMEM((BRANCH_WIDTH, BRANCH_WIDTH), F32),
            pltpu.VMEM((BRANCH_WIDTH, BRANCH_WIDTH), F32),
            pltpu.VMEM((N, BRANCH_WIDTH, BRANCH_WIDTH), BF16),
        ],
        compiler_params=_params(("arbitrary", "arbitrary", "arbitrary")),
        name="ret",
    )(ret_in, *consts)


def _conv_fill_window(u_ref, win_ref, i, nt, tt):
    T = u_ref.shape[1]
    t0 = pl.multiple_of(i * tt, tt)
    left_start = pl.multiple_of(jnp.maximum(t0 - CONV_HALO, 0), CONV_HALO)
    right_start = pl.multiple_of(jnp.minimum(t0 + tt, T - CONV_HALO), CONV_HALO)
    left = u_ref[0, pl.ds(left_start, CONV_HALO), :].astype(F32) * (i > 0).astype(F32)
    right = u_ref[0, pl.ds(right_start, CONV_HALO), :].astype(F32) * (i < nt - 1).astype(F32)
    win = jnp.concatenate([left, u_ref[0, pl.ds(t0, tt), :].astype(F32), right], axis=0)
    span = win_ref.shape[1]
    for s in range(SUBLANES):
        win_ref[s] = win[s:s + span]


def _conv_rows(win_ref, w_ref, b_ref, lg_ref, lb_ref, r):
    off = CONV_HALO - CONV_WIDTH // 2
    acc = jnp.zeros((CONV_ROWS // SUBLANES, SUBLANES, BRANCH_WIDTH), F32)
    for tap in range(CONV_WIDTH):
        s, a = (tap + off) % SUBLANES, (tap + off) // SUBLANES * SUBLANES
        rows = win_ref[s, r + a:r + a + CONV_ROWS, :].reshape(acc.shape)
        acc = acc + rows * w_ref[tap][None]
    acc = acc.reshape(CONV_ROWS, BRANCH_WIDTH) + b_ref[...]
    mu = jnp.mean(acc, axis=-1, keepdims=True)
    d = acc - mu
    var = jnp.mean(d * d, axis=-1, keepdims=True)
    y = d * lax.rsqrt(var + NORM_EPS) * lg_ref[...] + lb_ref[...]
    return (y * jax.nn.sigmoid(y)).astype(BF16)


def _na_layout(rows):
    R = NA_QROWS
    KR = R + NA_WIN_ROWS
    nblk = rows // R
    qc = np.arange(GRID_W)
    ws = np.clip(qc - NA_WIN_COLS // 2, 0, GRID_W - NA_WIN_COLS)
    col_ok = (qc[None, :] >= ws[:, None]) & (qc[None, :] < ws[:, None] + NA_WIN_COLS)
    dc = qc[None, :] - qc[:, None] + NA_WIN_COLS - 1
    n_dr = 2 * NA_WIN_ROWS - 1
    variants, var_of_block, kr0s = [], [], []
    for j in range(nblk):
        r0 = j * R
        kr0 = int(np.clip(r0 - NA_WIN_ROWS // 2, 0, rows - KR))
        r = r0 + np.arange(R)
        start = np.clip(r - NA_WIN_ROWS // 2, 0, rows - NA_WIN_ROWS)
        kr = kr0 + np.arange(KR)
        row_ok = (kr[None, :] >= start[:, None]) & (kr[None, :] < start[:, None] + NA_WIN_ROWS)
        tile = np.where(row_ok, kr[None, :] - r[:, None] + NA_WIN_ROWS - 1, n_dr)
        key = tile.tobytes()
        if key not in [v[0] for v in variants]:
            variants.append((key, tile))
        var_of_block.append([v[0] for v in variants].index(key))
        kr0s.append(kr0)
    tiles = np.stack([v[1] for v in variants])
    onehot = (dc[None] == np.arange(2 * NA_WIN_COLS - 1)[:, None, None]) & col_ok[None]
    return tiles, onehot, col_ok, np.asarray(var_of_block, np.int32), np.asarray(kr0s, np.int32)


def _na_bias_table(rpb, rows):
    tiles, onehot, col_ok, _, _ = _na_layout(rows)
    H, n_dr, n_dc = rpb.shape
    oh = jnp.asarray(onehot.reshape(n_dc, GRID_W * GRID_W), F32)
    tz = jnp.dot(rpb.astype(F32).reshape(H * n_dr, n_dc), oh, precision=lax.Precision.HIGHEST)
    tz = jnp.where(col_ok[None, None], tz.reshape(H, n_dr, GRID_W, GRID_W) * LOG2E, NEG_INF)
    tz = jnp.concatenate([tz, jnp.full((H, 1, GRID_W, GRID_W), NEG_INF, F32)], axis=1)
    nv, R, KR = tiles.shape
    per_variant = []
    for vv in range(nv):
        slabs = [jnp.concatenate([tz[:, int(tiles[vv, rl, kl])] for kl in range(KR)], axis=-1) for rl in range(R)]
        per_variant.append(jnp.concatenate(slabs, axis=1))
    return jnp.stack(per_variant)


def _na_bias_tables(rpb, row_counts):
    by_layout, out = {}, {}
    for rows in row_counts:
        key = _na_layout(rows)[0].tobytes()
        if key not in by_layout:
            by_layout[key] = _na_bias_table(rpb, rows)
        out[rows] = by_layout[key]
    return out


def _na_kernel(var_ref, kr0_ref, q_ref, k_ref, v_ref, tab_ref, o_ref):
    step = pl.program_id(1)
    nq = q_ref.shape[1] // NA_BLOCKS_PER_STEP
    nk = tab_ref.shape[3]
    low = lax.broadcasted_iota(jnp.int32, (nq, LANES), 1) < HEAD_DIM
    units = [(sb, pair) for sb in range(NA_BLOCKS_PER_STEP) for pair in range(BRANCH_WIDTH // LANES)]
    blocks = [step * NA_BLOCKS_PER_STEP + sb for sb in range(NA_BLOCKS_PER_STEP)]
    kstart = [pl.multiple_of(kr0_ref[blk] * GRID_W, GRID_W) for blk in blocks]
    variant = [var_ref[blk] for blk in blocks]
    scores = []
    for sb, pair in units:
        cols = slice(pair * LANES, (pair + 1) * LANES)
        q2 = q_ref[0, sb * nq:(sb + 1) * nq, cols]
        zero = jnp.zeros_like(q2)
        qs = jnp.concatenate([jnp.where(low, q2, zero), jnp.where(low, zero, q2)], axis=0)
        scores.append(_dot_nt(qs, k_ref[0, pl.ds(kstart[sb], nk), cols]) + tab_ref[variant[sb], pair])
    weights = [jnp.exp2(s - jnp.max(s, axis=-1, keepdims=True)).astype(BF16) for s in scores]
    ones = jnp.ones((nk, LANES), BF16)
    for (sb, pair), e in zip(units, weights):
        cols = slice(pair * LANES, (pair + 1) * LANES)
        o2 = _dot(e, jnp.concatenate([v_ref[0, pl.ds(kstart[sb], nk), cols], ones], axis=1))
        o2 = o2[:, :LANES] * (1.0 / o2[:, LANES:])
        o_ref[0, sb * nq:(sb + 1) * nq, cols] = jnp.where(low, o2[:nq], o2[nq:]).astype(BF16)


def _na_call(q, k, v, tab):
    B, T, _ = q.shape
    rows = T // GRID_W
    _, _, _, var_of_block, kr0s = _na_layout(rows)
    nq = NA_QROWS * GRID_W
    per_step = NA_BLOCKS_PER_STEP
    nblk = rows // NA_QROWS
    assert nblk % per_step == 0
    nv, heads, _, nk = tab.shape
    tab = tab.reshape(nv, heads // 2, 2 * nq, nk)
    grid_spec = pltpu.PrefetchScalarGridSpec(
        num_scalar_prefetch=2,
        grid=(B, nblk // per_step),
        in_specs=[
            pl.BlockSpec((1, per_step * nq, BRANCH_WIDTH), lambda b, j, var, kr0: (b, j, 0)),
            pl.BlockSpec((1, T, BRANCH_WIDTH), lambda b, j, var, kr0: (b, 0, 0)),
            pl.BlockSpec((1, T, BRANCH_WIDTH), lambda b, j, var, kr0: (b, 0, 0)),
            pl.BlockSpec(memory_space=pltpu.VMEM),
        ],
        out_specs=pl.BlockSpec((1, per_step * nq, BRANCH_WIDTH), lambda b, j, var, kr0: (b, j, 0)),
    )
    return pl.pallas_call(
        _na_kernel,
        grid_spec=grid_spec,
        out_shape=jax.ShapeDtypeStruct((B, T, BRANCH_WIDTH), BF16),
        compiler_params=_params(("arbitrary", "arbitrary")),
        name="na",
    )(jnp.asarray(var_of_block), jnp.asarray(kr0s), q, k, v, tab)


def _t5_bucket(rel):
    half = T5_BUCKETS // 2
    exact = half // 2
    n = np.abs(rel)
    large = exact + (np.log(np.maximum(n, 1) / exact) / math.log(T5_MAX_DIST / exact) * (half - exact)).astype(np.int64)
    large = np.minimum(large, half - 1)
    return (rel > 0).astype(np.int64) * half + np.where(n < exact, n, large)


def _swa_bias_table(t5_bias):
    kpos = np.arange(3 * SWA_BLOCK) - SWA_BLOCK
    rel = kpos[None, :] - np.arange(SWA_BLOCK)[:, None]
    onehot = (_t5_bucket(rel)[None] == np.arange(T5_BUCKETS)[:, None, None])
    oh = jnp.asarray(onehot.reshape(T5_BUCKETS, -1), F32)
    tab = jnp.dot(t5_bias.astype(F32).T, oh, precision=lax.Precision.HIGHEST).reshape((-1,) + rel.shape)
    tab = jnp.where((np.abs(rel) <= SWA_WINDOW)[None], tab * LOG2E, NEG_INF)
    before = (kpos < 0)[None, None, :]
    after = (kpos >= SWA_BLOCK)[None, None, :]
    tabs = jnp.stack([jnp.where(before, NEG_INF, tab), tab, jnp.where(after, NEG_INF, tab)])
    hq = tab.shape[0]
    return tabs.reshape(3, hq // 2, 2 * SWA_BLOCK, 3 * SWA_BLOCK)


def _swa_kernel(sink_ref, q_ref, k_ref, v_ref, tab_ref, o_ref):
    n = pl.program_id(1)
    tq = q_ref.shape[1]
    T = k_ref.shape[1]
    nb = T // SWA_BLOCK
    per_step = tq // SWA_BLOCK
    low = lax.broadcasted_iota(jnp.int32, (SWA_BLOCK, LANES), 1) < HEAD_DIM
    first_rows = lax.broadcasted_iota(jnp.int32, (2 * SWA_BLOCK, 1), 0) < SWA_BLOCK
    units = [(i, hk) for i in range(per_step) for hk in range(2)]

    def band(ref, blk):
        prev_start = pl.multiple_of(jnp.maximum(blk - 1, 0) * SWA_BLOCK, SWA_BLOCK)
        cur_start = pl.multiple_of(blk * SWA_BLOCK, SWA_BLOCK)
        next_start = pl.multiple_of(jnp.minimum(blk + 1, nb - 1) * SWA_BLOCK, SWA_BLOCK)
        return jnp.concatenate([ref[0, pl.ds(prev_start, SWA_BLOCK), :],
                                ref[0, pl.ds(cur_start, SWA_BLOCK), :],
                                ref[0, pl.ds(next_start, SWA_BLOCK), :]], axis=0)

    scores = []
    for i, hk in units:
        blk = n * per_step + i
        variant = jnp.where(blk == 0, 0, jnp.where(blk == nb - 1, 2, 1))
        q2 = q_ref[0, i * SWA_BLOCK:(i + 1) * SWA_BLOCK, hk * LANES:(hk + 1) * LANES]
        q2r = pltpu.roll(q2, HEAD_DIM, 1)
        zero = jnp.zeros_like(q2)
        if hk == 0:
            qs = jnp.concatenate([jnp.where(low, q2, zero), jnp.where(low, q2r, zero)], axis=0)
        else:
            qs = jnp.concatenate([jnp.where(low, zero, q2r), jnp.where(low, zero, q2)], axis=0)
        scores.append(_dot_nt(qs, band(k_ref, blk)) + tab_ref[variant, hk])
    weights = []
    for (i, hk), s in zip(units, scores):
        sink = jnp.where(first_rows, sink_ref[2 * hk], sink_ref[2 * hk + 1]) * LOG2E
        m = jnp.maximum(jnp.max(s, axis=-1, keepdims=True), sink)
        weights.append((jnp.exp2(s - m).astype(BF16), jnp.exp2(sink - m)))
    ones = jnp.ones((3 * SWA_BLOCK, LANES), BF16)
    for (i, hk), (e, sink_weight) in zip(units, weights):
        o2 = _dot(e, jnp.concatenate([band(v_ref, n * per_step + i), ones], axis=1))
        o2 = o2[:, :LANES] * (1.0 / (o2[:, LANES:] + sink_weight))
        top, bot = o2[:SWA_BLOCK], o2[SWA_BLOCK:]
        if hk == 0:
            out = jnp.where(low, top, pltpu.roll(bot, HEAD_DIM, 1))
        else:
            out = jnp.where(low, pltpu.roll(top, HEAD_DIM, 1), bot)
        o_ref[0, i * SWA_BLOCK:(i + 1) * SWA_BLOCK, hk * LANES:(hk + 1) * LANES] = out.astype(BF16)


def _swa_call(q, k, v, tab, sink):
    B, T, _ = q.shape
    tq = TQ_SWA
    assert T // SWA_BLOCK >= 2
    return pl.pallas_call(
        _swa_kernel,
        grid=(B, T // tq),
        in_specs=[
            pl.BlockSpec(memory_space=pltpu.SMEM),
            pl.BlockSpec((1, tq, BRANCH_WIDTH), lambda b, i: (b, i, 0)),
            pl.BlockSpec((1, T, LANES), lambda b, i: (b, 0, 0)),
            pl.BlockSpec((1, T, LANES), lambda b, i: (b, 0, 0)),
            _resident(tab.shape),
        ],
        out_specs=pl.BlockSpec((1, tq, BRANCH_WIDTH), lambda b, i: (b, i, 0)),
        out_shape=jax.ShapeDtypeStruct((B, T, BRANCH_WIDTH), BF16),
        compiler_params=_params(("arbitrary", "arbitrary")),
        name="swa",
    )(sink.astype(F32), q, k, v, tab)


def _merge_kernel(x_ref, mod_ref, gain_ref, ret_ref, u_ref, na_ref, swa_ref, cw_ref, cb_ref, clg_ref, clb_ref,
                  wm_ref, wb_ref, wo_ref, gain2_ref, fin_ref, w1_ref, w2_ref, o_ref, win_ref, conv_ref, *, final_norm):
    tm = x_ref.shape[1]
    x = x_ref[0]
    h = _modulated_rmsnorm(x, gain_ref[...], mod_ref[0, 1:2, :], mod_ref[0, 0:1, :]).astype(BF16)
    _conv_fill_window(u_ref, win_ref, pl.program_id(1), pl.num_programs(1), tm)
    branch_refs = {0: ret_ref, 2: na_ref, 3: swa_ref}
    width = D_MODEL // MERGE_COL_SPLITS
    col_blocks = [slice(c * width, (c + 1) * width) for c in range(MERGE_COL_SPLITS)]
    slots = [(i, c) for c in range(MERGE_COL_SPLITS) for i in (0, 2, 3, 1)]
    groups = list(range(0, tm, CONV_ROWS))
    after_slot = [[] for _ in slots]
    for g, r in enumerate(groups):
        after_slot[1 + g * (len(slots) - 1) // len(groups)].append(r)
    merged = [None] * MERGE_COL_SPLITS
    conv_gate = [None] * MERGE_COL_SPLITS
    tie = None
    for n, (i, c) in enumerate(slots):
        gate = jax.nn.sigmoid(_dot(h, wm_ref[i, :, col_blocks[c]]))
        if i in branch_refs:
            br = branch_refs[i][0]
            term = gate * _dot(br if tie is None else br + tie, wb_ref[i, :, col_blocks[c]])
            merged[c] = term if merged[c] is None else merged[c] + term
        else:
            conv_gate[c] = gate
        for r in after_slot[n]:
            rows = _conv_rows(win_ref, cw_ref, cb_ref, clg_ref, clb_ref, r)
            conv_ref[r:r + CONV_ROWS, :] = rows
            tie = _zero_tied_to(rows)
    merged = jnp.concatenate(merged, axis=1) + jnp.concatenate(conv_gate, axis=1) * _dot(conv_ref[...], wb_ref[1])
    x = x + mod_ref[0, 2:3, :] * _dot(merged.astype(BF16), wo_ref[...])
    h = _modulated_rmsnorm(x, gain2_ref[...], mod_ref[0, 4:5, :], mod_ref[0, 3:4, :]).astype(BF16)
    ff = None
    for c in range(0, D_FF, FFN_CHUNK):
        a = jnp.maximum(_dot(h, w1_ref[:, c:c + FFN_CHUNK]), 0.0)
        part = _dot((a * a).astype(BF16), w2_ref[c:c + FFN_CHUNK, :])
        ff = part if ff is None else ff + part
    y = x + mod_ref[0, 5:6, :] * ff
    if final_norm:
        y = y * lax.rsqrt(jnp.mean(y * y, axis=-1, keepdims=True) + NORM_EPS) * fin_ref[...]
    o_ref[0] = y


def _merge_call(x, mod, gain, ret, u, na, swa, conv_params, w_merge, w_branch, w_out,
                gain2, final_gain, w1, w2, final_norm):
    B, T, _ = x.shape
    tm = TM_MERGE
    tok = lambda w: pl.BlockSpec((1, tm, w), lambda b, i: (b, i, 0))
    whole = pl.BlockSpec(memory_space=pltpu.VMEM)
    dw_kernel, dw_bias, ln_gain, ln_bias = conv_params
    row = lambda a: a.reshape(1, BRANCH_WIDTH).astype(F32)
    taps = jnp.broadcast_to(dw_kernel.astype(F32)[:, None, :], (CONV_WIDTH, SUBLANES, BRANCH_WIDTH))
    return pl.pallas_call(
        functools.partial(_merge_kernel, final_norm=final_norm),
        grid=(B, T // tm),
        in_specs=[tok(D_MODEL), pl.BlockSpec((1, 6, D_MODEL), lambda b, i: (b, 0, 0)), _resident((1, D_MODEL)),
                  tok(BRANCH_WIDTH), pl.BlockSpec((1, T, BRANCH_WIDTH), lambda b, i: (b, 0, 0)),
                  tok(BRANCH_WIDTH), tok(BRANCH_WIDTH),
                  _resident((CONV_WIDTH, SUBLANES, BRANCH_WIDTH)),
                  _resident((1, BRANCH_WIDTH)), _resident((1, BRANCH_WIDTH)), _resident((1, BRANCH_WIDTH)),
                  whole, whole, whole, _resident((1, D_MODEL)), _resident((1, D_MODEL)), whole, whole],
        out_specs=tok(D_MODEL),
        out_shape=jax.ShapeDtypeStruct((B, T, D_MODEL), F32),
        scratch_shapes=[pltpu.VMEM((SUBLANES, tm + 2 * CONV_HALO - SUBLANES, BRANCH_WIDTH), F32),
                        pltpu.VMEM((tm, BRANCH_WIDTH), BF16)],
        compiler_params=_params(("arbitrary", "arbitrary")),
        name="merge",
    )(x, mod, gain, ret, u, na, swa, taps, row(dw_bias), row(ln_gain), row(ln_bias), w_merge, w_branch, w_out,
      gain2, final_gain, w1, w2)


def _rotary_tables(T):
    half = HEAD_DIM // 2
    inv = (ROPE_BASE ** (-np.arange(half, dtype=np.float32) / half)).astype(np.float32)
    ang = (np.arange(T, dtype=np.float32)[:, None] * inv[None, :]).astype(np.float32).astype(np.float64)
    lane = np.arange(LANES) % HEAD_DIM
    cos = np.cos(ang)[:, lane % half]
    sin = np.sin(ang)[:, lane % half] * np.where(lane < half, -1.0, 1.0)[None, :]
    return jnp.asarray(cos, F32), jnp.asarray(sin, F32)


def _trunk(x, mods, layer_params, shared):
    B, T, _ = x.shape
    cos_t, sin_t = _rotary_tables(T)
    rows = T // GRID_W
    for l, lp in enumerate(layer_params):
        mod = mods[l]
        ret_in, u, naq, nak, nav, sq, sk, sv = _proj_call(x, mod, lp["gain1"], lp["w_in"], cos_t, sin_t)
        ret = _ret_call(ret_in, lp["ret_tables"], lp["ret_gn_gain"])
        na = _na_call(naq, nak, nav, lp["na_tables"][rows])
        swa = _swa_call(sq, sk, sv, shared["swa_table"], lp["swa_sink"])
        conv_params = (lp["conv_dw_kernel"], lp["conv_dw_bias"], lp["conv_ln_gain"], lp["conv_ln_bias"])
        x = _merge_call(x, mod, lp["gain1"], ret, u, na, swa, conv_params, lp["w_merge"], lp["w_branch"], lp["w_out"],
                        lp["gain2"], shared["final_gain"], lp["w_ff1"], lp["w_ff2"], l == len(layer_params) - 1)
    return x


def kernel(x_prompt, x_sample, c_prompt, c_sample, w_ada, b_ada, norm_gain, w_in, ret_decay_logit, ret_gn_gain,
           conv_dw_kernel, conv_dw_bias, conv_ln_gain, conv_ln_bias, na_rpb, swa_sink, t5_bias,
           w_branch, w_merge, w_out, w_ff1, w_ff2, final_gain):
    nbp = c_prompt.shape[0]
    c_all = jnp.concatenate([c_prompt, c_sample], axis=0)
    mods = _ada_call(c_all, w_ada, b_ada.astype(F32))
    mods = mods.reshape(DEPTH, c_all.shape[0], 6, D_MODEL)
    row_counts = sorted({x_prompt.shape[1] // GRID_W, x_sample.shape[1] // GRID_W})
    layer_params = []
    for l in range(DEPTH):
        layer_params.append(dict(
            gain1=norm_gain[l, 0].reshape(1, D_MODEL).astype(F32),
            gain2=norm_gain[l, 1].reshape(1, D_MODEL).astype(F32),
            w_in=w_in[l].astype(BF16),
            ret_tables=_ret_tables(ret_decay_logit[l]),
            ret_gn_gain=ret_gn_gain[l],
            conv_dw_kernel=conv_dw_kernel[l], conv_dw_bias=conv_dw_bias[l],
            conv_ln_gain=conv_ln_gain[l], conv_ln_bias=conv_ln_bias[l],
            na_tables=_na_bias_tables(na_rpb[l], row_counts),
            swa_sink=swa_sink[l],
            w_merge=w_merge[l].astype(BF16), w_branch=w_branch[l].astype(BF16), w_out=w_out[l].astype(BF16),
            w_ff1=w_ff1[l].astype(BF16), w_ff2=w_ff2[l].astype(BF16),
        ))
    shared = dict(swa_table=_swa_bias_table(t5_bias), final_gain=final_gain.reshape(1, D_MODEL).astype(F32))
    y_prompt = _trunk(x_prompt, [m[:nbp] for m in mods], layer_params, shared)
    y_sample = _trunk(x_sample, [m[nbp:] for m in mods], layer_params, shared)
    return (y_prompt, y_sample)
```
